```python
import jax
import jax.numpy as jnp
from jax import lax
import numpy as np

D_MODEL = 2048
BATCH = 4
SEQ = 4096
DEPTH = 4

GRID_W = 64
CTX_LEN = 256

HEAD_DIM = 128
N_HEADS = 8
N_KV_HEADS = 2
Q_PER_KV = N_HEADS // N_KV_HEADS
ATTN_W = N_HEADS * HEAD_DIM
KV_W = N_KV_HEADS * HEAD_DIM
Q_BLOCK = 128
ROPE_THETA = 10000.0
ROPE_AXIS_DIM = HEAD_DIM // 2
ATTN_SCALE = HEAD_DIM ** -0.5

FOURIER_GROUPS = 4
FOURIER_W = FOURIER_GROUPS * HEAD_DIM

CONV_GROUPS = 4
CONV_W = CONV_GROUPS * HEAD_DIM
CONV_K = 3

GMLP_GROUPS = 4
GMLP_HEAD = HEAD_DIM
GMLP_W = GMLP_GROUPS * GMLP_HEAD
CHUNK = 128

MIX_W = ATTN_W + FOURIER_W + CONV_W + GMLP_W
SPLIT_SIZES = (ATTN_W, KV_W, KV_W, FOURIER_W, CONV_W, CONV_W, CONV_W, GMLP_W, GMLP_W)
IN_W = sum(SPLIT_SIZES)
SPLIT_POINTS = tuple(int(s) for s in np.cumsum(SPLIT_SIZES)[:-1])
OFF_K = ATTN_W
OFF_V = OFF_K + KV_W
OFF_F = OFF_V + KV_W

D_FF = 5632
FFN_CONV_K = 3

N_MOD = 6
EPS = 1e-6

kernel_name = "hybrid_parallel_groups_diffusion_block"


def rms_norm(x, g):
    xf = x.astype(jnp.float32)
    y = xf * lax.rsqrt(jnp.mean(xf * xf, axis=-1, keepdims=True) + EPS)
    return (y * g.astype(jnp.float32)).astype(x.dtype)


def layer_norm(x, g, b):
    xf = x.astype(jnp.float32)
    xc = xf - jnp.mean(xf, axis=-1, keepdims=True)
    y = xc * lax.rsqrt(jnp.mean(xc * xc, axis=-1, keepdims=True) + EPS)
    return (y * g.astype(jnp.float32) + b.astype(jnp.float32)).astype(x.dtype)


def modulate(h, shift, scale):
    return h * (1 + scale) + shift


def dwconv3(x, w):
    xp = jnp.pad(x, ((0, 0), (1, 1), (0, 0)))
    return xp[:, :-2] * w[0] + xp[:, 1:-1] * w[1] + xp[:, 2:] * w[2]


def axial_rope_tables(rows):
    row = jnp.repeat(jnp.arange(rows, dtype=jnp.float32), GRID_W)
    col = jnp.tile(jnp.arange(GRID_W, dtype=jnp.float32), rows)
    freqs = ROPE_THETA ** (-jnp.arange(0, ROPE_AXIS_DIM, 2, dtype=jnp.float32) / ROPE_AXIS_DIM)
    ang_r = row[:, None] * freqs[None, :]
    ang_c = col[:, None] * freqs[None, :]
    return (jnp.cos(ang_r), jnp.sin(ang_r), jnp.cos(ang_c), jnp.sin(ang_c))


def _rotate(xp, cos, sin):
    half = xp.shape[-1] // 2
    x1, x2 = xp[..., :half], xp[..., half:]
    cos = cos[None, :, None, :].astype(xp.dtype)
    sin = sin[None, :, None, :].astype(xp.dtype)
    return jnp.concatenate([x1 * cos - x2 * sin, x2 * cos + x1 * sin], axis=-1)


def rope_2d(x, tables):
    cos_r, sin_r, cos_c, sin_c = tables
    return jnp.concatenate([_rotate(x[..., :ROPE_AXIS_DIM], cos_r, sin_r),
                            _rotate(x[..., ROPE_AXIS_DIM:], cos_c, sin_c)], axis=-1)


def attn_kv(pk, pv, k_g):
    bsz, n = pk.shape[:2]
    k = rms_norm(pk.reshape(bsz, n, N_KV_HEADS, HEAD_DIM), k_g)
    v = pv.reshape(bsz, n, N_KV_HEADS, HEAD_DIM)
    return k, v


def gqa_attend(q, k, v):
    bsz, nq = q.shape[:2]
    qg = q.reshape(bsz, nq, N_KV_HEADS, Q_PER_KV, HEAD_DIM)
    s = jnp.einsum('bqhgd,bkhd->bhgqk', qg, k).astype(jnp.float32) * ATTN_SCALE
    p = jax.nn.softmax(s, axis=-1).astype(v.dtype)
    o = jnp.einsum('bhgqk,bkhd->bqhgd', p, v)
    return o.reshape(bsz, nq, ATTN_W)


def blocked_attention(q, k_all, v_all):
    bsz, n = q.shape[:2]
    nb = n // Q_BLOCK
    qb = q.reshape(bsz, nb, Q_BLOCK, N_HEADS, HEAD_DIM).swapaxes(0, 1)
    o = lax.map(lambda qq: gqa_attend(qq, k_all, v_all), qb)
    return o.swapaxes(0, 1).reshape(bsz, n, ATTN_W)


def fourier_mix(f):
    bsz, n = f.shape[:2]
    z = f.astype(jnp.float32).reshape(bsz, n, FOURIER_GROUPS, FOURIER_W // FOURIER_GROUPS)
    y = jnp.fft.fft2(z, axes=(1, 3), norm='ortho').real
    return y.reshape(bsz, n, FOURIER_W).astype(f.dtype)


def spatial_gating(pu, pv, ln_g, ln_b, ws, bs):
    u = jax.nn.gelu(pu)
    v = layer_norm(jax.nn.gelu(pv), ln_g, ln_b)
    bsz, n = v.shape[:2]
    vc = v.reshape(bsz, n // CHUNK, CHUNK, GMLP_GROUPS, GMLP_HEAD)
    s = jnp.einsum('gqp,bcpgd->bcqgd', ws, vc) + bs.T[None, None, :, :, None]
    return u * s.reshape(bsz, n, GMLP_W)


def token_mix(h, w_in, q_g, k_g, conv_w, gm_ln_g, gm_ln_b, gm_ws, gm_b, rope, ctx_kv):
    bsz, n = h.shape[:2]
    p = h @ w_in
    pq, pk, pv, pf, pcb, pcc, pch, pgu, pgv = jnp.split(p, SPLIT_POINTS, axis=-1)
    q = rms_norm(pq.reshape(bsz, n, N_HEADS, HEAD_DIM), q_g)
    k, v = attn_kv(pk, pv, k_g)
    if ctx_kv is None:
        attn = gqa_attend(q, k, v)
    else:
        q = rope_2d(q, rope)
        k = rope_2d(k, rope)
        k_all = jnp.concatenate([ctx_kv[0], k], axis=1)
        v_all = jnp.concatenate([ctx_kv[1], v], axis=1)
        attn = blocked_attention(q, k_all, v_all)
    four = fourier_mix(pf)
    conv = pcb * dwconv3(pcc * pch, conv_w)
    gm = spatial_gating(pgu, pgv, gm_ln_g, gm_ln_b, gm_ws, gm_b)
    mix = jnp.concatenate([attn, four, conv, gm], axis=-1)
    return mix, k, v


def conv_ffn(h, w_up, conv_w, conv_b, w_down):
    g, u = jnp.split(h @ w_up, 2, axis=-1)
    return (jax.nn.silu(dwconv3(g, conv_w) + conv_b) * u) @ w_down


def setup_inputs(seed: int = 0) -> dict:
    key = jax.random.key(seed)
    ks = jax.random.split(key, 24)
    f32 = jnp.float32

    def nrm(k, shape, s):
        return jax.random.normal(k, shape, f32) * s

    return {
        'x': nrm(ks[0], (BATCH, SEQ, D_MODEL), 1.0),
        'c': nrm(ks[1], (BATCH, D_MODEL), 1.0),
        'ctx': nrm(ks[2], (BATCH, CTX_LEN, D_MODEL), 1.0),
        'c_ctx': nrm(ks[3], (D_MODEL,), 1.0),
        'w_mod': nrm(ks[4], (DEPTH, D_MODEL, N_MOD * D_MODEL), 0.5 * D_MODEL ** -0.5),
        'b_mod': nrm(ks[5], (DEPTH, N_MOD * D_MODEL), 0.02),
        'norm1_g': 1.0 + nrm(ks[6], (DEPTH, D_MODEL), 0.02),
        'norm2_g': 1.0 + nrm(ks[7], (DEPTH, D_MODEL), 0.02),
        'w_in': nrm(ks[8], (DEPTH, D_MODEL, IN_W), D_MODEL ** -0.5),
        'q_norm_g': 1.0 + nrm(ks[9], (DEPTH, HEAD_DIM), 0.02),
        'k_norm_g': 1.0 + nrm(ks[10], (DEPTH, HEAD_DIM), 0.02),
        'conv_w': nrm(ks[11], (DEPTH, CONV_K, CONV_W), CONV_K ** -0.5),
        'gm_ln_g': 1.0 + nrm(ks[12], (DEPTH, GMLP_W), 0.02),
        'gm_ln_b': nrm(ks[13], (DEPTH, GMLP_W), 0.02),
        'gm_ws': nrm(ks[14], (DEPTH, GMLP_GROUPS, CHUNK, CHUNK), CHUNK ** -0.5),
        'gm_b': 1.0 + nrm(ks[15], (DEPTH, GMLP_GROUPS, CHUNK), 0.1),
        'w_out': nrm(ks[16], (DEPTH, MIX_W, D_MODEL), MIX_W ** -0.5),
        'w_up': nrm(ks[17], (DEPTH, D_MODEL, 2 * D_FF), D_MODEL ** -0.5),
        'ffn_conv_w': nrm(ks[18], (DEPTH, FFN_CONV_K, D_FF), FFN_CONV_K ** -0.5),
        'ffn_conv_b': nrm(ks[19], (DEPTH, D_FF), 0.02),
        'w_down': nrm(ks[20], (DEPTH, D_FF, D_MODEL), D_FF ** -0.5),
        'final_norm_g': 1.0 + nrm(ks[21], (D_MODEL,), 0.02),
    }


def reference(x, c, ctx, c_ctx, w_mod, b_mod, norm1_g, norm2_g, w_in, q_norm_g, k_norm_g,
              conv_w, gm_ln_g, gm_ln_b, gm_ws, gm_b, w_out, w_up, ffn_conv_w, ffn_conv_b,
              w_down, final_norm_g):
    n_lat = x.shape[1]
    rows = n_lat // GRID_W
    rope = axial_rope_tables(rows)
    silu_c = jax.nn.silu(c)
    silu_cc = jax.nn.silu(c_ctx)
    for l in range(DEPTH):
        mod_x = silu_c @ w_mod[l] + b_mod[l]
        mod_c = silu_cc @ w_mod[l] + b_mod[l]
        sh1, sc1, ga1, sh2, sc2, ga2 = jnp.split(mod_x[:, None, :], N_MOD, axis=-1)
        csh1, csc1, cga1, csh2, csc2, cga2 = jnp.split(mod_c, N_MOD, axis=-1)
        hc = modulate(rms_norm(ctx, norm1_g[l]), csh1, csc1)
        hx = modulate(rms_norm(x, norm1_g[l]), sh1, sc1)
        mixer_w = (w_in[l], q_norm_g[l], k_norm_g[l], conv_w[l],
                   gm_ln_g[l], gm_ln_b[l], gm_ws[l], gm_b[l])
        if l < DEPTH - 1:
            mix_c, k_c, v_c = token_mix(hc, *mixer_w, None, None)
            ctx_next = ctx + cga1 * (mix_c @ w_out[l])
            hc2 = modulate(rms_norm(ctx_next, norm2_g[l]), csh2, csc2)
            ctx_next = ctx_next + cga2 * conv_ffn(hc2, w_up[l], ffn_conv_w[l], ffn_conv_b[l], w_down[l])
        else:
            k_c, v_c = attn_kv(hc @ w_in[l][:, OFF_K:OFF_V], hc @ w_in[l][:, OFF_V:OFF_F], k_norm_g[l])
            ctx_next = ctx
        mix_x, _, _ = token_mix(hx, *mixer_w, rope, (k_c, v_c))
        x = x + ga1 * (mix_x @ w_out[l])
        hx2 = modulate(rms_norm(x, norm2_g[l]), sh2, sc2)
        x = x + ga2 * conv_ffn(hx2, w_up[l], ffn_conv_w[l], ffn_conv_b[l], w_down[l])
        ctx = ctx_next
    return rms_norm(x, final_norm_g)
```

```python
import functools
import math

import jax
import jax.numpy as jnp
from jax import lax
from jax.experimental import pallas as pl
from jax.experimental.pallas import tpu as pltpu

F32 = jnp.float32
BF16 = jnp.bfloat16

GRID_W = 64
HEAD_DIM = 128
N_HEADS = 8
N_KV_HEADS = 2
Q_PER_KV = N_HEADS // N_KV_HEADS
ATTN_W = N_HEADS * HEAD_DIM
KV_W = N_KV_HEADS * HEAD_DIM
ROPE_THETA = 10000.0
ROPE_AXIS_DIM = HEAD_DIM // 2
ATTN_SCALE = HEAD_DIM ** -0.5
GROUP_W = 4 * HEAD_DIM
CHUNK = 128
N_MOD = 6
EPS = 1e-6

OFF_K = ATTN_W
OFF_V = OFF_K + KV_W
OFF_F = OFF_V + KV_W
OFF_CB = OFF_F + GROUP_W
OFF_CC = OFF_CB + GROUP_W
OFF_CH = OFF_CC + GROUP_W
OFF_GU = OFF_CH + GROUP_W
OFF_GV = OFF_GU + GROUP_W
IN_W = OFF_GV + GROUP_W
MIX_W = ATTN_W + 3 * GROUP_W

V7X_VMEM_LIMIT_BYTES = 56 * 1024 * 1024
SUBLANES = 8
MOD_ROWS = 8
FF_CHUNK = 512
KV_CHUNK = 512
DFT_ROWS = 512


def _params(n_axes):
    return pltpu.CompilerParams(dimension_semantics=("arbitrary",) * n_axes,
                                vmem_limit_bytes=V7X_VMEM_LIMIT_BYTES)


def _resident(shape):
    return pl.BlockSpec(shape, lambda *_: (0,) * len(shape), pipeline_mode=pl.Buffered(1))


def _dot(a, b):
    return jnp.dot(a, b, preferred_element_type=F32)


def _gelu_tanh(x):
    return 0.5 * x * (1.0 + jnp.tanh(math.sqrt(2.0 / math.pi) * (x + 0.044715 * (x * x * x))))


def _silu(x):
    return x * (1.0 / (1.0 + jnp.exp(-x)))


def _rms(x, eps=EPS):
    return x * lax.rsqrt(jnp.mean(x * x, axis=-1, keepdims=True) + eps)


def _mod_kernel(c_ref, w_ref, b_ref, o_ref):
    s = _silu(c_ref[...])
    o_ref[...] = jnp.dot(s, w_ref[...], preferred_element_type=F32,
                         precision=lax.Precision.HIGHEST) + b_ref[...]


def _modulation(cs, w_mod, b_mod):
    depth, d, n = w_mod.shape
    tn = next(t for t in (1024, 512, 256, 128) if n % t == 0)
    return pl.pallas_call(
        _mod_kernel,
        grid=(depth, n // tn),
        in_specs=[pl.BlockSpec((MOD_ROWS, d), lambda l, j: (0, 0)),
                  pl.BlockSpec((None, d, tn), lambda l, j: (l, 0, j)),
                  pl.BlockSpec((None, 1, tn), lambda l, j: (l, 0, j))],
        out_specs=pl.BlockSpec((None, MOD_ROWS, tn), lambda l, j: (l, 0, j)),
        out_shape=jax.ShapeDtypeStruct((depth, MOD_ROWS, n), F32),
        compiler_params=_params(2),
        name="modulation",
    )(cs, w_mod, b_mod.reshape(depth, 1, n))


def _in_kernel(x_ref, m_ref, g1_ref, w_ref, qg_ref, kg_ref, cos_ref, sin_ref, dftc_ref,
               lng_ref, lnb_ref, ws_ref, bs_ref,
               q_ref, k_ref, v_ref, ab_ref, cb_ref, t_ref, gm_ref):
    tm = x_ref.shape[0]
    h = _rms(x_ref[...]) * g1_ref[...]
    h = h * (1.0 + m_ref[1:2, :]) + m_ref[0:1, :]
    hb = h.astype(BF16)

    def proj(lo, width=GROUP_W):
        return _dot(hb, w_ref[:, lo:lo + width])

    cos = cos_ref[...]
    sin = sin_ref[...]
    lane = lax.broadcasted_iota(jnp.int32, (tm, HEAD_DIM), 1)
    first_half = (lane % (ROPE_AXIS_DIM)) < (ROPE_AXIS_DIM // 2)

    def norm_rope(ph, gain, scale):
        y = _rms(ph) * gain
        partner = jnp.where(first_half,
                            pltpu.roll(y, HEAD_DIM - ROPE_AXIS_DIM // 2, 1),
                            pltpu.roll(y, ROPE_AXIS_DIM // 2, 1))
        y = y * cos + partner * sin
        if scale != 1.0:
            y = y * scale
        return y.astype(BF16)

    qg = qg_ref[...]
    for half in range(ATTN_W // GROUP_W):
        p = proj(half * GROUP_W)
        for hh in range(GROUP_W // HEAD_DIM):
            c0 = half * GROUP_W + hh * HEAD_DIM
            q_ref[:, c0:c0 + HEAD_DIM] = norm_rope(p[:, hh * HEAD_DIM:(hh + 1) * HEAD_DIM], qg, ATTN_SCALE)

    p = proj(OFF_K)
    kg = kg_ref[...]
    for hh in range(N_KV_HEADS):
        k_ref[:, hh * HEAD_DIM:(hh + 1) * HEAD_DIM] = norm_rope(p[:, hh * HEAD_DIM:(hh + 1) * HEAD_DIM], kg, 1.0)
    v_ref[...] = p[:, KV_W:].astype(BF16)

    p = proj(OFF_F).astype(BF16)
    dftc = dftc_ref[...]
    for g in range(GROUP_W // HEAD_DIM):
        r = _dot(p[:, g * HEAD_DIM:(g + 1) * HEAD_DIM], dftc)
        ab_ref[:, g * HEAD_DIM:(g + 1) * HEAD_DIM] = r[:, :HEAD_DIM].astype(BF16)
        ab_ref[:, GROUP_W + g * HEAD_DIM:GROUP_W + (g + 1) * HEAD_DIM] = r[:, HEAD_DIM:].astype(BF16)

    cb_ref[...] = proj(OFF_CB).astype(BF16)
    t_ref[...] = (proj(OFF_CC) * proj(OFF_CH)).astype(BF16)

    u = _gelu_tanh(proj(OFF_GU))
    gv = _gelu_tanh(proj(OFF_GV))
    gc = gv - jnp.mean(gv, axis=-1, keepdims=True)
    vn = gc * lax.rsqrt(jnp.mean(gc * gc, axis=-1, keepdims=True) + EPS) * lng_ref[...] + lnb_ref[...]
    vn = vn.astype(BF16)
    for g in range(GROUP_W // HEAD_DIM):
        wsg = ws_ref[g]
        bsg = bs_ref[g]
        for c in range(tm // CHUNK):
            rows = slice(c * CHUNK, (c + 1) * CHUNK)
            cols = slice(g * HEAD_DIM, (g + 1) * HEAD_DIM)
            s = _dot(wsg, vn[rows, cols]) + bsg
            gm_ref[rows, cols] = (u[rows, cols] * s).astype(BF16)


def _in_proj(xall, modt, layer, g1, w_in, qg, kg, cos_t, sin_t, dftc, lng, lnb, ws, bsf, *, lay):
    r_rows, d = xall.shape
    tm = lay["tm"]
    grp, nlt, tps = lay["grp"], lay["n_lat_tiles"], lay["tiles_per_seq"]
    row = lambda w: pl.BlockSpec((tm, w), lambda i: (i, 0))
    vec = lambda w: pl.BlockSpec((1, w), lambda i: (0, 0))
    tab = pl.BlockSpec((tm, HEAD_DIM), lambda i: (jnp.where(i < nlt, i % tps, tps), 0))
    widths = (ATTN_W, KV_W, KV_W, 2 * GROUP_W, GROUP_W, GROUP_W, GROUP_W)
    return pl.pallas_call(
        _in_kernel,
        grid=(r_rows // tm,),
        in_specs=[row(d),
                  pl.BlockSpec((None, MOD_ROWS, d), lambda i: (layer, 0, grp(i))),
                  vec(d), _resident(w_in.shape), vec(HEAD_DIM), vec(HEAD_DIM), tab, tab,
                  _resident(dftc.shape), vec(GROUP_W), vec(GROUP_W),
                  _resident(ws.shape), _resident(bsf.shape)],
        out_specs=[row(w) for w in widths],
        out_shape=[jax.ShapeDtypeStruct((r_rows, w), BF16) for w in widths],
        compiler_params=_params(1),
        name="in_proj",
    )(xall, modt, g1, w_in, qg, kg, cos_t, sin_t, dftc, lng, lnb, ws, bsf)


def _dft_kernel(f_ref, ab_ref, o_ref):
    n = ab_ref.shape[0]
    r = _dot(f_ref[:, :n], ab_ref[:, :GROUP_W]) + _dot(f_ref[:, n:], ab_ref[:, GROUP_W:])
    o_ref[...] = r.astype(BF16)


def _pos_dft(fmat, ab, *, batch, n, first_row):
    tmd = min(DFT_ROWS, n)
    blk0 = first_row // n
    return pl.pallas_call(
        _dft_kernel,
        grid=(batch, n // tmd),
        in_specs=[pl.BlockSpec((tmd, 2 * n), lambda b, i: (i, 0)),
                  pl.BlockSpec((n, 2 * GROUP_W), lambda b, i: (blk0 + b, 0))],
        out_specs=pl.BlockSpec((tmd, GROUP_W), lambda b, i: (b * (n // tmd) + i, 0)),
        out_shape=jax.ShapeDtypeStruct((batch * n, GROUP_W), BF16),
        compiler_params=_params(2),
        name="pos_dft",
    )(fmat, ab)


def _attn_kernel(q_ref, kl_ref, kc_ref, vl_ref, vc_ref, o_ref, *, nq):
    tq = q_ref.shape[0]
    seq = kl_ref.shape[0]
    i = pl.program_id(2)
    q = jnp.concatenate([q_ref[:, g * HEAD_DIM:(g + 1) * HEAD_DIM] for g in range(Q_PER_KV)], axis=0)

    def step(carry, kb, vb):
        m, l, acc = carry
        s = lax.dot_general(q, kb, (((1,), (1,)), ((), ())), preferred_element_type=F32)
        m_new = jnp.maximum(m, jnp.max(s, axis=-1, keepdims=True))
        alpha = jnp.exp(m - m_new)
        p = jnp.exp(s - m_new)
        l = alpha * l + jnp.sum(p, axis=-1, keepdims=True)
        acc = alpha * acc + _dot(p.astype(BF16), vb)
        return m_new, l, acc

    rows = Q_PER_KV * tq
    carry = (jnp.full((rows, 1), -jnp.inf, F32), jnp.zeros((rows, 1), F32), jnp.zeros((rows, HEAD_DIM), F32))
    carry = step(carry, kc_ref[...], vc_ref[...])

    def body(j, c):
        start = pl.multiple_of(j * KV_CHUNK, KV_CHUNK)
        return step(c, kl_ref[pl.ds(start, KV_CHUNK), :], vl_ref[pl.ds(start, KV_CHUNK), :])

    n_steps = jnp.where(i < nq, seq // KV_CHUNK, 0)
    _, l, acc = lax.fori_loop(0, n_steps, body, carry)
    o = acc * (1.0 / l)
    for g in range(Q_PER_KV):
        o_ref[:, g * HEAD_DIM:(g + 1) * HEAD_DIM] = o[g * tq:(g + 1) * tq].astype(BF16)


def _attention(q, k, v, *, batch, seq, ctx_len, with_ctx_queries):
    r_rows = q.shape[0]
    tq = ctx_len
    nq = seq // tq
    n_lat_q = batch * nq
    ctx_blk0 = batch * seq // ctx_len
    qw = Q_PER_KV * HEAD_DIM
    q_map = lambda b, hg, i: (jnp.where(i < nq, b * nq + i, n_lat_q + b), hg)
    lat = pl.BlockSpec((seq, HEAD_DIM), lambda b, hg, i: (b, hg))
    ctx = pl.BlockSpec((ctx_len, HEAD_DIM), lambda b, hg, i: (ctx_blk0 + b, hg))
    return pl.pallas_call(
        functools.partial(_attn_kernel, nq=nq),
        grid=(batch, N_KV_HEADS, nq + (1 if with_ctx_queries else 0)),
        in_specs=[pl.BlockSpec((tq, qw), q_map), lat, ctx, lat, ctx],
        out_specs=pl.BlockSpec((tq, qw), q_map),
        out_shape=jax.ShapeDtypeStruct((r_rows, ATTN_W), BF16),
        compiler_params=_params(3),
        name="attention",
    )(q, k, k, v, v)


def _seq_edge_masks(tile, tm, n_lat_tiles, seq, ctx_len):
    r = lax.broadcasted_iota(jnp.int32, (tm, 1), 0)
    period = jnp.where(tile < n_lat_tiles, seq, ctx_len)
    pos = (tile * tm + r) & (period - 1)
    return pos == 0, pos == period - 1


def _dwconv3(center, prev_row, next_row, w_ref, is_start, is_end):
    tm = center.shape[0]
    r = lax.broadcasted_iota(jnp.int32, (tm, 1), 0)
    up = jnp.where(r == 0, prev_row, pltpu.roll(center, 1, 0))
    dn = jnp.where(r == tm - 1, next_row, pltpu.roll(center, tm - 1, 0))
    up = jnp.where(is_start, 0.0, up)
    dn = jnp.where(is_end, 0.0, dn)
    return up * w_ref[0:1, :] + center * w_ref[1:2, :] + dn * w_ref[2:3, :]


def _out_kernel(attn_ref, four_ref, cb_ref, t_ref, tp_ref, tn_ref, gm_ref, cw_ref, w_ref, x_ref, m_ref,
                o_ref, mix_ref, *, n_lat_tiles, seq, ctx_len):
    tm = x_ref.shape[0]
    i = pl.program_id(0)
    is_start, is_end = _seq_edge_masks(i, tm, n_lat_tiles, seq, ctx_len)
    conv = _dwconv3(t_ref[...].astype(F32), tp_ref[SUBLANES - 1:SUBLANES, :].astype(F32),
                    tn_ref[0:1, :].astype(F32), cw_ref, is_start, is_end)
    mix_ref[:, :ATTN_W] = attn_ref[...]
    mix_ref[:, ATTN_W:ATTN_W + GROUP_W] = four_ref[...]
    mix_ref[:, ATTN_W + GROUP_W:ATTN_W + 2 * GROUP_W] = (cb_ref[...].astype(F32) * conv).astype(BF16)
    mix_ref[:, ATTN_W + 2 * GROUP_W:] = gm_ref[...]
    o_ref[...] = x_ref[...] + m_ref[2:3, :] * _dot(mix_ref[...], w_ref[...])


def _halo_specs(tm, width, n_rows):
    per = tm // SUBLANES
    last = n_rows // SUBLANES - 1
    prev = pl.BlockSpec((SUBLANES, width), lambda i, *_: (jnp.maximum(i * per - 1, 0), 0))
    nxt = pl.BlockSpec((SUBLANES, width), lambda i, *_: (jnp.minimum((i + 1) * per, last), 0))
    return prev, nxt


def _out_proj(attn, four, cb, t, gm, conv_w, w_out, xall, modt, layer, *, lay, n_rows):
    d = xall.shape[1]
    tm = lay["tm"]
    grp = lay["grp"]
    row = lambda w: pl.BlockSpec((tm, w), lambda i: (i, 0))
    tprev, tnext = _halo_specs(tm, GROUP_W, t.shape[0])
    kern = functools.partial(_out_kernel, n_lat_tiles=lay["n_lat_tiles"], seq=lay["seq"], ctx_len=lay["ctx_len"])
    return pl.pallas_call(
        kern,
        grid=(n_rows // tm,),
        in_specs=[row(ATTN_W), row(GROUP_W), row(GROUP_W), row(GROUP_W), tprev, tnext, row(GROUP_W),
                  _resident(conv_w.shape), _resident(w_out.shape), row(d),
                  pl.BlockSpec((None, MOD_ROWS, d), lambda i: (layer, 0, grp(i)))],
        out_specs=row(d),
        out_shape=jax.ShapeDtypeStruct((n_rows, d), F32),
        scratch_shapes=[pltpu.VMEM((tm, MIX_W), BF16)],
        compiler_params=_params(1),
        name="out_proj",
    )(attn, four, cb, t, t, t, gm, conv_w, w_out, xall, modt)


def _ffn_kernel(x_ref, xp_ref, xn_ref, m_ref, g2_ref, wg_ref, wu_ref, cw_ref, cb_ref, wd_ref, fg_ref,
                o_ref, h_ref, acc_ref, *, n_lat_tiles, seq, ctx_len, final_norm):
    tm = x_ref.shape[0]
    i = pl.program_id(0)
    j = pl.program_id(1)

    def norm_mod(x):
        return (_rms(x) * g2_ref[...] * (1.0 + m_ref[4:5, :]) + m_ref[3:4, :]).astype(BF16)

    @pl.when(j == 0)
    def _():
        h_ref[0:SUBLANES, :] = norm_mod(xp_ref[...])
        h_ref[SUBLANES:SUBLANES + tm, :] = norm_mod(x_ref[...])
        h_ref[SUBLANES + tm:, :] = norm_mod(xn_ref[...])
        acc_ref[...] = jnp.zeros_like(acc_ref)

    is_start, is_end = _seq_edge_masks(i, tm, n_lat_tiles, seq, ctx_len)
    g = _dot(h_ref[...], wg_ref[...])
    u = _dot(h_ref[SUBLANES:SUBLANES + tm, :], wu_ref[...])
    conv = _dwconv3(g[SUBLANES:SUBLANES + tm], g[SUBLANES - 1:SUBLANES], g[SUBLANES + tm:SUBLANES + tm + 1],
                    cw_ref, is_start, is_end)
    act = (_silu(conv + cb_ref[...]) * u).astype(BF16)
    acc_ref[...] += _dot(act, wd_ref[...])

    @pl.when(j == pl.num_programs(1) - 1)
    def _():
        y = x_ref[...] + m_ref[5:6, :] * acc_ref[...]
        if final_norm:
            y = _rms(y) * fg_ref[...]
        o_ref[...] = y


def _ffn(xall, modt, layer, g2, w_up, conv_w, conv_b, w_down, final_g, *, lay, n_rows, final_norm):
    d = xall.shape[1]
    d_ff = w_down.shape[0]
    tm = lay["tm"]
    grp = lay["grp"]
    nj = d_ff // FF_CHUNK
    xprev, xnext = _halo_specs(tm, d, xall.shape[0])
    kern = functools.partial(_ffn_kernel, n_lat_tiles=lay["n_lat_tiles"], seq=lay["seq"],
                             ctx_len=lay["ctx_len"], final_norm=final_norm)
    return pl.pallas_call(
        kern,
        grid=(n_rows // tm, nj),
        in_specs=[pl.BlockSpec((tm, d), lambda i, j: (i, 0)), xprev, xnext,
                  pl.BlockSpec((None, MOD_ROWS, d), lambda i, j: (layer, 0, grp(i))),
                  pl.BlockSpec((1, d), lambda i, j: (0, 0)),
                  pl.BlockSpec((d, FF_CHUNK), lambda i, j: (0, j)),
                  pl.BlockSpec((d, FF_CHUNK), lambda i, j: (0, nj + j)),
                  pl.BlockSpec((SUBLANES, FF_CHUNK), lambda i, j: (0, j)),
                  pl.BlockSpec((1, FF_CHUNK), lambda i, j: (0, j)),
                  pl.BlockSpec((FF_CHUNK, d), lambda i, j: (j, 0)),
                  pl.BlockSpec((1, d), lambda i, j: (0, 0))],
        out_specs=pl.BlockSpec((tm, d), lambda i, j: (i, 0)),
        out_shape=jax.ShapeDtypeStruct((n_rows, d), F32),
        scratch_shapes=[pltpu.VMEM((tm + 2 * SUBLANES, d), BF16), pltpu.VMEM((tm, d), F32)],
        compiler_params=_params(2),
        name="ffn",
    )(xall, xall, xall, modt, g2, w_up, w_up, conv_w, conv_b, w_down, final_g)


def _rope_tables(seq, tm):
    pos = jnp.arange(seq, dtype=jnp.int32)
    row = (pos // GRID_W).astype(F32)
    col = (pos % GRID_W).astype(F32)
    freqs = ROPE_THETA ** (-jnp.arange(0, ROPE_AXIS_DIM, 2, dtype=F32) / ROPE_AXIS_DIM)
    ang_r = row[:, None] * freqs[None, :]
    ang_c = col[:, None] * freqs[None, :]
    cos = jnp.concatenate([jnp.cos(ang_r)] * 2 + [jnp.cos(ang_c)] * 2, axis=-1)
    sin = jnp.concatenate([-jnp.sin(ang_r), jnp.sin(ang_r), -jnp.sin(ang_c), jnp.sin(ang_c)], axis=-1)
    cos = jnp.concatenate([cos, jnp.ones((tm, HEAD_DIM), F32)], axis=0)
    sin = jnp.concatenate([sin, jnp.zeros((tm, HEAD_DIM), F32)], axis=0)
    return cos, sin


def _dft_matrix(n, scale):
    n1 = 1
    while n1 * n1 < n:
        n1 *= 2
    n2 = n // n1
    t = jnp.arange(n, dtype=jnp.int32)[None, :]
    a = jnp.arange(n1, dtype=jnp.int32)[:, None]
    b = jnp.arange(n2, dtype=jnp.int32)[:, None]
    ang_a = ((a * t) % n1).astype(F32) * (2.0 * math.pi / n1)
    ang_b = ((b * t) % n).astype(F32) * (2.0 * math.pi / n)
    ca, sa = jnp.cos(ang_a)[:, None, :], jnp.sin(ang_a)[:, None, :]
    cb, sb = jnp.cos(ang_b)[None, :, :] * scale, jnp.sin(ang_b)[None, :, :] * scale
    c = (ca * cb - sa * sb).reshape(n, n)
    s = (sa * cb + ca * sb).reshape(n, n)
    return jnp.concatenate([c, -s], axis=1).astype(BF16)


def _channel_dft(scale):
    k = jnp.arange(HEAD_DIM, dtype=jnp.int32)
    ang = ((k[:, None] * k[None, :]) % HEAD_DIM).astype(F32) * (2.0 * math.pi / HEAD_DIM)
    return (jnp.concatenate([jnp.cos(ang), jnp.sin(ang)], axis=1) * scale).astype(BF16)


def _layout(batch, seq, ctx_len):
    n_lat, n_ctx = batch * seq, batch * ctx_len
    tm = 512
    while n_ctx % tm or seq % tm:
        tm //= 2
    assert tm >= CHUNK and seq & (seq - 1) == 0 and ctx_len & (ctx_len - 1) == 0
    assert seq % GRID_W == 0 and ctx_len % CHUNK == 0 and n_lat % ctx_len == 0 and n_lat % seq == 0
    n_lat_tiles, tps = n_lat // tm, seq // tm
    grp = lambda i: jnp.where(i < n_lat_tiles, 1 + i // tps, 0)
    return dict(tm=tm, n_lat_tiles=n_lat_tiles, tiles_per_seq=tps, grp=grp, seq=seq, ctx_len=ctx_len,
                n_lat=n_lat, n_ctx=n_ctx)


def kernel(x, c, ctx, c_ctx, w_mod, b_mod, norm1_g, norm2_g, w_in, q_norm_g, k_norm_g, conv_w, gm_ln_g,
           gm_ln_b, gm_ws, gm_b, w_out, w_up, ffn_conv_w, ffn_conv_b, w_down, final_norm_g):
    batch, seq, d = x.shape
    ctx_len = ctx.shape[1]
    depth = w_mod.shape[0]
    d_ff = w_down.shape[1]
    assert batch + 1 <= MOD_ROWS and w_in.shape[2] == IN_W and w_down.shape[1] % FF_CHUNK == 0
    lay = _layout(batch, seq, ctx_len)
    n_lat, tm = lay["n_lat"], lay["tm"]

    xall = jnp.concatenate([x.reshape(n_lat, d), ctx.reshape(batch * ctx_len, d)], axis=0)

    cs = jnp.concatenate([c_ctx[None, :], c, jnp.zeros((MOD_ROWS - 1 - batch, d), F32)], axis=0)
    mod = _modulation(cs, w_mod, b_mod)
    modt = mod.reshape(depth, MOD_ROWS, N_MOD, d).transpose(0, 2, 1, 3).reshape(depth, N_MOD, MOD_ROWS * d)
    modt = jnp.pad(modt, ((0, 0), (0, MOD_ROWS - N_MOD), (0, 0)))

    cos_t, sin_t = _rope_tables(seq, tm)
    f_lat = _dft_matrix(seq, seq ** -0.5)
    f_ctx = _dft_matrix(ctx_len, ctx_len ** -0.5)
    dftc = _channel_dft(HEAD_DIM ** -0.5)

    pad_rows = lambda w: jnp.pad(w, ((0, 0), (0, SUBLANES - w.shape[1]), (0, 0)))
    conv_w8 = pad_rows(conv_w)
    ffn_conv_w8 = pad_rows(ffn_conv_w)
    bsf = jnp.broadcast_to(gm_b[..., None], gm_b.shape + (HEAD_DIM,))
    w_in_b, w_out_b, w_up_b, w_down_b = (w.astype(BF16) for w in (w_in, w_out, w_up, w_down))
    gm_ws_b = gm_ws.astype(BF16)
    fg = final_norm_g.reshape(1, d)

    for l in range(depth):
        last = l == depth - 1
        q, k, v, ab, cb, t, gm = _in_proj(
            xall, modt, l, norm1_g[l].reshape(1, d), w_in_b[l], q_norm_g[l].reshape(1, HEAD_DIM),
            k_norm_g[l].reshape(1, HEAD_DIM), cos_t, sin_t, dftc, gm_ln_g[l].reshape(1, GROUP_W),
            gm_ln_b[l].reshape(1, GROUP_W), gm_ws_b[l], bsf[l], lay=lay)
        four = _pos_dft(f_lat, ab, batch=batch, n=seq, first_row=0)
        if not last:
            four_ctx = _pos_dft(f_ctx, ab, batch=batch, n=ctx_len, first_row=n_lat)
            four = jnp.concatenate([four, four_ctx], axis=0)
        attn = _attention(q, k, v, batch=batch, seq=seq, ctx_len=ctx_len, with_ctx_queries=not last)
        n_rows = n_lat if last else xall.shape[0]
        xall = _out_proj(attn, four, cb, t, gm, conv_w8[l], w_out_b[l], xall, modt, l, lay=lay, n_rows=n_rows)
        xall = _ffn(xall, modt, l, norm2_g[l].reshape(1, d), w_up_b[l], ffn_conv_w8[l],
                    ffn_conv_b[l].reshape(1, d_ff), w_down_b[l], fg, lay=lay, n_rows=n_rows, final_norm=last)
    return xall.reshape(batch, seq, d)
```

```python
import functools
import math

import jax
import jax.numpy as jnp
from jax import lax
from jax.experimental import pallas as pl
from jax.experimental.pallas import tpu as pltpu

F32 = jnp.float32
BF16 = jnp.bfloat16

GRID_W = 64
HEAD_DIM = 128
N_HEADS = 8
N_KV_HEADS = 2
Q_PER_KV = N_HEADS // N_KV_HEADS
ATTN_W = N_HEADS * HEAD_DIM
KV_W = N_KV_HEADS * HEAD_DIM
ROPE_THETA = 10000.0
ROPE_AXIS_DIM = HEAD_DIM // 2
ATTN_SCALE = HEAD_DIM ** -0.5
Q_SCALE = ATTN_SCALE * math.log2(math.e)
GROUP_W = 4 * HEAD_DIM
CHUNK = 128
N_MOD = 6
EPS = 1e-6

OFF_K = ATTN_W
OFF_V = OFF_K + KV_W
OFF_F = OFF_V + KV_W
OFF_CB = OFF_F + GROUP_W
OFF_CC = OFF_CB + GROUP_W
OFF_CH = OFF_CC + GROUP_W
OFF_GU = OFF_CH + GROUP_W
OFF_GV = OFF_GU + GROUP_W
IN_W = OFF_GV + GROUP_W
MIX_W = ATTN_W + 3 * GROUP_W

V7X_VMEM_LIMIT_BYTES = 56 * 1024 * 1024
SUBLANES = 8
MOD_ROWS = 8
FF_CHUNK = 512
KV_CHUNK = 512
DFT_ROWS = 512


def _params(n_axes):
    return pltpu.CompilerParams(dimension_semantics=("arbitrary",) * n_axes,
                                vmem_limit_bytes=V7X_VMEM_LIMIT_BYTES)


def _resident(shape):
    return pl.BlockSpec(shape, lambda *_: (0,) * len(shape), pipeline_mode=pl.Buffered(1))


def _dot(a, b):
    return jnp.dot(a, b, preferred_element_type=F32)


def _gelu_tanh(x):
    return 0.5 * x * (1.0 + jnp.tanh(math.sqrt(2.0 / math.pi) * (x + 0.044715 * (x * x * x))))


def _silu(x):
    return x * (1.0 / (1.0 + jnp.exp(-x)))


def _rms(x, eps=EPS):
    return x * lax.rsqrt(jnp.mean(x * x, axis=-1, keepdims=True) + eps)


def _mod_kernel(c_ref, w_ref, b_ref, o_ref):
    s = _silu(c_ref[...])
    o_ref[...] = jnp.dot(s, w_ref[...], preferred_element_type=F32,
                         precision=lax.Precision.HIGHEST) + b_ref[...]


def _modulation(cs, w_mod, b_mod):
    depth, d, n = w_mod.shape
    tn = next(t for t in (1024, 512, 256, 128) if n % t == 0)
    return pl.pallas_call(
        _mod_kernel,
        grid=(depth, n // tn),
        in_specs=[pl.BlockSpec((MOD_ROWS, d), lambda l, j: (0, 0)),
                  pl.BlockSpec((None, d, tn), lambda l, j: (l, 0, j)),
                  pl.BlockSpec((None, 1, tn), lambda l, j: (l, 0, j))],
        out_specs=pl.BlockSpec((None, MOD_ROWS, tn), lambda l, j: (l, 0, j)),
        out_shape=jax.ShapeDtypeStruct((depth, MOD_ROWS, n), F32),
        compiler_params=_params(2),
        name="modulation",
    )(cs, w_mod, b_mod.reshape(depth, 1, n))


def _in_kernel(x_ref, m_ref, g1_ref, w_ref, qg_ref, kg_ref, cos_ref, sin_ref, dftc_ref,
               lng_ref, lnb_ref, ws_ref, bs_ref,
               qt_ref, k_ref, vt_ref, ab_ref, cb_ref, t_ref, gm_ref):
    tm = x_ref.shape[0]
    h = _rms(x_ref[...]) * g1_ref[...]
    h = h * (1.0 + m_ref[1:2, :]) + m_ref[0:1, :]
    hb = h.astype(BF16)

    def proj(lo, width=GROUP_W):
        return _dot(hb, w_ref[:, lo:lo + width])

    cos = cos_ref[...]
    sin = sin_ref[...]
    lane = lax.broadcasted_iota(jnp.int32, (tm, HEAD_DIM), 1)
    first_half = (lane % (ROPE_AXIS_DIM)) < (ROPE_AXIS_DIM // 2)

    def norm_rope(ph, gain):
        y = _rms(ph) * gain
        partner = jnp.where(first_half,
                            pltpu.roll(y, HEAD_DIM - ROPE_AXIS_DIM // 2, 1),
                            pltpu.roll(y, ROPE_AXIS_DIM // 2, 1))
        return y * cos + partner * sin

    qg = qg_ref[...] * Q_SCALE
    for half in range(ATTN_W // GROUP_W):
        p = proj(half * GROUP_W)
        for hh in range(GROUP_W // HEAD_DIM):
            c0 = half * GROUP_W + hh * HEAD_DIM
            y = norm_rope(p[:, hh * HEAD_DIM:(hh + 1) * HEAD_DIM], qg)
            qt_ref[c0:c0 + HEAD_DIM, :] = y.T.astype(BF16)

    p = proj(OFF_K)
    kg = kg_ref[...]
    for hh in range(N_KV_HEADS):
        cols = slice(hh * HEAD_DIM, (hh + 1) * HEAD_DIM)
        k_ref[:, cols] = norm_rope(p[:, cols], kg).astype(BF16)
        vt_ref[cols, :] = p[:, KV_W + hh * HEAD_DIM:KV_W + (hh + 1) * HEAD_DIM].T.astype(BF16)

    p = proj(OFF_F).astype(BF16)
    dftc = dftc_ref[...]
    for g in range(GROUP_W // HEAD_DIM):
        r = _dot(p[:, g * HEAD_DIM:(g + 1) * HEAD_DIM], dftc)
        ab_ref[:, g * HEAD_DIM:(g + 1) * HEAD_DIM] = r[:, :HEAD_DIM].astype(BF16)
        ab_ref[:, GROUP_W + g * HEAD_DIM:GROUP_W + (g + 1) * HEAD_DIM] = r[:, HEAD_DIM:].astype(BF16)

    cb_ref[...] = proj(OFF_CB).astype(BF16)
    t_ref[...] = (proj(OFF_CC) * proj(OFF_CH)).astype(BF16)

    u = _gelu_tanh(proj(OFF_GU))
    gv = _gelu_tanh(proj(OFF_GV))
    gc = gv - jnp.mean(gv, axis=-1, keepdims=True)
    vn = gc * lax.rsqrt(jnp.mean(gc * gc, axis=-1, keepdims=True) + EPS) * lng_ref[...] + lnb_ref[...]
    vn = vn.astype(BF16)
    for g in range(GROUP_W // HEAD_DIM):
        wsg = ws_ref[g]
        bsg = bs_ref[g]
        for c in range(tm // CHUNK):
            rows = slice(c * CHUNK, (c + 1) * CHUNK)
            cols = slice(g * HEAD_DIM, (g + 1) * HEAD_DIM)
            s = _dot(wsg, vn[rows, cols]) + bsg
            gm_ref[rows, cols] = (u[rows, cols] * s).astype(BF16)


def _in_proj(xall, modt, layer, g1, w_in, qg, kg, cos_t, sin_t, dftc, lng, lnb, ws, bsf, *, lay):
    r_rows, d = xall.shape
    tm = lay["tm"]
    grp, nlt, tps = lay["grp"], lay["n_lat_tiles"], lay["tiles_per_seq"]
    row = lambda w: pl.BlockSpec((tm, w), lambda i: (i, 0))
    vec = lambda w: pl.BlockSpec((1, w), lambda i: (0, 0))
    tab = pl.BlockSpec((tm, HEAD_DIM), lambda i: (jnp.where(i < nlt, i % tps, tps), 0))
    col = lambda h: pl.BlockSpec((h, tm), lambda i: (0, i))
    rows_bf16 = lambda w: jax.ShapeDtypeStruct((r_rows, w), BF16)
    cols_bf16 = lambda h: jax.ShapeDtypeStruct((h, r_rows), BF16)
    return pl.pallas_call(
        _in_kernel,
        grid=(r_rows // tm,),
        in_specs=[row(d),
                  pl.BlockSpec((None, MOD_ROWS, d), lambda i: (layer, 0, grp(i))),
                  vec(d), _resident(w_in.shape), vec(HEAD_DIM), vec(HEAD_DIM), tab, tab,
                  _resident(dftc.shape), vec(GROUP_W), vec(GROUP_W),
                  _resident(ws.shape), _resident(bsf.shape)],
        out_specs=[col(ATTN_W), row(KV_W), col(KV_W), row(2 * GROUP_W), row(GROUP_W), row(GROUP_W), row(GROUP_W)],
        out_shape=[cols_bf16(ATTN_W), rows_bf16(KV_W), cols_bf16(KV_W), rows_bf16(2 * GROUP_W),
                   rows_bf16(GROUP_W), rows_bf16(GROUP_W), rows_bf16(GROUP_W)],
        compiler_params=_params(1),
        name="in_proj",
    )(xall, modt, g1, w_in, qg, kg, cos_t, sin_t, dftc, lng, lnb, ws, bsf)


def _dft_kernel(f_ref, ab_ref, o_ref):
    n = ab_ref.shape[0]
    r = _dot(f_ref[:, :n], ab_ref[:, :GROUP_W]) + _dot(f_ref[:, n:], ab_ref[:, GROUP_W:])
    o_ref[...] = r.astype(BF16)


def _pos_dft(fmat, ab, *, batch, n, first_row):
    tmd = min(DFT_ROWS, n)
    blk0 = first_row // n
    return pl.pallas_call(
        _dft_kernel,
        grid=(batch, n // tmd),
        in_specs=[pl.BlockSpec((tmd, 2 * n), lambda b, i: (i, 0)),
                  pl.BlockSpec((n, 2 * GROUP_W), lambda b, i: (blk0 + b, 0))],
        out_specs=pl.BlockSpec((tmd, GROUP_W), lambda b, i: (b * (n // tmd) + i, 0)),
        out_shape=jax.ShapeDtypeStruct((batch * n, GROUP_W), BF16),
        compiler_params=_params(2),
        name="pos_dft",
    )(fmat, ab)


def _attn_kernel(qt_ref, kl_ref, kc_ref, vtl_ref, vtc_ref, o_ref, *, nq, with_ctx_queries):
    tq = qt_ref.shape[1]
    seq = kl_ref.shape[0]
    cols = Q_PER_KV * tq
    qt = jnp.concatenate([qt_ref[g * HEAD_DIM:(g + 1) * HEAD_DIM, :] for g in range(Q_PER_KV)], axis=1)

    def step(carry, kb, vtb):
        m, l, acc = carry
        s = _dot(kb, qt)
        m_new = jnp.maximum(m, jnp.max(s, axis=0, keepdims=True))
        alpha = jnp.exp2(m - m_new)
        p = jnp.exp2(s - m_new)
        l = alpha * l + jnp.sum(p, axis=0, keepdims=True)
        acc = alpha * acc + _dot(vtb, p.astype(BF16))
        return m_new, l, acc

    def finish(carry):
        _, l, acc = carry
        o = acc * (1.0 / l)
        for g in range(Q_PER_KV):
            o_ref[:, g * HEAD_DIM:(g + 1) * HEAD_DIM] = o[:, g * tq:(g + 1) * tq].T.astype(BF16)

    def run(with_latent_keys):
        carry = (jnp.full((1, cols), -jnp.inf, F32), jnp.zeros((1, cols), F32), jnp.zeros((HEAD_DIM, cols), F32))
        carry = step(carry, kc_ref[...], vtc_ref[...])
        if with_latent_keys:
            for j in range(seq // KV_CHUNK):
                ks = slice(j * KV_CHUNK, (j + 1) * KV_CHUNK)
                carry = step(carry, kl_ref[ks, :], vtl_ref[:, ks])
        finish(carry)

    if not with_ctx_queries:
        run(True)
    else:
        is_latent = pl.program_id(2) < nq
        pl.when(is_latent)(lambda: run(True))
        pl.when(jnp.logical_not(is_latent))(lambda: run(False))


def _attention(qt, k, vt, *, batch, seq, ctx_len, with_ctx_queries):
    r_rows = k.shape[0]
    tq = ctx_len
    nq = seq // tq
    n_lat_q = batch * nq
    ctx_blk0 = batch * seq // ctx_len
    qw = Q_PER_KV * HEAD_DIM
    q_blk = lambda b, i: jnp.where(i < nq, b * nq + i, n_lat_q + b)
    return pl.pallas_call(
        functools.partial(_attn_kernel, nq=nq, with_ctx_queries=with_ctx_queries),
        grid=(batch, N_KV_HEADS, nq + (1 if with_ctx_queries else 0)),
        in_specs=[pl.BlockSpec((qw, tq), lambda b, hg, i: (hg, q_blk(b, i))),
                  pl.BlockSpec((seq, HEAD_DIM), lambda b, hg, i: (b, hg)),
                  pl.BlockSpec((ctx_len, HEAD_DIM), lambda b, hg, i: (ctx_blk0 + b, hg)),
                  pl.BlockSpec((HEAD_DIM, seq), lambda b, hg, i: (hg, b)),
                  pl.BlockSpec((HEAD_DIM, ctx_len), lambda b, hg, i: (hg, ctx_blk0 + b))],
        out_specs=pl.BlockSpec((tq, qw), lambda b, hg, i: (q_blk(b, i), hg)),
        out_shape=jax.ShapeDtypeStruct((r_rows, ATTN_W), BF16),
        compiler_params=_params(3),
        name="attention",
    )(qt, k, k, vt, vt)


def _seq_edge_masks(tile, tm, n_lat_tiles, seq, ctx_len):
    r = lax.broadcasted_iota(jnp.int32, (tm, 1), 0)
    period = jnp.where(tile < n_lat_tiles, seq, ctx_len)
    pos = (tile * tm + r) & (period - 1)
    return pos == 0, pos == period - 1


def _dwconv3(center, prev_row, next_row, w_ref, is_start, is_end):
    tm = center.shape[0]
    r = lax.broadcasted_iota(jnp.int32, (tm, 1), 0)
    up = jnp.where(r == 0, prev_row, pltpu.roll(center, 1, 0))
    dn = jnp.where(r == tm - 1, next_row, pltpu.roll(center, tm - 1, 0))
    up = jnp.where(is_start, 0.0, up)
    dn = jnp.where(is_end, 0.0, dn)
    return up * w_ref[0:1, :] + center * w_ref[1:2, :] + dn * w_ref[2:3, :]


def _out_kernel(attn_ref, four_ref, cb_ref, t_ref, tp_ref, tn_ref, gm_ref, cw_ref, w_ref, x_ref, m_ref,
                o_ref, mix_ref, *, n_lat_tiles, seq, ctx_len):
    tm = x_ref.shape[0]
    i = pl.program_id(0)
    is_start, is_end = _seq_edge_masks(i, tm, n_lat_tiles, seq, ctx_len)
    conv = _dwconv3(t_ref[...].astype(F32), tp_ref[SUBLANES - 1:SUBLANES, :].astype(F32),
                    tn_ref[0:1, :].astype(F32), cw_ref, is_start, is_end)
    mix_ref[:, :ATTN_W] = attn_ref[...]
    mix_ref[:, ATTN_W:ATTN_W + GROUP_W] = four_ref[...]
    mix_ref[:, ATTN_W + GROUP_W:ATTN_W + 2 * GROUP_W] = (cb_ref[...].astype(F32) * conv).astype(BF16)
    mix_ref[:, ATTN_W + 2 * GROUP_W:] = gm_ref[...]
    o_ref[...] = x_ref[...] + m_ref[2:3, :] * _dot(mix_ref[...], w_ref[...])


def _halo_specs(tm, width, n_rows):
    per = tm // SUBLANES
    last = n_rows // SUBLANES - 1
    prev = pl.BlockSpec((SUBLANES, width), lambda i, *_: (jnp.maximum(i * per - 1, 0), 0))
    nxt = pl.BlockSpec((SUBLANES, width), lambda i, *_: (jnp.minimum((i + 1) * per, last), 0))
    return prev, nxt


def _out_proj(attn, four, cb, t, gm, conv_w, w_out, xall, modt, layer, *, lay, n_rows):
    d = xall.shape[1]
    tm = lay["tm"]
    grp = lay["grp"]
    row = lambda w: pl.BlockSpec((tm, w), lambda i: (i, 0))
    tprev, tnext = _halo_specs(tm, GROUP_W, t.shape[0])
    kern = functools.partial(_out_kernel, n_lat_tiles=lay["n_lat_tiles"], seq=lay["seq"], ctx_len=lay["ctx_len"])
    return pl.pallas_call(
        kern,
        grid=(n_rows // tm,),
        in_specs=[row(ATTN_W), row(GROUP_W), row(GROUP_W), row(GROUP_W), tprev, tnext, row(GROUP_W),
                  _resident(conv_w.shape), _resident(w_out.shape), row(d),
                  pl.BlockSpec((None, MOD_ROWS, d), lambda i: (layer, 0, grp(i)))],
        out_specs=row(d),
        out_shape=jax.ShapeDtypeStruct((n_rows, d), F32),
        scratch_shapes=[pltpu.VMEM((tm, MIX_W), BF16)],
        compiler_params=_params(1),
        name="out_proj",
    )(attn, four, cb, t, t, t, gm, conv_w, w_out, xall, modt)


def _ffn_kernel(x_ref, xp_ref, xn_ref, m_ref, g2_ref, wg_ref, wu_ref, cw_ref, cb_ref, wd_ref, fg_ref,
                o_ref, h_ref, acc_ref, *, n_lat_tiles, seq, ctx_len, final_norm):
    tm = x_ref.shape[0]
    i = pl.program_id(0)
    j = pl.program_id(1)

    def norm_mod(x):
        return (_rms(x) * g2_ref[...] * (1.0 + m_ref[4:5, :]) + m_ref[3:4, :]).astype(BF16)

    @pl.when(j == 0)
    def _():
        h_ref[0:SUBLANES, :] = norm_mod(xp_ref[...])
        h_ref[SUBLANES:SUBLANES + tm, :] = norm_mod(x_ref[...])
        h_ref[SUBLANES + tm:, :] = norm_mod(xn_ref[...])
        acc_ref[...] = jnp.zeros_like(acc_ref)

    is_start, is_end = _seq_edge_masks(i, tm, n_lat_tiles, seq, ctx_len)
    g = _dot(h_ref[...], wg_ref[...])
    u = _dot(h_ref[SUBLANES:SUBLANES + tm, :], wu_ref[...])
    conv = _dwconv3(g[SUBLANES:SUBLANES + tm], g[SUBLANES - 1:SUBLANES], g[SUBLANES + tm:SUBLANES + tm + 1],
                    cw_ref, is_start, is_end)
    act = (_silu(conv + cb_ref[...]) * u).astype(BF16)
    acc_ref[...] += _dot(act, wd_ref[...])

    @pl.when(j == pl.num_programs(1) - 1)
    def _():
        y = x_ref[...] + m_ref[5:6, :] * acc_ref[...]
        if final_norm:
            y = _rms(y) * fg_ref[...]
        o_ref[...] = y


def _ffn(xall, modt, layer, g2, w_up, conv_w, conv_b, w_down, final_g, *, lay, n_rows, final_norm):
    d = xall.shape[1]
    d_ff = w_down.shape[0]
    tm = lay["tm"]
    grp = lay["grp"]
    nj = d_ff // FF_CHUNK
    xprev, xnext = _halo_specs(tm, d, xall.shape[0])
    kern = functools.partial(_ffn_kernel, n_lat_tiles=lay["n_lat_tiles"], seq=lay["seq"],
                             ctx_len=lay["ctx_len"], final_norm=final_norm)
    return pl.pallas_call(
        kern,
        grid=(n_rows // tm, nj),
        in_specs=[pl.BlockSpec((tm, d), lambda i, j: (i, 0)), xprev, xnext,
                  pl.BlockSpec((None, MOD_ROWS, d), lambda i, j: (layer, 0, grp(i))),
                  pl.BlockSpec((1, d), lambda i, j: (0, 0)),
                  pl.BlockSpec((d, FF_CHUNK), lambda i, j: (0, j)),
                  pl.BlockSpec((d, FF_CHUNK), lambda i, j: (0, nj + j)),
                  pl.BlockSpec((SUBLANES, FF_CHUNK), lambda i, j: (0, j)),
                  pl.BlockSpec((1, FF_CHUNK), lambda i, j: (0, j)),
                  pl.BlockSpec((FF_CHUNK, d), lambda i, j: (j, 0)),
                  pl.BlockSpec((1, d), lambda i, j: (0, 0))],
        out_specs=pl.BlockSpec((tm, d), lambda i, j: (i, 0)),
        out_shape=jax.ShapeDtypeStruct((n_rows, d), F32),
        scratch_shapes=[pltpu.VMEM((tm + 2 * SUBLANES, d), BF16), pltpu.VMEM((tm, d), F32)],
        compiler_params=_params(2),
        name="ffn",
    )(xall, xall, xall, modt, g2, w_up, w_up, conv_w, conv_b, w_down, final_g)


def _rope_tables(seq, tm):
    pos = jnp.arange(seq, dtype=jnp.int32)
    row = (pos // GRID_W).astype(F32)
    col = (pos % GRID_W).astype(F32)
    freqs = ROPE_THETA ** (-jnp.arange(0, ROPE_AXIS_DIM, 2, dtype=F32) / ROPE_AXIS_DIM)
    ang_r = row[:, None] * freqs[None, :]
    ang_c = col[:, None] * freqs[None, :]
    cos = jnp.concatenate([jnp.cos(ang_r)] * 2 + [jnp.cos(ang_c)] * 2, axis=-1)
    sin = jnp.concatenate([-jnp.sin(ang_r), jnp.sin(ang_r), -jnp.sin(ang_c), jnp.sin(ang_c)], axis=-1)
    cos = jnp.concatenate([cos, jnp.ones((tm, HEAD_DIM), F32)], axis=0)
    sin = jnp.concatenate([sin, jnp.zeros((tm, HEAD_DIM), F32)], axis=0)
    return cos, sin


def _dft_matrix(n, scale):
    n1 = 1
    while n1 * n1 < n:
        n1 *= 2
    n2 = n // n1
    t = jnp.arange(n, dtype=jnp.int32)[None, :]
    a = jnp.arange(n1, dtype=jnp.int32)[:, None]
    b = jnp.arange(n2, dtype=jnp.int32)[:, None]
    ang_a = ((a * t) % n1).astype(F32) * (2.0 * math.pi / n1)
    ang_b = ((b * t) % n).astype(F32) * (2.0 * math.pi / n)
    ca, sa = jnp.cos(ang_a)[:, None, :], jnp.sin(ang_a)[:, None, :]
    cb, sb = jnp.cos(ang_b)[None, :, :] * scale, jnp.sin(ang_b)[None, :, :] * scale
    c = (ca * cb - sa * sb).reshape(n, n)
    s = (sa * cb + ca * sb).reshape(n, n)
    return jnp.concatenate([c, -s], axis=1).astype(BF16)


def _channel_dft(scale):
    k = jnp.arange(HEAD_DIM, dtype=jnp.int32)
    ang = ((k[:, None] * k[None, :]) % HEAD_DIM).astype(F32) * (2.0 * math.pi / HEAD_DIM)
    return (jnp.concatenate([jnp.cos(ang), jnp.sin(ang)], axis=1) * scale).astype(BF16)


def _layout(batch, seq, ctx_len):
    n_lat, n_ctx = batch * seq, batch * ctx_len
    tm = 512
    while n_ctx % tm or seq % tm:
        tm //= 2
    assert tm >= CHUNK and seq & (seq - 1) == 0 and ctx_len & (ctx_len - 1) == 0
    assert seq % GRID_W == 0 and ctx_len % CHUNK == 0 and n_lat % ctx_len == 0 and n_lat % seq == 0
    n_lat_tiles, tps = n_lat // tm, seq // tm
    grp = lambda i: jnp.where(i < n_lat_tiles, 1 + i // tps, 0)
    return dict(tm=tm, n_lat_tiles=n_lat_tiles, tiles_per_seq=tps, grp=grp, seq=seq, ctx_len=ctx_len,
                n_lat=n_lat, n_ctx=n_ctx)


def kernel(x, c, ctx, c_ctx, w_mod, b_mod, norm1_g, norm2_g, w_in, q_norm_g, k_norm_g, conv_w, gm_ln_g,
           gm_ln_b, gm_ws, gm_b, w_out, w_up, ffn_conv_w, ffn_conv_b, w_down, final_norm_g):
    batch, seq, d = x.shape
    ctx_len = ctx.shape[1]
    depth = w_mod.shape[0]
    d_ff = w_down.shape[1]
    assert batch + 1 <= MOD_ROWS and w_in.shape[2] == IN_W and w_down.shape[1] % FF_CHUNK == 0
    lay = _layout(batch, seq, ctx_len)
    n_lat, tm = lay["n_lat"], lay["tm"]

    xall = jnp.concatenate([x.reshape(n_lat, d), ctx.reshape(batch * ctx_len, d)], axis=0)

    cs = jnp.concatenate([c_ctx[None, :], c, jnp.zeros((MOD_ROWS - 1 - batch, d), F32)], axis=0)
    mod = _modulation(cs, w_mod, b_mod)
    modt = mod.reshape(depth, MOD_ROWS, N_MOD, d).transpose(0, 2, 1, 3).reshape(depth, N_MOD, MOD_ROWS * d)
    modt = jnp.pad(modt, ((0, 0), (0, MOD_ROWS - N_MOD), (0, 0)))

    cos_t, sin_t = _rope_tables(seq, tm)
    f_lat = _dft_matrix(seq, seq ** -0.5)
    f_ctx = _dft_matrix(ctx_len, ctx_len ** -0.5)
    dftc = _channel_dft(HEAD_DIM ** -0.5)

    pad_rows = lambda w: jnp.pad(w, ((0, 0), (0, SUBLANES - w.shape[1]), (0, 0)))
    conv_w8 = pad_rows(conv_w)
    ffn_conv_w8 = pad_rows(ffn_conv_w)
    bsf = jnp.broadcast_to(gm_b[..., None], gm_b.shape + (HEAD_DIM,))
    w_in_b, w_out_b, w_up_b, w_down_b = (w.astype(BF16) for w in (w_in, w_out, w_up, w_down))
    gm_ws_b = gm_ws.astype(BF16)
    fg = final_norm_g.reshape(1, d)

    for l in range(depth):
        last = l == depth - 1
        qt, k, vt, ab, cb, t, gm = _in_proj(
            xall, modt, l, norm1_g[l].reshape(1, d), w_in_b[l], q_norm_g[l].reshape(1, HEAD_DIM),
            k_norm_g[l].reshape(1, HEAD_DIM), cos_t, sin_t, dftc, gm_ln_g[l].reshape(1, GROUP_W),
            gm_ln_b[l].reshape(1, GROUP_W), gm_ws_b[l], bsf[l], lay=lay)
        four = _pos_dft(f_lat, ab, batch=batch, n=seq, first_row=0)
        if not last:
            four_ctx = _pos_dft(f_ctx, ab, batch=batch, n=ctx_len, first_row=n_lat)
            four = jnp.concatenate([four, four_ctx], axis=0)
        attn = _attention(qt, k, vt, batch=batch, seq=seq, ctx_len=ctx_len, with_ctx_queries=not last)
        n_rows = n_lat if last else xall.shape[0]
        xall = _out_proj(attn, four, cb, t, gm, conv_w8[l], w_out_b[l], xall, modt, l, lay=lay, n_rows=n_rows)
        xall = _ffn(xall, modt, l, norm2_g[l].reshape(1, d), w_up_b[l], ffn_conv_w8[l],
                    ffn_conv_b[l].reshape(1, d_ff), w_down_b[l], fg, lay=lay, n_rows=n_rows, final_norm=last)
    return xall.reshape(batch, seq, d)
```

```python
import functools
import math

import jax
import jax.numpy as jnp
from jax import lax
from jax.experimental import pallas as pl
from jax.experimental.pallas import tpu as pltpu

F32 = jnp.float32
BF16 = jnp.bfloat16

GRID_W = 64
HEAD_DIM = 128
N_HEADS = 8
N_KV_HEADS = 2
Q_PER_KV = N_HEADS // N_KV_HEADS
ATTN_W = N_HEADS * HEAD_DIM
KV_W = N_KV_HEADS * HEAD_DIM
ROPE_THETA = 10000.0
ROPE_AXIS_DIM = HEAD_DIM // 2
ATTN_SCALE = HEAD_DIM ** -0.5
Q_SCALE = ATTN_SCALE * math.log2(math.e)
GROUP_W = 4 * HEAD_DIM
CHUNK = 128
N_MOD = 6
EPS = 1e-6

OFF_K = ATTN_W
OFF_V = OFF_K + KV_W
OFF_F = OFF_V + KV_W
OFF_CB = OFF_F + GROUP_W
OFF_CC = OFF_CB + GROUP_W
OFF_CH = OFF_CC + GROUP_W
OFF_GU = OFF_CH + GROUP_W
OFF_GV = OFF_GU + GROUP_W
IN_W = OFF_GV + GROUP_W
MIX_W = ATTN_W + 3 * GROUP_W

V7X_VMEM_LIMIT_BYTES = 56 * 1024 * 1024
SUBLANES = 8
MOD_ROWS = 8
FF_CHUNK = 512
KV_CHUNK = 1024
BF16_SUBLANES = 16
VT_ROWS = HEAD_DIM + BF16_SUBLANES
MAX_FIXED_SHIFT = 60.0
DFT_ROWS = 512


def _params(n_axes):
    return pltpu.CompilerParams(dimension_semantics=("arbitrary",) * n_axes,
                                vmem_limit_bytes=V7X_VMEM_LIMIT_BYTES)


def _resident(shape):
    return pl.BlockSpec(shape, lambda *_: (0,) * len(shape), pipeline_mode=pl.Buffered(1))


def _resident_layer(stacked_shape, layer):
    rest = tuple(stacked_shape[1:])
    return pl.BlockSpec((None,) + rest, lambda *_: (layer,) + (0,) * len(rest), pipeline_mode=pl.Buffered(1))


def _dot(a, b):
    return jnp.dot(a, b, preferred_element_type=F32)


def _gelu_tanh(x):
    return 0.5 * x * (1.0 + jnp.tanh(math.sqrt(2.0 / math.pi) * (x + 0.044715 * (x * x * x))))


def _silu(x):
    return x * (1.0 / (1.0 + jnp.exp(-x)))


def _rms(x, eps=EPS):
    return x * lax.rsqrt(jnp.mean(x * x, axis=-1, keepdims=True) + eps)


def _mod_kernel(c_ref, w_ref, b_ref, o_ref):
    s = _silu(c_ref[...])
    o_ref[...] = jnp.dot(s, w_ref[...], preferred_element_type=F32,
                         precision=lax.Precision.HIGHEST) + b_ref[...]


def _modulation(cs, w_mod, b_mod):
    depth, d, n = w_mod.shape
    tn = next(t for t in (1024, 512, 256, 128) if n % t == 0)
    return pl.pallas_call(
        _mod_kernel,
        grid=(depth, n // tn),
        in_specs=[pl.BlockSpec((MOD_ROWS, d), lambda l, j: (0, 0)),
                  pl.BlockSpec((None, d, tn), lambda l, j: (l, 0, j)),
                  pl.BlockSpec((None, 1, tn), lambda l, j: (l, 0, j))],
        out_specs=pl.BlockSpec((None, MOD_ROWS, tn), lambda l, j: (l, 0, j)),
        out_shape=jax.ShapeDtypeStruct((depth, MOD_ROWS, n), F32),
        compiler_params=_params(2),
        name="modulation",
    )(cs, w_mod, b_mod.reshape(depth, 1, n))


def _in_kernel(x_ref, m_ref, g1_ref, w_ref, qg_ref, kg_ref, cos_ref, sin_ref, dftc_ref,
               lng_ref, lnb_ref, ws_ref, bs_ref,
               qt_ref, k_ref, vt_ref, ab_ref, cb_ref, t_ref, gm_ref):
    tm = x_ref.shape[0]
    h = _rms(x_ref[...]) * g1_ref[...]
    h = h * (1.0 + m_ref[1:2, :]) + m_ref[0:1, :]
    hb = h.astype(BF16)

    def proj(lo, width=GROUP_W):
        return _dot(hb, w_ref[:, lo:lo + width])

    cos = cos_ref[...]
    sin = sin_ref[...]
    lane = lax.broadcasted_iota(jnp.int32, (tm, HEAD_DIM), 1)
    first_half = (lane % (ROPE_AXIS_DIM)) < (ROPE_AXIS_DIM // 2)

    def norm_rope(ph, gain):
        y = _rms(ph) * gain
        partner = jnp.where(first_half,
                            pltpu.roll(y, HEAD_DIM - ROPE_AXIS_DIM // 2, 1),
                            pltpu.roll(y, ROPE_AXIS_DIM // 2, 1))
        return y * cos + partner * sin

    qg = qg_ref[...] * Q_SCALE
    for half in range(ATTN_W // GROUP_W):
        p = proj(half * GROUP_W)
        for hh in range(GROUP_W // HEAD_DIM):
            c0 = half * GROUP_W + hh * HEAD_DIM
            y = norm_rope(p[:, hh * HEAD_DIM:(hh + 1) * HEAD_DIM], qg)
            qt_ref[c0:c0 + HEAD_DIM, :] = y.T.astype(BF16)

    p = proj(OFF_K)
    kg = kg_ref[...]
    for hh in range(N_KV_HEADS):
        cols = slice(hh * HEAD_DIM, (hh + 1) * HEAD_DIM)
        k_ref[:, cols] = norm_rope(p[:, cols], kg).astype(BF16)
        r0 = hh * VT_ROWS
        vt_ref[r0:r0 + HEAD_DIM, :] = p[:, KV_W + hh * HEAD_DIM:KV_W + (hh + 1) * HEAD_DIM].T.astype(BF16)
        vt_ref[r0 + HEAD_DIM:r0 + VT_ROWS, :] = jnp.ones((BF16_SUBLANES, tm), BF16)

    p = proj(OFF_F).astype(BF16)
    dftc = dftc_ref[...]
    for g in range(GROUP_W // HEAD_DIM):
        r = _dot(p[:, g * HEAD_DIM:(g + 1) * HEAD_DIM], dftc)
        ab_ref[:, g * HEAD_DIM:(g + 1) * HEAD_DIM] = r[:, :HEAD_DIM].astype(BF16)
        ab_ref[:, GROUP_W + g * HEAD_DIM:GROUP_W + (g + 1) * HEAD_DIM] = r[:, HEAD_DIM:].astype(BF16)

    cb_ref[...] = proj(OFF_CB).astype(BF16)
    t_ref[...] = (proj(OFF_CC) * proj(OFF_CH)).astype(BF16)

    u = _gelu_tanh(proj(OFF_GU))
    gv = _gelu_tanh(proj(OFF_GV))
    gc = gv - jnp.mean(gv, axis=-1, keepdims=True)
    vn = gc * lax.rsqrt(jnp.mean(gc * gc, axis=-1, keepdims=True) + EPS) * lng_ref[...] + lnb_ref[...]
    vn = vn.astype(BF16)
    for g in range(GROUP_W // HEAD_DIM):
        wsg = ws_ref[g]
        bsg = bs_ref[g]
        for c in range(tm // CHUNK):
            rows = slice(c * CHUNK, (c + 1) * CHUNK)
            cols = slice(g * HEAD_DIM, (g + 1) * HEAD_DIM)
            s = _dot(wsg, vn[rows, cols]) + bsg
            gm_ref[rows, cols] = (u[rows, cols] * s).astype(BF16)


def _in_proj(xall, modt, layer, g1, w_in, qg, kg, cos_t, sin_t, dftc, lng, lnb, ws, bsf, *, lay):
    r_rows, d = xall.shape
    tm = lay["tm"]
    grp, nlt, tps = lay["grp"], lay["n_lat_tiles"], lay["tiles_per_seq"]
    row = lambda w: pl.BlockSpec((tm, w), lambda i: (i, 0))
    vec = lambda w: pl.BlockSpec((1, w), lambda i: (0, 0))
    tab = pl.BlockSpec((tm, HEAD_DIM), lambda i: (jnp.where(i < nlt, i % tps, tps), 0))
    col = lambda h: pl.BlockSpec((h, tm), lambda i: (0, i))
    rows_bf16 = lambda w: jax.ShapeDtypeStruct((r_rows, w), BF16)
    cols_bf16 = lambda h: jax.ShapeDtypeStruct((h, r_rows), BF16)
    return pl.pallas_call(
        _in_kernel,
        grid=(r_rows // tm,),
        in_specs=[row(d),
                  pl.BlockSpec((None, MOD_ROWS, d), lambda i: (layer, 0, grp(i))),
                  vec(d), _resident_layer(w_in.shape, layer), vec(HEAD_DIM), vec(HEAD_DIM), tab, tab,
                  _resident(dftc.shape), vec(GROUP_W), vec(GROUP_W),
                  _resident(ws.shape), _resident(bsf.shape)],
        out_specs=[col(ATTN_W), row(KV_W), col(N_KV_HEADS * VT_ROWS), row(2 * GROUP_W), row(GROUP_W), row(GROUP_W),
                   row(GROUP_W)],
        out_shape=[cols_bf16(ATTN_W), rows_bf16(KV_W), cols_bf16(N_KV_HEADS * VT_ROWS), rows_bf16(2 * GROUP_W),
                   rows_bf16(GROUP_W), rows_bf16(GROUP_W), rows_bf16(GROUP_W)],
        compiler_params=_params(1),
        name="in_proj",
    )(xall, modt, g1, w_in, qg, kg, cos_t, sin_t, dftc, lng, lnb, ws, bsf)


def _dft_kernel(fc_ref, fs_ref, ab_ref, o_ref):
    r = _dot(fc_ref[...], ab_ref[:, :GROUP_W]) + _dot(fs_ref[...], ab_ref[:, GROUP_W:])
    o_ref[...] = r.astype(BF16)


def _pos_dft(fmats, ab, *, batch, n, first_row):
    tmd = min(DFT_ROWS, n)
    blk0 = first_row // n
    fspec = pl.BlockSpec((tmd, n), lambda b, i: (i, 0))
    return pl.pallas_call(
        _dft_kernel,
        grid=(batch, n // tmd),
        in_specs=[fspec, fspec, pl.BlockSpec((n, 2 * GROUP_W), lambda b, i: (blk0 + b, 0))],
        out_specs=pl.BlockSpec((tmd, GROUP_W), lambda b, i: (b * (n // tmd) + i, 0)),
        out_shape=jax.ShapeDtypeStruct((batch * n, GROUP_W), BF16),
        compiler_params=_params(2),
        name="pos_dft",
    )(*fmats, ab)


def _attn_kernel(shift_ref, qt_ref, kl_ref, kc_ref, vtl_ref, vtc_ref, o_ref, *, nq, with_ctx_queries, fixed_shift):
    tq = qt_ref.shape[1]
    seq = kl_ref.shape[0]
    cols = Q_PER_KV * tq
    qt = jnp.concatenate([qt_ref[g * HEAD_DIM:(g + 1) * HEAD_DIM, :] for g in range(Q_PER_KV)], axis=1)

    def finish(l, acc):
        o = acc * (1.0 / l)
        for g in range(Q_PER_KV):
            o_ref[:, g * HEAD_DIM:(g + 1) * HEAD_DIM] = o[:, g * tq:(g + 1) * tq].T.astype(BF16)

    def key_chunks(with_latent_keys):
        chunks = [(kc_ref, vtc_ref, slice(None))]
        if with_latent_keys:
            chunks += [(kl_ref, vtl_ref, slice(j * KV_CHUNK, (j + 1) * KV_CHUNK)) for j in range(seq // KV_CHUNK)]
        return chunks

    def run_fixed(with_latent_keys):
        shift = shift_ref[0, 0]
        acc = None
        for k_ref, vt_ref, ks in key_chunks(with_latent_keys):
            p = jnp.exp2(_dot(k_ref[ks, :], qt) - shift).astype(BF16)
            part = _dot(vt_ref[:, ks], p)
            acc = part if acc is None else acc + part
        finish(acc[HEAD_DIM:HEAD_DIM + 1], acc[:HEAD_DIM])

    def run_online(with_latent_keys):
        m = jnp.full((1, cols), -jnp.inf, F32)
        l = jnp.zeros((1, cols), F32)
        acc = jnp.zeros((HEAD_DIM, cols), F32)
        for k_ref, vt_ref, ks in key_chunks(with_latent_keys):
            s = _dot(k_ref[ks, :], qt)
            m_new = jnp.maximum(m, jnp.max(s, axis=0, keepdims=True))
            alpha = jnp.exp2(m - m_new)
            p = jnp.exp2(s - m_new)
            l = alpha * l + jnp.sum(p, axis=0, keepdims=True)
            acc = alpha * acc + _dot(vt_ref[:HEAD_DIM, ks], p.astype(BF16))
            m = m_new
        finish(l, acc)

    run = run_fixed if fixed_shift else run_online
    if not with_ctx_queries:
        run(True)
    else:
        is_latent = pl.program_id(2) < nq
        pl.when(is_latent)(lambda: run(True))
        pl.when(jnp.logical_not(is_latent))(lambda: run(False))


def _attention(qt, k, vt, q_gain, k_gain, *, batch, seq, ctx_len, with_ctx_queries):
    r_rows = k.shape[0]
    tq = ctx_len
    nq = seq // tq
    n_lat_q = batch * nq
    ctx_blk0 = batch * seq // ctx_len
    qw = Q_PER_KV * HEAD_DIM
    q_blk = lambda b, i: jnp.where(i < nq, b * nq + i, n_lat_q + b)

    def call(fixed_shift, shift):
        return pl.pallas_call(
            functools.partial(_attn_kernel, nq=nq, with_ctx_queries=with_ctx_queries, fixed_shift=fixed_shift),
            grid=(batch, N_KV_HEADS, nq + (1 if with_ctx_queries else 0)),
            in_specs=[pl.BlockSpec(memory_space=pltpu.SMEM),
                      pl.BlockSpec((qw, tq), lambda b, hg, i: (hg, q_blk(b, i))),
                      pl.BlockSpec((seq, HEAD_DIM), lambda b, hg, i: (b, hg)),
                      pl.BlockSpec((ctx_len, HEAD_DIM), lambda b, hg, i: (ctx_blk0 + b, hg)),
                      pl.BlockSpec((VT_ROWS, seq), lambda b, hg, i: (hg, b)),
                      pl.BlockSpec((VT_ROWS, ctx_len), lambda b, hg, i: (hg, ctx_blk0 + b))],
            out_specs=pl.BlockSpec((tq, qw), lambda b, hg, i: (q_blk(b, i), hg)),
            out_shape=jax.ShapeDtypeStruct((r_rows, ATTN_W), BF16),
            compiler_params=_params(3),
            name="attention_fixed" if fixed_shift else "attention_online",
        )(shift, qt, k, k, vt, vt)

    bound = (HEAD_DIM * Q_SCALE * (1.0 + 2.0 ** -6)) * jnp.max(jnp.abs(q_gain)) * jnp.max(jnp.abs(k_gain))
    shift = bound.astype(F32).reshape(1, 1)
    return lax.cond(bound <= MAX_FIXED_SHIFT, lambda: call(True, shift), lambda: call(False, shift))


def _seq_edge_masks(tile, tm, n_lat_tiles, seq, ctx_len):
    r = lax.broadcasted_iota(jnp.int32, (tm, 1), 0)
    period = jnp.where(tile < n_lat_tiles, seq, ctx_len)
    pos = (tile * tm + r) & (period - 1)
    return pos == 0, pos == period - 1


def _dwconv3(center, prev_row, next_row, w_ref, is_start, is_end):
    tm = center.shape[0]
    r = lax.broadcasted_iota(jnp.int32, (tm, 1), 0)
    up = jnp.where(r == 0, prev_row, pltpu.roll(center, 1, 0))
    dn = jnp.where(r == tm - 1, next_row, pltpu.roll(center, tm - 1, 0))
    up = jnp.where(is_start, 0.0, up)
    dn = jnp.where(is_end, 0.0, dn)
    return up * w_ref[0:1, :] + center * w_ref[1:2, :] + dn * w_ref[2:3, :]


def _out_kernel(attn_ref, four_ref, cb_ref, t_ref, tp_ref, tn_ref, gm_ref, cw_ref, w_ref, x_ref, m_ref,
                o_ref, mix_ref, *, n_lat_tiles, seq, ctx_len):
    tm = x_ref.shape[0]
    i = pl.program_id(0)
    is_start, is_end = _seq_edge_masks(i, tm, n_lat_tiles, seq, ctx_len)
    conv = _dwconv3(t_ref[...].astype(F32), tp_ref[SUBLANES - 1:SUBLANES, :].astype(F32),
                    tn_ref[0:1, :].astype(F32), cw_ref, is_start, is_end)
    mix_ref[:, :ATTN_W] = attn_ref[...]
    mix_ref[:, ATTN_W:ATTN_W + GROUP_W] = four_ref[...]
    mix_ref[:, ATTN_W + GROUP_W:ATTN_W + 2 * GROUP_W] = (cb_ref[...].astype(F32) * conv).astype(BF16)
    mix_ref[:, ATTN_W + 2 * GROUP_W:] = gm_ref[...]
    o_ref[...] = x_ref[...] + m_ref[2:3, :] * _dot(mix_ref[...], w_ref[...])


def _halo_specs(tm, width, n_rows):
    per = tm // SUBLANES
    last = n_rows // SUBLANES - 1
    prev = pl.BlockSpec((SUBLANES, width), lambda i, *_: (jnp.maximum(i * per - 1, 0), 0))
    nxt = pl.BlockSpec((SUBLANES, width), lambda i, *_: (jnp.minimum((i + 1) * per, last), 0))
    return prev, nxt


def _out_proj(attn, four, cb, t, gm, conv_w, w_out, xall, modt, layer, *, lay, n_rows):
    d = xall.shape[1]
    tm = lay["tm"]
    grp = lay["grp"]
    row = lambda w: pl.BlockSpec((tm, w), lambda i: (i, 0))
    tprev, tnext = _halo_specs(tm, GROUP_W, t.shape[0])
    kern = functools.partial(_out_kernel, n_lat_tiles=lay["n_lat_tiles"], seq=lay["seq"], ctx_len=lay["ctx_len"])
    return pl.pallas_call(
        kern,
        grid=(n_rows // tm,),
        in_specs=[row(ATTN_W), row(GROUP_W), row(GROUP_W), row(GROUP_W), tprev, tnext, row(GROUP_W),
                  _resident(conv_w.shape), _resident_layer(w_out.shape, layer), row(d),
                  pl.BlockSpec((None, MOD_ROWS, d), lambda i: (layer, 0, grp(i)))],
        out_specs=row(d),
        out_shape=jax.ShapeDtypeStruct((n_rows, d), F32),
        scratch_shapes=[pltpu.VMEM((tm, MIX_W), BF16)],
        compiler_params=_params(1),
        name="out_proj",
    )(attn, four, cb, t, t, t, gm, conv_w, w_out, xall, modt)


def _ffn_kernel(x_ref, xp_ref, xn_ref, m_ref, g2_ref, wg_ref, wu_ref, cw_ref, cb_ref, wd_ref, fg_ref,
                o_ref, h_ref, acc_ref, *, n_lat_tiles, seq, ctx_len, final_norm):
    tm = x_ref.shape[0]
    i = pl.program_id(0)
    j = pl.program_id(1)

    def norm_mod(x):
        return (_rms(x) * g2_ref[...] * (1.0 + m_ref[4:5, :]) + m_ref[3:4, :]).astype(BF16)

    @pl.when(j == 0)
    def _():
        h_ref[0:SUBLANES, :] = norm_mod(xp_ref[...])
        h_ref[SUBLANES:SUBLANES + tm, :] = norm_mod(x_ref[...])
        h_ref[SUBLANES + tm:, :] = norm_mod(xn_ref[...])
        acc_ref[...] = jnp.zeros_like(acc_ref)

    is_start, is_end = _seq_edge_masks(i, tm, n_lat_tiles, seq, ctx_len)
    g = _dot(h_ref[...], wg_ref[...])
    u = _dot(h_ref[SUBLANES:SUBLANES + tm, :], wu_ref[...])
    conv = _dwconv3(g[SUBLANES:SUBLANES + tm], g[SUBLANES - 1:SUBLANES], g[SUBLANES + tm:SUBLANES + tm + 1],
                    cw_ref, is_start, is_end)
    act = (_silu(conv + cb_ref[...]) * u).astype(BF16)
    acc_ref[...] += _dot(act, wd_ref[...])

    @pl.when(j == pl.num_programs(1) - 1)
    def _():
        y = x_ref[...] + m_ref[5:6, :] * acc_ref[...]
        if final_norm:
            y = _rms(y) * fg_ref[...]
        o_ref[...] = y


def _ffn(xall, modt, layer, g2, w_up, conv_w, conv_b, w_down, final_g, *, lay, n_rows, final_norm):
    d = xall.shape[1]
    d_ff = w_down.shape[1]
    tm = lay["tm"]
    grp = lay["grp"]
    nj = d_ff // FF_CHUNK
    xprev, xnext = _halo_specs(tm, d, xall.shape[0])
    kern = functools.partial(_ffn_kernel, n_lat_tiles=lay["n_lat_tiles"], seq=lay["seq"],
                             ctx_len=lay["ctx_len"], final_norm=final_norm)
    return pl.pallas_call(
        kern,
        grid=(n_rows // tm, nj),
        in_specs=[pl.BlockSpec((tm, d), lambda i, j: (i, 0)), xprev, xnext,
                  pl.BlockSpec((None, MOD_ROWS, d), lambda i, j: (layer, 0, grp(i))),
                  pl.BlockSpec((1, d), lambda i, j: (0, 0)),
                  pl.BlockSpec((None, d, FF_CHUNK), lambda i, j: (layer, 0, j)),
                  pl.BlockSpec((None, d, FF_CHUNK), lambda i, j: (layer, 0, nj + j)),
                  pl.BlockSpec((SUBLANES, FF_CHUNK), lambda i, j: (0, j)),
                  pl.BlockSpec((1, FF_CHUNK), lambda i, j: (0, j)),
                  pl.BlockSpec((None, FF_CHUNK, d), lambda i, j: (layer, j, 0)),
                  pl.BlockSpec((1, d), lambda i, j: (0, 0))],
        out_specs=pl.BlockSpec((tm, d), lambda i, j: (i, 0)),
        out_shape=jax.ShapeDtypeStruct((n_rows, d), F32),
        scratch_shapes=[pltpu.VMEM((tm + 2 * SUBLANES, d), BF16), pltpu.VMEM((tm, d), F32)],
        compiler_params=_params(2),
        name="ffn",
    )(xall, xall, xall, modt, g2, w_up, w_up, conv_w, conv_b, w_down, final_g)


def _rope_tables(seq, tm):
    pos = jnp.arange(seq, dtype=jnp.int32)
    row = (pos // GRID_W).astype(F32)
    col = (pos % GRID_W).astype(F32)
    freqs = ROPE_THETA ** (-jnp.arange(0, ROPE_AXIS_DIM, 2, dtype=F32) / ROPE_AXIS_DIM)
    ang_r = row[:, None] * freqs[None, :]
    ang_c = col[:, None] * freqs[None, :]
    cos = jnp.concatenate([jnp.cos(ang_r)] * 2 + [jnp.cos(ang_c)] * 2, axis=-1)
    sin = jnp.concatenate([-jnp.sin(ang_r), jnp.sin(ang_r), -jnp.sin(ang_c), jnp.sin(ang_c)], axis=-1)
    cos = jnp.concatenate([cos, jnp.ones((tm, HEAD_DIM), F32)], axis=0)
    sin = jnp.concatenate([sin, jnp.zeros((tm, HEAD_DIM), F32)], axis=0)
    return cos, sin


def _dft_matrix(n, scale):
    n1 = 1
    while n1 * n1 < n:
        n1 *= 2
    n2 = n // n1
    t = jnp.arange(n, dtype=jnp.int32)[None, :]
    a = jnp.arange(n1, dtype=jnp.int32)[:, None]
    b = jnp.arange(n2, dtype=jnp.int32)[:, None]
    ang_a = ((a * t) % n1).astype(F32) * (2.0 * math.pi / n1)
    ang_b = ((b * t) % n).astype(F32) * (2.0 * math.pi / n)
    ca, sa = jnp.cos(ang_a)[:, None, :], jnp.sin(ang_a)[:, None, :]
    cb, sb = jnp.cos(ang_b)[None, :, :] * scale, jnp.sin(ang_b)[None, :, :] * scale
    c = (ca * cb - sa * sb).reshape(n, n)
    s = (sa * cb + ca * sb).reshape(n, n)
    return c.astype(BF16), (-s).astype(BF16)


def _channel_dft(scale):
    k = jnp.arange(HEAD_DIM, dtype=jnp.int32)
    ang = ((k[:, None] * k[None, :]) % HEAD_DIM).astype(F32) * (2.0 * math.pi / HEAD_DIM)
    return (jnp.concatenate([jnp.cos(ang), jnp.sin(ang)], axis=1) * scale).astype(BF16)


def _layout(batch, seq, ctx_len):
    n_lat, n_ctx = batch * seq, batch * ctx_len
    tm = 512
    while n_ctx % tm or seq % tm:
        tm //= 2
    assert tm >= CHUNK and seq & (seq - 1) == 0 and ctx_len & (ctx_len - 1) == 0
    assert seq % GRID_W == 0 and ctx_len % CHUNK == 0 and n_lat % ctx_len == 0 and n_lat % seq == 0
    n_lat_tiles, tps = n_lat // tm, seq // tm
    grp = lambda i: jnp.where(i < n_lat_tiles, 1 + i // tps, 0)
    return dict(tm=tm, n_lat_tiles=n_lat_tiles, tiles_per_seq=tps, grp=grp, seq=seq, ctx_len=ctx_len,
                n_lat=n_lat, n_ctx=n_ctx)


def kernel(x, c, ctx, c_ctx, w_mod, b_mod, norm1_g, norm2_g, w_in, q_norm_g, k_norm_g, conv_w, gm_ln_g,
           gm_ln_b, gm_ws, gm_b, w_out, w_up, ffn_conv_w, ffn_conv_b, w_down, final_norm_g):
    batch, seq, d = x.shape
    ctx_len = ctx.shape[1]
    depth = w_mod.shape[0]
    d_ff = w_down.shape[1]
    assert batch + 1 <= MOD_ROWS and w_in.shape[2] == IN_W and w_down.shape[1] % FF_CHUNK == 0
    lay = _layout(batch, seq, ctx_len)
    n_lat, tm = lay["n_lat"], lay["tm"]

    xall = jnp.concatenate([x.reshape(n_lat, d), ctx.reshape(batch * ctx_len, d)], axis=0)

    cs = jnp.concatenate([c_ctx[None, :], c, jnp.zeros((MOD_ROWS - 1 - batch, d), F32)], axis=0)
    mod = _modulation(cs, w_mod, b_mod)
    modt = mod.reshape(depth, MOD_ROWS, N_MOD, d).transpose(0, 2, 1, 3).reshape(depth, N_MOD, MOD_ROWS * d)
    modt = jnp.pad(modt, ((0, 0), (0, MOD_ROWS - N_MOD), (0, 0)))

    cos_t, sin_t = _rope_tables(seq, tm)
    f_lat = _dft_matrix(seq, seq ** -0.5)
    f_ctx = _dft_matrix(ctx_len, ctx_len ** -0.5)
    dftc = _channel_dft(HEAD_DIM ** -0.5)

    pad_rows = lambda w: jnp.pad(w, ((0, 0), (0, SUBLANES - w.shape[1]), (0, 0)))
    conv_w8 = pad_rows(conv_w)
    ffn_conv_w8 = pad_rows(ffn_conv_w)
    bsf = jnp.broadcast_to(gm_b[..., None], gm_b.shape + (HEAD_DIM,))
    w_in_b, w_out_b, w_up_b, w_down_b = (w.astype(BF16) for w in (w_in, w_out, w_up, w_down))
    gm_ws_b = gm_ws.astype(BF16)
    fg = final_norm_g.reshape(1, d)

    for l in range(depth):
        last = l == depth - 1
        qt, k, vt, ab, cb, t, gm = _in_proj(
            xall, modt, l, norm1_g[l].reshape(1, d), w_in_b, q_norm_g[l].reshape(1, HEAD_DIM),
            k_norm_g[l].reshape(1, HEAD_DIM), cos_t, sin_t, dftc, gm_ln_g[l].reshape(1, GROUP_W),
            gm_ln_b[l].reshape(1, GROUP_W), gm_ws_b[l], bsf[l], lay=lay)
        four = _pos_dft(f_lat, ab, batch=batch, n=seq, first_row=0)
        if not last:
            four_ctx = _pos_dft(f_ctx, ab, batch=batch, n=ctx_len, first_row=n_lat)
            four = jnp.concatenate([four, four_ctx], axis=0)
        attn = _attention(qt, k, vt, q_norm_g[l], k_norm_g[l], batch=batch, seq=seq, ctx_len=ctx_len,
                          with_ctx_queries=not last)
        n_rows = n_lat if last else xall.shape[0]
        xall = _out_proj(attn, four, cb, t, gm, conv_w8[l], w_out_b, xall, modt, l, lay=lay, n_rows=n_rows)
        xall = _ffn(xall, modt, l, norm2_g[l].reshape(1, d), w_up_b, ffn_conv_w8[l],
                    ffn_conv_b[l].reshape(1, d_ff), w_down_b, fg, lay=lay, n_rows=n_rows, final_norm=last)
    return xall.reshape(batch, seq, d)
```

```python
import functools
import math

import jax
import jax.numpy as jnp
from jax import lax
from jax.experimental import pallas as pl
from jax.experimental.pallas import tpu as pltpu

F32 = jnp.float32
BF16 = jnp.bfloat16

GRID_W = 64
HEAD_DIM = 128
N_HEADS = 8
N_KV_HEADS = 2
Q_PER_KV = N_HEADS // N_KV_HEADS
ATTN_W = N_HEADS * HEAD_DIM
KV_W = N_KV_HEADS * HEAD_DIM
ROPE_THETA = 10000.0
ROPE_AXIS_DIM = HEAD_DIM // 2
ATTN_SCALE = HEAD_DIM ** -0.5
Q_SCALE = ATTN_SCALE * math.log2(math.e)
GROUP_W = 4 * HEAD_DIM
CHUNK = 128
N_MOD = 6
EPS = 1e-6

OFF_K = ATTN_W
OFF_V = OFF_K + KV_W
OFF_F = OFF_V + KV_W
OFF_CB = OFF_F + GROUP_W
OFF_CC = OFF_CB + GROUP_W
OFF_CH = OFF_CC + GROUP_W
OFF_GU = OFF_CH + GROUP_W
OFF_GV = OFF_GU + GROUP_W
IN_W = OFF_GV + GROUP_W
MIX_W = ATTN_W + 3 * GROUP_W

V7X_VMEM_LIMIT_BYTES = 56 * 1024 * 1024
SUBLANES = 8
MOD_ROWS = 8
FF_CHUNK = 512
KV_CHUNK = 1024
BF16_SUBLANES = 16
VT_ROWS = HEAD_DIM + BF16_SUBLANES
MAX_FIXED_SHIFT = 60.0
DFT_ROWS = 512


def _params(n_axes):
    return pltpu.CompilerParams(dimension_semantics=("arbitrary",) * n_axes,
                                vmem_limit_bytes=V7X_VMEM_LIMIT_BYTES)


def _resident(shape):
    return pl.BlockSpec(shape, lambda *_: (0,) * len(shape), pipeline_mode=pl.Buffered(1))


def _resident_layer(stacked_shape, layer):
    rest = tuple(stacked_shape[1:])
    return pl.BlockSpec((None,) + rest, lambda *_: (layer,) + (0,) * len(rest), pipeline_mode=pl.Buffered(1))


def _dot(a, b):
    return jnp.dot(a, b, preferred_element_type=F32)


def _gelu_tanh(x):
    return 0.5 * x * (1.0 + jnp.tanh(math.sqrt(2.0 / math.pi) * (x + 0.044715 * (x * x * x))))


def _silu(x):
    return x * (1.0 / (1.0 + jnp.exp(-x)))


def _rms(x, eps=EPS):
    return x * lax.rsqrt(jnp.mean(x * x, axis=-1, keepdims=True) + eps)


def _mod_kernel(c_ref, w_ref, b_ref, o_ref):
    s = _silu(c_ref[...])
    o_ref[...] = jnp.dot(s, w_ref[...], preferred_element_type=F32,
                         precision=lax.Precision.HIGHEST) + b_ref[...]


def _modulation(cs, w_mod, b_mod):
    depth, d, n = w_mod.shape
    tn = next(t for t in (1024, 512, 256, 128) if n % t == 0)
    return pl.pallas_call(
        _mod_kernel,
        grid=(depth, n // tn),
        in_specs=[pl.BlockSpec((MOD_ROWS, d), lambda l, j: (0, 0)),
                  pl.BlockSpec((None, d, tn), lambda l, j: (l, 0, j)),
                  pl.BlockSpec((None, 1, tn), lambda l, j: (l, 0, j))],
        out_specs=pl.BlockSpec((None, MOD_ROWS, tn), lambda l, j: (l, 0, j)),
        out_shape=jax.ShapeDtypeStruct((depth, MOD_ROWS, n), F32),
        compiler_params=_params(2),
        name="modulation",
    )(cs, w_mod, b_mod.reshape(depth, 1, n))


def _in_kernel(x_ref, m_ref, g1_ref, w_ref, qg_ref, kg_ref, cos_ref, sin_ref, dftc_ref,
               lng_ref, lnb_ref, ws_ref, bs_ref,
               qt_ref, k_ref, vt_ref, ab_ref, cb_ref, t_ref, gm_ref):
    tm = x_ref.shape[0]
    h = _rms(x_ref[...]) * g1_ref[...]
    h = h * (1.0 + m_ref[1:2, :]) + m_ref[0:1, :]
    hb = h.astype(BF16)

    def proj(lo, width=GROUP_W):
        return _dot(hb, w_ref[:, lo:lo + width])

    cos = cos_ref[...]
    sin = sin_ref[...]
    lane = lax.broadcasted_iota(jnp.int32, (tm, HEAD_DIM), 1)
    first_half = (lane % (ROPE_AXIS_DIM)) < (ROPE_AXIS_DIM // 2)

    def norm_rope(ph, gain):
        y = _rms(ph) * gain
        partner = jnp.where(first_half,
                            pltpu.roll(y, HEAD_DIM - ROPE_AXIS_DIM // 2, 1),
                            pltpu.roll(y, ROPE_AXIS_DIM // 2, 1))
        return y * cos + partner * sin

    qg = qg_ref[...] * Q_SCALE
    for half in range(ATTN_W // GROUP_W):
        p = proj(half * GROUP_W)
        for hh in range(GROUP_W // HEAD_DIM):
            c0 = half * GROUP_W + hh * HEAD_DIM
            y = norm_rope(p[:, hh * HEAD_DIM:(hh + 1) * HEAD_DIM], qg)
            qt_ref[c0:c0 + HEAD_DIM, :] = y.T.astype(BF16)

    p = proj(OFF_K)
    kg = kg_ref[...]
    for hh in range(N_KV_HEADS):
        cols = slice(hh * HEAD_DIM, (hh + 1) * HEAD_DIM)
        k_ref[:, cols] = norm_rope(p[:, cols], kg).astype(BF16)
        r0 = hh * VT_ROWS
        vt_ref[r0:r0 + HEAD_DIM, :] = p[:, KV_W + hh * HEAD_DIM:KV_W + (hh + 1) * HEAD_DIM].T.astype(BF16)
        vt_ref[r0 + HEAD_DIM:r0 + VT_ROWS, :] = jnp.ones((BF16_SUBLANES, tm), BF16)

    p = proj(OFF_F).astype(BF16)
    dftc = dftc_ref[...]
    for g in range(GROUP_W // HEAD_DIM):
        r = _dot(p[:, g * HEAD_DIM:(g + 1) * HEAD_DIM], dftc)
        ab_ref[:, g * HEAD_DIM:(g + 1) * HEAD_DIM] = r[:, :HEAD_DIM].astype(BF16)
        ab_ref[:, GROUP_W + g * HEAD_DIM:GROUP_W + (g + 1) * HEAD_DIM] = r[:, HEAD_DIM:].astype(BF16)

    cb_ref[...] = proj(OFF_CB).astype(BF16)
    t_ref[...] = (proj(OFF_CC) * proj(OFF_CH)).astype(BF16)

    u = _gelu_tanh(proj(OFF_GU))
    gv = _gelu_tanh(proj(OFF_GV))
    gc = gv - jnp.mean(gv, axis=-1, keepdims=True)
    vn = gc * lax.rsqrt(jnp.mean(gc * gc, axis=-1, keepdims=True) + EPS) * lng_ref[...] + lnb_ref[...]
    vn = vn.astype(BF16)
    for g in range(GROUP_W // HEAD_DIM):
        wsg = ws_ref[g]
        bsg = bs_ref[g]
        for c in range(tm // CHUNK):
            rows = slice(c * CHUNK, (c + 1) * CHUNK)
            cols = slice(g * HEAD_DIM, (g + 1) * HEAD_DIM)
            s = _dot(wsg, vn[rows, cols]) + bsg
            gm_ref[rows, cols] = (u[rows, cols] * s).astype(BF16)


def _in_proj(xall, modt, layer, g1, w_in, qg, kg, cos_t, sin_t, dftc, lng, lnb, ws, bsf, *, lay):
    r_rows, d = xall.shape
    tm = lay["tm"]
    grp, nlt, tps = lay["grp"], lay["n_lat_tiles"], lay["tiles_per_seq"]
    row = lambda w: pl.BlockSpec((tm, w), lambda i: (i, 0))
    vec = lambda w: pl.BlockSpec((1, w), lambda i: (0, 0))
    tab = pl.BlockSpec((tm, HEAD_DIM), lambda i: (jnp.where(i < nlt, i % tps, tps), 0))
    col = lambda h: pl.BlockSpec((h, tm), lambda i: (0, i))
    rows_bf16 = lambda w: jax.ShapeDtypeStruct((r_rows, w), BF16)
    cols_bf16 = lambda h: jax.ShapeDtypeStruct((h, r_rows), BF16)
    return pl.pallas_call(
        _in_kernel,
        grid=(r_rows // tm,),
        in_specs=[row(d),
                  pl.BlockSpec((None, MOD_ROWS, d), lambda i: (layer, 0, grp(i))),
                  vec(d), _resident_layer(w_in.shape, layer), vec(HEAD_DIM), vec(HEAD_DIM), tab, tab,
                  _resident(dftc.shape), vec(GROUP_W), vec(GROUP_W),
                  _resident(ws.shape), _resident(bsf.shape)],
        out_specs=[col(ATTN_W), row(KV_W), col(N_KV_HEADS * VT_ROWS), row(2 * GROUP_W), row(GROUP_W), row(GROUP_W),
                   row(GROUP_W)],
        out_shape=[cols_bf16(ATTN_W), rows_bf16(KV_W), cols_bf16(N_KV_HEADS * VT_ROWS), rows_bf16(2 * GROUP_W),
                   rows_bf16(GROUP_W), rows_bf16(GROUP_W), rows_bf16(GROUP_W)],
        compiler_params=_params(1),
        name="in_proj",
    )(xall, modt, g1, w_in, qg, kg, cos_t, sin_t, dftc, lng, lnb, ws, bsf)


def _dft_kernel(fc_ref, fs_ref, ab_ref, *rest):
    o_ref = rest[-1]
    r = _dot(fc_ref[...], ab_ref[:, :GROUP_W]) + _dot(fs_ref[...], ab_ref[:, GROUP_W:])
    o_ref[...] = r.astype(BF16)


def _pos_dft(fmats, ab, *, batch, n, first_row, into=None):
    tmd = min(DFT_ROWS, n)
    blk0 = first_row // n
    fspec = pl.BlockSpec((tmd, n), lambda b, i: (i, 0))
    extra = [] if into is None else [into]
    return pl.pallas_call(
        _dft_kernel,
        grid=(batch, n // tmd),
        in_specs=[fspec, fspec, pl.BlockSpec((n, 2 * GROUP_W), lambda b, i: (blk0 + b, 0))]
                 + [pl.BlockSpec(memory_space=pl.ANY)] * len(extra),
        out_specs=pl.BlockSpec((tmd, GROUP_W), lambda b, i: ((blk0 + b) * (n // tmd) + i, 0)),
        out_shape=jax.ShapeDtypeStruct((ab.shape[0], GROUP_W), BF16),
        input_output_aliases={3: 0} if extra else {},
        compiler_params=_params(2),
        name="pos_dft",
    )(*fmats, ab, *extra)


def _attn_kernel(shift_ref, qt_ref, kl_ref, kc_ref, vtl_ref, vtc_ref, o_ref, *, nq, with_ctx_queries, fixed_shift):
    tq = qt_ref.shape[1]
    seq = kl_ref.shape[0]
    cols = Q_PER_KV * tq
    qt = jnp.concatenate([qt_ref[g * HEAD_DIM:(g + 1) * HEAD_DIM, :] for g in range(Q_PER_KV)], axis=1)

    def finish(l, acc):
        o = acc * (1.0 / l)
        for g in range(Q_PER_KV):
            o_ref[:, g * HEAD_DIM:(g + 1) * HEAD_DIM] = o[:, g * tq:(g + 1) * tq].T.astype(BF16)

    def key_chunks(with_latent_keys):
        chunks = [(kc_ref, vtc_ref, slice(None))]
        if with_latent_keys:
            chunks += [(kl_ref, vtl_ref, slice(j * KV_CHUNK, (j + 1) * KV_CHUNK)) for j in range(seq // KV_CHUNK)]
        return chunks

    def run_fixed(with_latent_keys):
        shift = shift_ref[0, 0]
        acc = None
        for k_ref, vt_ref, ks in key_chunks(with_latent_keys):
            p = jnp.exp2(_dot(k_ref[ks, :], qt) - shift).astype(BF16)
            part = _dot(vt_ref[:, ks], p)
            acc = part if acc is None else acc + part
        finish(acc[HEAD_DIM:HEAD_DIM + 1], acc[:HEAD_DIM])

    def run_online(with_latent_keys):
        m = jnp.full((1, cols), -jnp.inf, F32)
        l = jnp.zeros((1, cols), F32)
        acc = jnp.zeros((HEAD_DIM, cols), F32)
        for k_ref, vt_ref, ks in key_chunks(with_latent_keys):
            s = _dot(k_ref[ks, :], qt)
            m_new = jnp.maximum(m, jnp.max(s, axis=0, keepdims=True))
            alpha = jnp.exp2(m - m_new)
            p = jnp.exp2(s - m_new)
            l = alpha * l + jnp.sum(p, axis=0, keepdims=True)
            acc = alpha * acc + _dot(vt_ref[:HEAD_DIM, ks], p.astype(BF16))
            m = m_new
        finish(l, acc)

    run = run_fixed if fixed_shift else run_online
    if not with_ctx_queries:
        run(True)
    else:
        is_latent = pl.program_id(2) < nq
        pl.when(is_latent)(lambda: run(True))
        pl.when(jnp.logical_not(is_latent))(lambda: run(False))


def _attention(qt, k, vt, q_gain, k_gain, *, batch, seq, ctx_len, with_ctx_queries):
    r_rows = k.shape[0]
    tq = ctx_len
    nq = seq // tq
    n_lat_q = batch * nq
    ctx_blk0 = batch * seq // ctx_len
    qw = Q_PER_KV * HEAD_DIM
    q_blk = lambda b, i: jnp.where(i < nq, b * nq + i, n_lat_q + b)

    def call(fixed_shift, shift):
        return pl.pallas_call(
            functools.partial(_attn_kernel, nq=nq, with_ctx_queries=with_ctx_queries, fixed_shift=fixed_shift),
            grid=(batch, N_KV_HEADS, nq + (1 if with_ctx_queries else 0)),
            in_specs=[pl.BlockSpec(memory_space=pltpu.SMEM),
                      pl.BlockSpec((qw, tq), lambda b, hg, i: (hg, q_blk(b, i))),
                      pl.BlockSpec((seq, HEAD_DIM), lambda b, hg, i: (b, hg)),
                      pl.BlockSpec((ctx_len, HEAD_DIM), lambda b, hg, i: (ctx_blk0 + b, hg)),
                      pl.BlockSpec((VT_ROWS, seq), lambda b, hg, i: (hg, b)),
                      pl.BlockSpec((VT_ROWS, ctx_len), lambda b, hg, i: (hg, ctx_blk0 + b))],
            out_specs=pl.BlockSpec((tq, qw), lambda b, hg, i: (q_blk(b, i), hg)),
            out_shape=jax.ShapeDtypeStruct((r_rows, ATTN_W), BF16),
            compiler_params=_params(3),
            name="attention_fixed" if fixed_shift else "attention_online",
        )(shift, qt, k, k, vt, vt)

    bound = (HEAD_DIM * Q_SCALE * (1.0 + 2.0 ** -6)) * jnp.max(jnp.abs(q_gain)) * jnp.max(jnp.abs(k_gain))
    shift = bound.astype(F32).reshape(1, 1)
    return lax.cond(bound <= MAX_FIXED_SHIFT, lambda: call(True, shift), lambda: call(False, shift))


def _seq_edge_masks(tile, tm, n_lat_tiles, seq, ctx_len):
    r = lax.broadcasted_iota(jnp.int32, (tm, 1), 0)
    period = jnp.where(tile < n_lat_tiles, seq, ctx_len)
    pos = (tile * tm + r) & (period - 1)
    return pos == 0, pos == period - 1


def _dwconv3(center, prev_row, next_row, w_ref, is_start, is_end):
    tm = center.shape[0]
    r = lax.broadcasted_iota(jnp.int32, (tm, 1), 0)
    up = jnp.where(r == 0, prev_row, pltpu.roll(center, 1, 0))
    dn = jnp.where(r == tm - 1, next_row, pltpu.roll(center, tm - 1, 0))
    up = jnp.where(is_start, 0.0, up)
    dn = jnp.where(is_end, 0.0, dn)
    return up * w_ref[0:1, :] + center * w_ref[1:2, :] + dn * w_ref[2:3, :]


def _out_kernel(attn_ref, four_ref, cb_ref, t_ref, tp_ref, tn_ref, gm_ref, cw_ref, w_ref, x_ref, m_ref,
                o_ref, mix_ref, *, n_lat_tiles, seq, ctx_len):
    tm = x_ref.shape[0]
    i = pl.program_id(0)
    is_start, is_end = _seq_edge_masks(i, tm, n_lat_tiles, seq, ctx_len)
    conv = _dwconv3(t_ref[...].astype(F32), tp_ref[SUBLANES - 1:SUBLANES, :].astype(F32),
                    tn_ref[0:1, :].astype(F32), cw_ref, is_start, is_end)
    mix_ref[:, :ATTN_W] = attn_ref[...]
    mix_ref[:, ATTN_W:ATTN_W + GROUP_W] = four_ref[...]
    mix_ref[:, ATTN_W + GROUP_W:ATTN_W + 2 * GROUP_W] = (cb_ref[...].astype(F32) * conv).astype(BF16)
    mix_ref[:, ATTN_W + 2 * GROUP_W:] = gm_ref[...]
    o_ref[...] = x_ref[...] + m_ref[2:3, :] * _dot(mix_ref[...], w_ref[...])


def _halo_specs(tm, width, n_rows):
    per = tm // SUBLANES
    last = n_rows // SUBLANES - 1
    prev = pl.BlockSpec((SUBLANES, width), lambda i, *_: (jnp.maximum(i * per - 1, 0), 0))
    nxt = pl.BlockSpec((SUBLANES, width), lambda i, *_: (jnp.minimum((i + 1) * per, last), 0))
    return prev, nxt


def _out_proj(attn, four, cb, t, gm, conv_w, w_out, xall, modt, layer, *, lay, n_rows):
    d = xall.shape[1]
    tm = lay["tm"]
    grp = lay["grp"]
    row = lambda w: pl.BlockSpec((tm, w), lambda i: (i, 0))
    tprev, tnext = _halo_specs(tm, GROUP_W, t.shape[0])
    kern = functools.partial(_out_kernel, n_lat_tiles=lay["n_lat_tiles"], seq=lay["seq"], ctx_len=lay["ctx_len"])
    return pl.pallas_call(
        kern,
        grid=(n_rows // tm,),
        in_specs=[row(ATTN_W), row(GROUP_W), row(GROUP_W), row(GROUP_W), tprev, tnext, row(GROUP_W),
                  _resident(conv_w.shape), _resident_layer(w_out.shape, layer), row(d),
                  pl.BlockSpec((None, MOD_ROWS, d), lambda i: (layer, 0, grp(i)))],
        out_specs=row(d),
        out_shape=jax.ShapeDtypeStruct((n_rows, d), F32),
        scratch_shapes=[pltpu.VMEM((tm, MIX_W), BF16)],
        compiler_params=_params(1),
        name="out_proj",
    )(attn, four, cb, t, t, t, gm, conv_w, w_out, xall, modt)


def _ffn_kernel(x_ref, xp_ref, xn_ref, m_ref, g2_ref, wg_ref, wu_ref, cw_ref, cb_ref, wd_ref, fg_ref,
                o_ref, h_ref, act0_ref, act1_ref, acc_ref, *, n_lat_tiles, seq, ctx_len, final_norm,
                last_chunk_even):
    tm = x_ref.shape[0]
    i = pl.program_id(0)
    j = pl.program_id(1)
    n_chunks = pl.num_programs(1) - 1

    def norm_mod(x):
        return (_rms(x) * g2_ref[...] * (1.0 + m_ref[4:5, :]) + m_ref[3:4, :]).astype(BF16)

    def up_dots():
        g = _dot(h_ref[...], wg_ref[...])
        u = _dot(h_ref[SUBLANES:SUBLANES + tm, :], wu_ref[...])
        return g, u

    def gate(g, u, act_ref):
        is_start, is_end = _seq_edge_masks(i, tm, n_lat_tiles, seq, ctx_len)
        conv = _dwconv3(g[SUBLANES:SUBLANES + tm], g[SUBLANES - 1:SUBLANES], g[SUBLANES + tm:SUBLANES + tm + 1],
                        cw_ref, is_start, is_end)
        act_ref[...] = (_silu(conv + cb_ref[...]) * u).astype(BF16)

    def down(act_ref):
        acc_ref[...] += _dot(act_ref[...], wd_ref[...])

    @pl.when(j == 0)
    def _():
        h_ref[0:SUBLANES, :] = norm_mod(xp_ref[...])
        h_ref[SUBLANES:SUBLANES + tm, :] = norm_mod(x_ref[...])
        h_ref[SUBLANES + tm:, :] = norm_mod(xn_ref[...])
        acc_ref[...] = jnp.zeros_like(acc_ref)
        gate(*up_dots(), act0_ref)

    def middle(prev_ref, next_ref):
        g, u = up_dots()
        down(prev_ref)
        gate(g, u, next_ref)

    in_middle = jnp.logical_and(j > 0, j < n_chunks)
    pl.when(jnp.logical_and(in_middle, j % 2 == 1))(lambda: middle(act0_ref, act1_ref))
    pl.when(jnp.logical_and(in_middle, j % 2 == 0))(lambda: middle(act1_ref, act0_ref))

    @pl.when(j == n_chunks)
    def _():
        down(act0_ref if last_chunk_even else act1_ref)
        y = x_ref[...] + m_ref[5:6, :] * acc_ref[...]
        if final_norm:
            y = _rms(y) * fg_ref[...]
        o_ref[...] = y


def _ffn(xall, modt, layer, g2, w_up, conv_w, conv_b, w_down, final_g, *, lay, n_rows, final_norm):
    d = xall.shape[1]
    d_ff = w_down.shape[1]
    tm = lay["tm"]
    grp = lay["grp"]
    nj = d_ff // FF_CHUNK
    xprev, xnext = _halo_specs(tm, d, xall.shape[0])
    kern = functools.partial(_ffn_kernel, n_lat_tiles=lay["n_lat_tiles"], seq=lay["seq"],
                             ctx_len=lay["ctx_len"], final_norm=final_norm, last_chunk_even=(nj - 1) % 2 == 0)
    up = lambda i, j: jnp.minimum(j, nj - 1)
    dn = lambda i, j: jnp.maximum(j - 1, 0)
    return pl.pallas_call(
        kern,
        grid=(n_rows // tm, nj + 1),
        in_specs=[pl.BlockSpec((tm, d), lambda i, j: (i, 0)), xprev, xnext,
                  pl.BlockSpec((None, MOD_ROWS, d), lambda i, j: (layer, 0, grp(i))),
                  pl.BlockSpec((1, d), lambda i, j: (0, 0)),
                  pl.BlockSpec((None, d, FF_CHUNK), lambda i, j: (layer, 0, up(i, j))),
                  pl.BlockSpec((None, d, FF_CHUNK), lambda i, j: (layer, 0, nj + up(i, j))),
                  pl.BlockSpec((SUBLANES, FF_CHUNK), lambda i, j: (0, up(i, j))),
                  pl.BlockSpec((1, FF_CHUNK), lambda i, j: (0, up(i, j))),
                  pl.BlockSpec((None, FF_CHUNK, d), lambda i, j: (layer, dn(i, j), 0)),
                  pl.BlockSpec((1, d), lambda i, j: (0, 0))],
        out_specs=pl.BlockSpec((tm, d), lambda i, j: (i, 0)),
        out_shape=jax.ShapeDtypeStruct((n_rows, d), F32),
        scratch_shapes=[pltpu.VMEM((tm + 2 * SUBLANES, d), BF16), pltpu.VMEM((tm, FF_CHUNK), BF16),
                        pltpu.VMEM((tm, FF_CHUNK), BF16), pltpu.VMEM((tm, d), F32)],
        compiler_params=_params(2),
        name="ffn",
    )(xall, xall, xall, modt, g2, w_up, w_up, conv_w, conv_b, w_down, final_g)


def _rope_tables(seq, tm):
    pos = jnp.arange(seq, dtype=jnp.int32)
    row = (pos // GRID_W).astype(F32)
    col = (pos % GRID_W).astype(F32)
    freqs = ROPE_THETA ** (-jnp.arange(0, ROPE_AXIS_DIM, 2, dtype=F32) / ROPE_AXIS_DIM)
    ang_r = row[:, None] * freqs[None, :]
    ang_c = col[:, None] * freqs[None, :]
    cos = jnp.concatenate([jnp.cos(ang_r)] * 2 + [jnp.cos(ang_c)] * 2, axis=-1)
    sin = jnp.concatenate([-jnp.sin(ang_r), jnp.sin(ang_r), -jnp.sin(ang_c), jnp.sin(ang_c)], axis=-1)
    cos = jnp.concatenate([cos, jnp.ones((tm, HEAD_DIM), F32)], axis=0)
    sin = jnp.concatenate([sin, jnp.zeros((tm, HEAD_DIM), F32)], axis=0)
    return cos, sin


def _dft_matrix(n, scale):
    n1 = 1
    while n1 * n1 < n:
        n1 *= 2
    n2 = n // n1
    t = jnp.arange(n, dtype=jnp.int32)[None, :]
    a = jnp.arange(n1, dtype=jnp.int32)[:, None]
    b = jnp.arange(n2, dtype=jnp.int32)[:, None]
    ang_a = ((a * t) % n1).astype(F32) * (2.0 * math.pi / n1)
    ang_b = ((b * t) % n).astype(F32) * (2.0 * math.pi / n)
    ca, sa = jnp.cos(ang_a)[:, None, :], jnp.sin(ang_a)[:, None, :]
    cb, sb = jnp.cos(ang_b)[None, :, :] * scale, jnp.sin(ang_b)[None, :, :] * scale
    c = (ca * cb - sa * sb).reshape(n, n)
    s = (sa * cb + ca * sb).reshape(n, n)
    return c.astype(BF16), (-s).astype(BF16)


def _channel_dft(scale):
    k = jnp.arange(HEAD_DIM, dtype=jnp.int32)
    ang = ((k[:, None] * k[None, :]) % HEAD_DIM).astype(F32) * (2.0 * math.pi / HEAD_DIM)
    return (jnp.concatenate([jnp.cos(ang), jnp.sin(ang)], axis=1) * scale).astype(BF16)


def _layout(batch, seq, ctx_len):
    n_lat, n_ctx = batch * seq, batch * ctx_len
    tm = 512
    while n_ctx % tm or seq % tm:
        tm //= 2
    assert tm >= CHUNK and seq & (seq - 1) == 0 and ctx_len & (ctx_len - 1) == 0
    assert seq % GRID_W == 0 and ctx_len % CHUNK == 0 and n_lat % ctx_len == 0 and n_lat % seq == 0
    n_lat_tiles, tps = n_lat // tm, seq // tm
    grp = lambda i: jnp.where(i < n_lat_tiles, 1 + i // tps, 0)
    return dict(tm=tm, n_lat_tiles=n_lat_tiles, tiles_per_seq=tps, grp=grp, seq=seq, ctx_len=ctx_len,
                n_lat=n_lat, n_ctx=n_ctx)


def kernel(x, c, ctx, c_ctx, w_mod, b_mod, norm1_g, norm2_g, w_in, q_norm_g, k_norm_g, conv_w, gm_ln_g,
           gm_ln_b, gm_ws, gm_b, w_out, w_up, ffn_conv_w, ffn_conv_b, w_down, final_norm_g):
    batch, seq, d = x.shape
    ctx_len = ctx.shape[1]
    depth = w_mod.shape[0]
    d_ff = w_down.shape[1]
    assert batch + 1 <= MOD_ROWS and w_in.shape[2] == IN_W and w_down.shape[1] % FF_CHUNK == 0
    lay = _layout(batch, seq, ctx_len)
    n_lat, tm = lay["n_lat"], lay["tm"]

    xall = jnp.concatenate([x.reshape(n_lat, d), ctx.reshape(batch * ctx_len, d)], axis=0)

    cs = jnp.concatenate([c_ctx[None, :], c, jnp.zeros((MOD_ROWS - 1 - batch, d), F32)], axis=0)
    mod = _modulation(cs, w_mod, b_mod)
    modt = mod.reshape(depth, MOD_ROWS, N_MOD, d).transpose(0, 2, 1, 3).reshape(depth, N_MOD, MOD_ROWS * d)
    modt = jnp.pad(modt, ((0, 0), (0, MOD_ROWS - N_MOD), (0, 0)))

    cos_t, sin_t = _rope_tables(seq, tm)
    f_lat = _dft_matrix(seq, seq ** -0.5)
    f_ctx = _dft_matrix(ctx_len, ctx_len ** -0.5)
    dftc = _channel_dft(HEAD_DIM ** -0.5)

    pad_rows = lambda w: jnp.pad(w, ((0, 0), (0, SUBLANES - w.shape[1]), (0, 0)))
    conv_w8 = pad_rows(conv_w)
    ffn_conv_w8 = pad_rows(ffn_conv_w)
    bsf = jnp.broadcast_to(gm_b[..., None], gm_b.shape + (HEAD_DIM,))
    w_in_b, w_out_b, w_up_b, w_down_b = (w.astype(BF16) for w in (w_in, w_out, w_up, w_down))
    gm_ws_b = gm_ws.astype(BF16)
    fg = final_norm_g.reshape(1, d)

    for l in range(depth):
        last = l == depth - 1
        qt, k, vt, ab, cb, t, gm = _in_proj(
            xall, modt, l, norm1_g[l].reshape(1, d), w_in_b, q_norm_g[l].reshape(1, HEAD_DIM),
            k_norm_g[l].reshape(1, HEAD_DIM), cos_t, sin_t, dftc, gm_ln_g[l].reshape(1, GROUP_W),
            gm_ln_b[l].reshape(1, GROUP_W), gm_ws_b[l], bsf[l], lay=lay)
        four = _pos_dft(f_lat, ab, batch=batch, n=seq, first_row=0)
        if not last:
            four = _pos_dft(f_ctx, ab, batch=batch, n=ctx_len, first_row=n_lat, into=four)
        attn = _attention(qt, k, vt, q_norm_g[l], k_norm_g[l], batch=batch, seq=seq, ctx_len=ctx_len,
                          with_ctx_queries=not last)
        n_rows = n_lat if last else xall.shape[0]
        xall = _out_proj(attn, four, cb, t, gm, conv_w8[l], w_out_b, xall, modt, l, lay=lay, n_rows=n_rows)
        xall = _ffn(xall, modt, l, norm2_g[l].reshape(1, d), w_up_b, ffn_conv_w8[l],
                    ffn_conv_b[l].reshape(1, d_ff), w_down_b, fg, lay=lay, n_rows=n_rows, final_norm=last)
    return xall.reshape(batch, seq, d)
```

```python
import functools
import math

import jax
import jax.numpy as jnp
from jax import lax
from jax.experimental import pallas as pl
from jax.experimental.pallas import tpu as pltpu

F32 = jnp.float32
BF16 = jnp.bfloat16

GRID_W = 64
HEAD_DIM = 128
N_HEADS = 8
N_KV_HEADS = 2
Q_PER_KV = N_HEADS // N_KV_HEADS
ATTN_W = N_HEADS * HEAD_DIM
KV_W = N_KV_HEADS * HEAD_DIM
ROPE_THETA = 10000.0
ROPE_AXIS_DIM = HEAD_DIM // 2
ATTN_SCALE = HEAD_DIM ** -0.5
Q_SCALE = ATTN_SCALE * math.log2(math.e)
GROUP_W = 4 * HEAD_DIM
CHUNK = 128
N_MOD = 6
EPS = 1e-6

OFF_K = ATTN_W
OFF_V = OFF_K + KV_W
OFF_F = OFF_V + KV_W
OFF_CB = OFF_F + GROUP_W
OFF_CC = OFF_CB + GROUP_W
OFF_CH = OFF_CC + GROUP_W
OFF_GU = OFF_CH + GROUP_W
OFF_GV = OFF_GU + GROUP_W
IN_W = OFF_GV + GROUP_W
MIX_W = ATTN_W + 3 * GROUP_W

V7X_VMEM_LIMIT_BYTES = 56 * 1024 * 1024
V7X_VMEM_FFN_LIMIT_BYTES = 61 * 1024 * 1024
SUBLANES = 8
MOD_ROWS = 8
ROW_TILE = 512
FFN_ROW_TILE = 1024
FF_CHUNK = 512
KV_CHUNK = 1024
BF16_SUBLANES = 16
VT_ROWS = HEAD_DIM + BF16_SUBLANES
MAX_FIXED_SHIFT = 60.0
DFT_ROWS = 512


def _params(n_axes, vmem_limit_bytes=V7X_VMEM_LIMIT_BYTES):
    return pltpu.CompilerParams(dimension_semantics=("arbitrary",) * n_axes, vmem_limit_bytes=vmem_limit_bytes)


def _resident(shape):
    return pl.BlockSpec(shape, lambda *_: (0,) * len(shape), pipeline_mode=pl.Buffered(1))


def _resident_layer(stacked_shape, layer):
    rest = tuple(stacked_shape[1:])
    return pl.BlockSpec((None,) + rest, lambda *_: (layer,) + (0,) * len(rest), pipeline_mode=pl.Buffered(1))


def _dot(a, b):
    return jnp.dot(a, b, preferred_element_type=F32)


def _gelu_tanh(x):
    return 0.5 * x * (1.0 + jnp.tanh(math.sqrt(2.0 / math.pi) * (x + 0.044715 * (x * x * x))))


def _silu(x):
    return x * (1.0 / (1.0 + jnp.exp(-x)))


def _rms(x, eps=EPS):
    return x * lax.rsqrt(jnp.mean(x * x, axis=-1, keepdims=True) + eps)


def _split_bf16(x):
    hi = x.astype(BF16)
    return hi, (x - hi.astype(F32)).astype(BF16)


def _mod_kernel(c_ref, w_ref, b_ref, o_ref):
    s_hi, s_lo = _split_bf16(_silu(c_ref[...]))
    w_hi, w_lo = _split_bf16(w_ref[...])
    r = _dot(jnp.concatenate([s_hi, s_lo], axis=0), w_hi)
    o_ref[...] = r[:MOD_ROWS] + r[MOD_ROWS:] + _dot(s_hi, w_lo) + b_ref[...]


def _modulation(cs, w_mod, b_mod):
    depth, d, n = w_mod.shape
    tn = next(t for t in (1024, 512, 256, 128) if n % t == 0)
    return pl.pallas_call(
        _mod_kernel,
        grid=(depth, n // tn),
        in_specs=[pl.BlockSpec((MOD_ROWS, d), lambda l, j: (0, 0)),
                  pl.BlockSpec((None, d, tn), lambda l, j: (l, 0, j)),
                  pl.BlockSpec((None, 1, tn), lambda l, j: (l, 0, j))],
        out_specs=pl.BlockSpec((None, MOD_ROWS, tn), lambda l, j: (l, 0, j)),
        out_shape=jax.ShapeDtypeStruct((depth, MOD_ROWS, n), F32),
        compiler_params=_params(2),
        name="modulation",
    )(cs, w_mod, b_mod.reshape(depth, 1, n))


def _in_kernel(x_ref, m_ref, g1_ref, w_ref, qg_ref, kg_ref, cos_ref, sin_ref, dftc_ref,
               lng_ref, lnb_ref, ws_ref, bs_ref,
               qt_ref, k_ref, vt_ref, ab_ref, cb_ref, t_ref, gm_ref):
    tm = x_ref.shape[0]
    h = _rms(x_ref[...]) * g1_ref[...]
    h = h * (1.0 + m_ref[1:2, :]) + m_ref[0:1, :]
    hb = h.astype(BF16)

    def proj(lo, width=GROUP_W):
        return _dot(hb, w_ref[:, lo:lo + width])

    cos = cos_ref[...]
    sin = sin_ref[...]
    lane = lax.broadcasted_iota(jnp.int32, (tm, HEAD_DIM), 1)
    first_half = (lane % (ROPE_AXIS_DIM)) < (ROPE_AXIS_DIM // 2)

    def norm_rope(ph, gain):
        y = _rms(ph) * gain
        partner = jnp.where(first_half,
                            pltpu.roll(y, HEAD_DIM - ROPE_AXIS_DIM // 2, 1),
                            pltpu.roll(y, ROPE_AXIS_DIM // 2, 1))
        return y * cos + partner * sin

    qg = qg_ref[...] * Q_SCALE
    for half in range(ATTN_W // GROUP_W):
        p = proj(half * GROUP_W)
        for hh in range(GROUP_W // HEAD_DIM):
            c0 = half * GROUP_W + hh * HEAD_DIM
            y = norm_rope(p[:, hh * HEAD_DIM:(hh + 1) * HEAD_DIM], qg)
            qt_ref[c0:c0 + HEAD_DIM, :] = y.T.astype(BF16)

    p = proj(OFF_K)
    kg = kg_ref[...]
    for hh in range(N_KV_HEADS):
        cols = slice(hh * HEAD_DIM, (hh + 1) * HEAD_DIM)
        k_ref[:, cols] = norm_rope(p[:, cols], kg).astype(BF16)
        r0 = hh * VT_ROWS
        vt_ref[r0:r0 + HEAD_DIM, :] = p[:, KV_W + hh * HEAD_DIM:KV_W + (hh + 1) * HEAD_DIM].T.astype(BF16)
        vt_ref[r0 + HEAD_DIM:r0 + VT_ROWS, :] = jnp.ones((BF16_SUBLANES, tm), BF16)

    p = proj(OFF_F).astype(BF16)
    dftc = dftc_ref[...]
    for g in range(GROUP_W // HEAD_DIM):
        r = _dot(p[:, g * HEAD_DIM:(g + 1) * HEAD_DIM], dftc)
        ab_ref[:, g * HEAD_DIM:(g + 1) * HEAD_DIM] = r[:, :HEAD_DIM].astype(BF16)
        ab_ref[:, GROUP_W + g * HEAD_DIM:GROUP_W + (g + 1) * HEAD_DIM] = r[:, HEAD_DIM:].astype(BF16)

    cb_ref[...] = proj(OFF_CB).astype(BF16)
    t_ref[...] = (proj(OFF_CC) * proj(OFF_CH)).astype(BF16)

    u = _gelu_tanh(proj(OFF_GU))
    gv = _gelu_tanh(proj(OFF_GV))
    gc = gv - jnp.mean(gv, axis=-1, keepdims=True)
    vn = gc * lax.rsqrt(jnp.mean(gc * gc, axis=-1, keepdims=True) + EPS) * lng_ref[...] + lnb_ref[...]
    vn = vn.astype(BF16)
    for g in range(GROUP_W // HEAD_DIM):
        wsg = ws_ref[g]
        bsg = bs_ref[g]
        for c in range(tm // CHUNK):
            rows = slice(c * CHUNK, (c + 1) * CHUNK)
            cols = slice(g * HEAD_DIM, (g + 1) * HEAD_DIM)
            s = _dot(wsg, vn[rows, cols]) + bsg
            gm_ref[rows, cols] = (u[rows, cols] * s).astype(BF16)


def _in_proj(xall, modt, layer, g1, w_in, qg, kg, cos_t, sin_t, dftc, lng, lnb, ws, bsf, *, lay):
    r_rows, d = xall.shape
    tm = lay["tm"]
    grp, nlt, tps = lay["grp"], lay["n_lat_tiles"], lay["tiles_per_seq"]
    row = lambda w: pl.BlockSpec((tm, w), lambda i: (i, 0))
    vec = lambda w: pl.BlockSpec((1, w), lambda i: (0, 0))
    tab = pl.BlockSpec((tm, HEAD_DIM), lambda i: (jnp.where(i < nlt, i % tps, tps), 0))
    col = lambda h: pl.BlockSpec((h, tm), lambda i: (0, i))
    rows_bf16 = lambda w: jax.ShapeDtypeStruct((r_rows, w), BF16)
    cols_bf16 = lambda h: jax.ShapeDtypeStruct((h, r_rows), BF16)
    return pl.pallas_call(
        _in_kernel,
        grid=(r_rows // tm,),
        in_specs=[row(d),
                  pl.BlockSpec((None, MOD_ROWS, d), lambda i: (layer, 0, grp(i))),
                  vec(d), _resident_layer(w_in.shape, layer), vec(HEAD_DIM), vec(HEAD_DIM), tab, tab,
                  _resident(dftc.shape), vec(GROUP_W), vec(GROUP_W),
                  _resident(ws.shape), _resident(bsf.shape)],
        out_specs=[col(ATTN_W), row(KV_W), col(N_KV_HEADS * VT_ROWS), row(2 * GROUP_W), row(GROUP_W), row(GROUP_W),
                   row(GROUP_W)],
        out_shape=[cols_bf16(ATTN_W), rows_bf16(KV_W), cols_bf16(N_KV_HEADS * VT_ROWS), rows_bf16(2 * GROUP_W),
                   rows_bf16(GROUP_W), rows_bf16(GROUP_W), rows_bf16(GROUP_W)],
        compiler_params=_params(1),
        name="in_proj",
    )(xall, modt, g1, w_in, qg, kg, cos_t, sin_t, dftc, lng, lnb, ws, bsf)


def _dft_kernel(fc_ref, fs_ref, ab_ref, cc_ref, cs_ref, abc_ref, o_ref, *, n_lat_steps):
    def dft(c, s, a, b):
        return (_dot(c, a) + _dot(s, b)).astype(BF16)

    @pl.when(pl.program_id(0) < n_lat_steps)
    def _():
        o_ref[...] = dft(fc_ref[...], fs_ref[...], ab_ref[:, :GROUP_W], ab_ref[:, GROUP_W:])

    @pl.when(pl.program_id(0) >= n_lat_steps)
    def _():
        n_ctx = cc_ref.shape[0]
        for r0 in range(0, o_ref.shape[0], n_ctx):
            rows = slice(r0, r0 + n_ctx)
            o_ref[rows, :] = dft(cc_ref[...], cs_ref[...], abc_ref[rows, :GROUP_W], abc_ref[rows, GROUP_W:])


def _pos_dft(f_lat, f_ctx, ab, *, batch, seq, ctx_len, with_ctx):
    tmd = min(DFT_ROWS, seq)
    assert tmd % ctx_len == 0 and (batch * ctx_len) % tmd == 0
    per_seq = seq // tmd
    n_lat_steps = batch * per_seq
    n_ctx_steps = batch * ctx_len // tmd if with_ctx else 0
    lat = lambda s: s < n_lat_steps
    fspec = pl.BlockSpec((tmd, seq), lambda s: (jnp.where(lat(s), s % per_seq, 0), 0))
    return pl.pallas_call(
        functools.partial(_dft_kernel, n_lat_steps=n_lat_steps),
        grid=(n_lat_steps + n_ctx_steps,),
        in_specs=[fspec, fspec,
                  pl.BlockSpec((seq, 2 * GROUP_W), lambda s: (jnp.where(lat(s), s // per_seq, 0), 0)),
                  _resident(f_ctx[0].shape), _resident(f_ctx[1].shape),
                  pl.BlockSpec((tmd, 2 * GROUP_W), lambda s: (jnp.where(lat(s), n_lat_steps, s), 0))],
        out_specs=pl.BlockSpec((tmd, GROUP_W), lambda s: (s, 0)),
        out_shape=jax.ShapeDtypeStruct(((n_lat_steps + n_ctx_steps) * tmd, GROUP_W), BF16),
        compiler_params=_params(1),
        name="pos_dft",
    )(*f_lat, ab, *f_ctx, ab)


def _attn_kernel(shift_ref, qt_ref, kl_ref, kc_ref, vtl_ref, vtc_ref, o_ref, *, nq, with_ctx_queries, fixed_shift):
    tq = qt_ref.shape[1]
    seq = kl_ref.shape[0]
    cols = Q_PER_KV * tq
    qt = jnp.concatenate([qt_ref[g * HEAD_DIM:(g + 1) * HEAD_DIM, :] for g in range(Q_PER_KV)], axis=1)

    def finish(l, acc):
        o = acc * (1.0 / l)
        for g in range(Q_PER_KV):
            o_ref[:, g * HEAD_DIM:(g + 1) * HEAD_DIM] = o[:, g * tq:(g + 1) * tq].T.astype(BF16)

    def key_chunks(with_latent_keys):
        chunks = [(kc_ref, vtc_ref, slice(None))]
        if with_latent_keys:
            chunks += [(kl_ref, vtl_ref, slice(j * KV_CHUNK, (j + 1) * KV_CHUNK)) for j in range(seq // KV_CHUNK)]
        return chunks

    def run_fixed(with_latent_keys):
        shift = shift_ref[0, 0]
        acc = None
        for k_ref, vt_ref, ks in key_chunks(with_latent_keys):
            p = jnp.exp2(_dot(k_ref[ks, :], qt) - shift).astype(BF16)
            part = _dot(vt_ref[:, ks], p)
            acc = part if acc is None else acc + part
        finish(acc[HEAD_DIM:HEAD_DIM + 1], acc[:HEAD_DIM])

    def run_online(with_latent_keys):
        m = jnp.full((1, cols), -jnp.inf, F32)
        l = jnp.zeros((1, cols), F32)
        acc = jnp.zeros((HEAD_DIM, cols), F32)
        for k_ref, vt_ref, ks in key_chunks(with_latent_keys):
            s = _dot(k_ref[ks, :], qt)
            m_new = jnp.maximum(m, jnp.max(s, axis=0, keepdims=True))
            alpha = jnp.exp2(m - m_new)
            p = jnp.exp2(s - m_new)
            l = alpha * l + jnp.sum(p, axis=0, keepdims=True)
            acc = alpha * acc + _dot(vt_ref[:HEAD_DIM, ks], p.astype(BF16))
            m = m_new
        finish(l, acc)

    run = run_fixed if fixed_shift else run_online
    if not with_ctx_queries:
        run(True)
    else:
        is_latent = pl.program_id(2) < nq
        pl.when(is_latent)(lambda: run(True))
        pl.when(jnp.logical_not(is_latent))(lambda: run(False))


def _attention(qt, k, vt, q_gain, k_gain, *, batch, seq, ctx_len, with_ctx_queries):
    r_rows = k.shape[0] if with_ctx_queries else batch * seq
    tq = ctx_len
    nq = seq // tq
    n_lat_q = batch * nq
    ctx_blk0 = batch * seq // ctx_len
    qw = Q_PER_KV * HEAD_DIM
    q_blk = lambda b, i: jnp.where(i < nq, b * nq + i, n_lat_q + b)

    def call(fixed_shift, shift):
        return pl.pallas_call(
            functools.partial(_attn_kernel, nq=nq, with_ctx_queries=with_ctx_queries, fixed_shift=fixed_shift),
            grid=(batch, N_KV_HEADS, nq + (1 if with_ctx_queries else 0)),
            in_specs=[pl.BlockSpec(memory_space=pltpu.SMEM),
                      pl.BlockSpec((qw, tq), lambda b, hg, i: (hg, q_blk(b, i))),
                      pl.BlockSpec((seq, HEAD_DIM), lambda b, hg, i: (b, hg)),
                      pl.BlockSpec((ctx_len, HEAD_DIM), lambda b, hg, i: (ctx_blk0 + b, hg)),
                      pl.BlockSpec((VT_ROWS, seq), lambda b, hg, i: (hg, b)),
                      pl.BlockSpec((VT_ROWS, ctx_len), lambda b, hg, i: (hg, ctx_blk0 + b))],
            out_specs=pl.BlockSpec((tq, qw), lambda b, hg, i: (q_blk(b, i), hg)),
            out_shape=jax.ShapeDtypeStruct((r_rows, ATTN_W), BF16),
            compiler_params=_params(3),
            name="attention_fixed" if fixed_shift else "attention_online",
        )(shift, qt, k, k, vt, vt)

    bound = (HEAD_DIM * Q_SCALE * (1.0 + 2.0 ** -6)) * jnp.max(jnp.abs(q_gain)) * jnp.max(jnp.abs(k_gain))
    shift = bound.astype(F32).reshape(1, 1)
    return lax.cond(bound <= MAX_FIXED_SHIFT, lambda: call(True, shift), lambda: call(False, shift))


def _seq_edge_masks(tile, tm, n_lat_tiles, seq, ctx_len):
    r = lax.broadcasted_iota(jnp.int32, (tm, 1), 0)
    period = jnp.where(tile < n_lat_tiles, seq, ctx_len)
    pos = (tile * tm + r) & (period - 1)
    return pos == 0, pos == period - 1


def _dwconv3(center, prev_row, next_row, w_ref, is_start, is_end):
    tm = center.shape[0]
    r = lax.broadcasted_iota(jnp.int32, (tm, 1), 0)
    up = jnp.where(r == 0, prev_row, pltpu.roll(center, 1, 0))
    dn = jnp.where(r == tm - 1, next_row, pltpu.roll(center, tm - 1, 0))
    up = jnp.where(is_start, 0.0, up)
    dn = jnp.where(is_end, 0.0, dn)
    return up * w_ref[0:1, :] + center * w_ref[1:2, :] + dn * w_ref[2:3, :]


def _out_kernel(attn_ref, four_ref, cb_ref, t_ref, tp_ref, tn_ref, gm_ref, cw_ref, w_ref, x_ref, m_ref,
                o_ref, mix_ref, *, n_lat_tiles, seq, ctx_len):
    tm = x_ref.shape[0]
    i = pl.program_id(0)
    is_start, is_end = _seq_edge_masks(i, tm, n_lat_tiles, seq, ctx_len)
    conv = _dwconv3(t_ref[...].astype(F32), tp_ref[SUBLANES - 1:SUBLANES, :].astype(F32),
                    tn_ref[0:1, :].astype(F32), cw_ref, is_start, is_end)
    mix_ref[:, :ATTN_W] = attn_ref[...]
    mix_ref[:, ATTN_W:ATTN_W + GROUP_W] = four_ref[...]
    mix_ref[:, ATTN_W + GROUP_W:ATTN_W + 2 * GROUP_W] = (cb_ref[...].astype(F32) * conv).astype(BF16)
    mix_ref[:, ATTN_W + 2 * GROUP_W:] = gm_ref[...]
    o_ref[...] = x_ref[...] + m_ref[2:3, :] * _dot(mix_ref[...], w_ref[...])


def _halo_specs(tm, width, n_rows):
    per = tm // SUBLANES
    last = n_rows // SUBLANES - 1
    prev = pl.BlockSpec((SUBLANES, width), lambda i, *_: (jnp.maximum(i * per - 1, 0), 0))
    nxt = pl.BlockSpec((SUBLANES, width), lambda i, *_: (jnp.minimum((i + 1) * per, last), 0))
    return prev, nxt


def _out_proj(attn, four, cb, t, gm, conv_w, w_out, xall, modt, layer, *, lay, n_rows):
    d = xall.shape[1]
    tm = lay["tm"]
    grp = lay["grp"]
    row = lambda w: pl.BlockSpec((tm, w), lambda i: (i, 0))
    tprev, tnext = _halo_specs(tm, GROUP_W, t.shape[0])
    kern = functools.partial(_out_kernel, n_lat_tiles=lay["n_lat_tiles"], seq=lay["seq"], ctx_len=lay["ctx_len"])
    return pl.pallas_call(
        kern,
        grid=(n_rows // tm,),
        in_specs=[row(ATTN_W), row(GROUP_W), row(GROUP_W), row(GROUP_W), tprev, tnext, row(GROUP_W),
                  _resident(conv_w.shape), _resident_layer(w_out.shape, layer), row(d),
                  pl.BlockSpec((None, MOD_ROWS, d), lambda i: (layer, 0, grp(i)))],
        out_specs=row(d),
        out_shape=jax.ShapeDtypeStruct((n_rows, d), F32),
        scratch_shapes=[pltpu.VMEM((tm, MIX_W), BF16)],
        compiler_params=_params(1),
        name="out_proj",
    )(attn, four, cb, t, t, t, gm, conv_w, w_out, xall, modt)


def _ffn_kernel(x_ref, xp_ref, xn_ref, m_ref, g2_ref, wg_ref, wu_ref, cw_ref, cb_ref, wd_ref, fg_ref,
                o_ref, h_ref, *, n_lat_tiles, seq, ctx_len, final_norm):
    tm = x_ref.shape[0]
    i = pl.program_id(0)
    j = pl.program_id(1)

    def norm_mod(x):
        return (_rms(x) * g2_ref[...] * (1.0 + m_ref[4:5, :]) + m_ref[3:4, :]).astype(BF16)

    @pl.when(j == 0)
    def _():
        h_ref[0:SUBLANES, :] = norm_mod(xp_ref[...])
        h_ref[SUBLANES:SUBLANES + tm, :] = norm_mod(x_ref[...])
        h_ref[SUBLANES + tm:, :] = norm_mod(xn_ref[...])
        o_ref[...] = jnp.zeros_like(o_ref)

    is_start, is_end = _seq_edge_masks(i, tm, n_lat_tiles, seq, ctx_len)
    g = _dot(h_ref[...], wg_ref[...])
    u = _dot(h_ref[SUBLANES:SUBLANES + tm, :], wu_ref[...])
    conv = _dwconv3(g[SUBLANES:SUBLANES + tm], g[SUBLANES - 1:SUBLANES], g[SUBLANES + tm:SUBLANES + tm + 1],
                    cw_ref, is_start, is_end)
    act = (_silu(conv + cb_ref[...]) * u).astype(BF16)
    o_ref[...] += _dot(act, wd_ref[...])

    @pl.when(j == pl.num_programs(1) - 1)
    def _():
        y = x_ref[...] + m_ref[5:6, :] * o_ref[...]
        if final_norm:
            y = _rms(y) * fg_ref[...]
        o_ref[...] = y


def _ffn(xall, modt, layer, g2, w_up, conv_w, conv_b, w_down, final_g, *, lay, n_rows, final_norm):
    d = xall.shape[1]
    d_ff = w_down.shape[1]
    tm = lay["tm"]
    grp = lay["grp"]
    nj = d_ff // FF_CHUNK
    xprev, xnext = _halo_specs(tm, d, xall.shape[0])
    kern = functools.partial(_ffn_kernel, n_lat_tiles=lay["n_lat_tiles"], seq=lay["seq"],
                             ctx_len=lay["ctx_len"], final_norm=final_norm)
    return pl.pallas_call(
        kern,
        grid=(n_rows // tm, nj),
        in_specs=[pl.BlockSpec((tm, d), lambda i, j: (i, 0)), xprev, xnext,
                  pl.BlockSpec((None, MOD_ROWS, d), lambda i, j: (layer, 0, grp(i))),
                  pl.BlockSpec((1, d), lambda i, j: (0, 0)),
                  pl.BlockSpec((None, d, FF_CHUNK), lambda i, j: (layer, 0, j)),
                  pl.BlockSpec((None, d, FF_CHUNK), lambda i, j: (layer, 0, nj + j)),
                  pl.BlockSpec((SUBLANES, FF_CHUNK), lambda i, j: (0, j)),
                  pl.BlockSpec((1, FF_CHUNK), lambda i, j: (0, j)),
                  pl.BlockSpec((None, FF_CHUNK, d), lambda i, j: (layer, j, 0)),
                  pl.BlockSpec((1, d), lambda i, j: (0, 0))],
        out_specs=pl.BlockSpec((tm, d), lambda i, j: (i, 0)),
        out_shape=jax.ShapeDtypeStruct((n_rows, d), F32),
        scratch_shapes=[pltpu.VMEM((tm + 2 * SUBLANES, d), BF16)],
        compiler_params=_params(2, V7X_VMEM_FFN_LIMIT_BYTES),
        name="ffn",
    )(xall, xall, xall, modt, g2, w_up, w_up, conv_w, conv_b, w_down, final_g)


def _rope_tables(seq, tm):
    pos = jnp.arange(seq, dtype=jnp.int32)
    row = (pos // GRID_W).astype(F32)
    col = (pos % GRID_W).astype(F32)
    freqs = ROPE_THETA ** (-jnp.arange(0, ROPE_AXIS_DIM, 2, dtype=F32) / ROPE_AXIS_DIM)
    ang_r = row[:, None] * freqs[None, :]
    ang_c = col[:, None] * freqs[None, :]
    cos = jnp.concatenate([jnp.cos(ang_r)] * 2 + [jnp.cos(ang_c)] * 2, axis=-1)
    sin = jnp.concatenate([-jnp.sin(ang_r), jnp.sin(ang_r), -jnp.sin(ang_c), jnp.sin(ang_c)], axis=-1)
    cos = jnp.concatenate([cos, jnp.ones((tm, HEAD_DIM), F32)], axis=0)
    sin = jnp.concatenate([sin, jnp.zeros((tm, HEAD_DIM), F32)], axis=0)
    return cos, sin


def _dft_matrix(n, scale):
    n1 = 1
    while n1 * n1 < n:
        n1 *= 2
    n2 = n // n1
    t = jnp.arange(n, dtype=jnp.int32)[None, :]
    a = jnp.arange(n1, dtype=jnp.int32)[:, None]
    b = jnp.arange(n2, dtype=jnp.int32)[:, None]
    ang_a = ((a * t) % n1).astype(F32) * (2.0 * math.pi / n1)
    ang_b = ((b * t) % n).astype(F32) * (2.0 * math.pi / n)
    ca, sa = jnp.cos(ang_a)[:, None, :], jnp.sin(ang_a)[:, None, :]
    cb, sb = jnp.cos(ang_b)[None, :, :] * scale, jnp.sin(ang_b)[None, :, :] * scale
    c = (ca * cb - sa * sb).reshape(n, n)
    s = (sa * cb + ca * sb).reshape(n, n)
    return c.astype(BF16), (-s).astype(BF16)


def _channel_dft(scale):
    k = jnp.arange(HEAD_DIM, dtype=jnp.int32)
    ang = ((k[:, None] * k[None, :]) % HEAD_DIM).astype(F32) * (2.0 * math.pi / HEAD_DIM)
    return (jnp.concatenate([jnp.cos(ang), jnp.sin(ang)], axis=1) * scale).astype(BF16)


def _layout(batch, seq, ctx_len, max_tm):
    n_lat, n_ctx = batch * seq, batch * ctx_len
    tm = max_tm
    while n_ctx % tm or seq % tm:
        tm //= 2
    assert tm >= CHUNK and seq & (seq - 1) == 0 and ctx_len & (ctx_len - 1) == 0
    assert seq % GRID_W == 0 and ctx_len % CHUNK == 0 and n_lat % ctx_len == 0 and n_lat % seq == 0
    n_lat_tiles, tps = n_lat // tm, seq // tm
    grp = lambda i: jnp.where(i < n_lat_tiles, 1 + i // tps, 0)
    return dict(tm=tm, n_lat_tiles=n_lat_tiles, tiles_per_seq=tps, grp=grp, seq=seq, ctx_len=ctx_len,
                n_lat=n_lat, n_ctx=n_ctx)


def kernel(x, c, ctx, c_ctx, w_mod, b_mod, norm1_g, norm2_g, w_in, q_norm_g, k_norm_g, conv_w, gm_ln_g,
           gm_ln_b, gm_ws, gm_b, w_out, w_up, ffn_conv_w, ffn_conv_b, w_down, final_norm_g):
    batch, seq, d = x.shape
    ctx_len = ctx.shape[1]
    depth = w_mod.shape[0]
    d_ff = w_down.shape[1]
    assert batch + 1 <= MOD_ROWS and w_in.shape[2] == IN_W and w_down.shape[1] % FF_CHUNK == 0
    lay = _layout(batch, seq, ctx_len, ROW_TILE)
    lay_ffn = _layout(batch, seq, ctx_len, FFN_ROW_TILE)
    n_lat, tm = lay["n_lat"], lay["tm"]

    xall = jnp.concatenate([x.reshape(n_lat, d), ctx.reshape(batch * ctx_len, d)], axis=0)

    cs = jnp.concatenate([c_ctx[None, :], c, jnp.zeros((MOD_ROWS - 1 - batch, d), F32)], axis=0)
    mod = _modulation(cs, w_mod, b_mod)
    modt = mod.reshape(depth, MOD_ROWS, N_MOD, d).transpose(0, 2, 1, 3).reshape(depth, N_MOD, MOD_ROWS * d)
    modt = jnp.pad(modt, ((0, 0), (0, MOD_ROWS - N_MOD), (0, 0)))

    cos_t, sin_t = _rope_tables(seq, tm)
    f_lat = _dft_matrix(seq, seq ** -0.5)
    f_ctx = _dft_matrix(ctx_len, ctx_len ** -0.5)
    dftc = _channel_dft(HEAD_DIM ** -0.5)

    pad_rows = lambda w: jnp.pad(w, ((0, 0), (0, SUBLANES - w.shape[1]), (0, 0)))
    conv_w8 = pad_rows(conv_w)
    ffn_conv_w8 = pad_rows(ffn_conv_w)
    bsf = jnp.broadcast_to(gm_b[..., None], gm_b.shape + (HEAD_DIM,))
    w_in_b, w_out_b, w_up_b, w_down_b = (w.astype(BF16) for w in (w_in, w_out, w_up, w_down))
    gm_ws_b = gm_ws.astype(BF16)
    fg = final_norm_g.reshape(1, d)

    for l in range(depth):
        last = l == depth - 1
        qt, k, vt, ab, cb, t, gm = _in_proj(
            xall, modt, l, norm1_g[l].reshape(1, d), w_in_b, q_norm_g[l].reshape(1, HEAD_DIM),
            k_norm_g[l].reshape(1, HEAD_DIM), cos_t, sin_t, dftc, gm_ln_g[l].reshape(1, GROUP_W),
            gm_ln_b[l].reshape(1, GROUP_W), gm_ws_b[l], bsf[l], lay=lay)
        four = _pos_dft(f_lat, f_ctx, ab, batch=batch, seq=seq, ctx_len=ctx_len, with_ctx=not last)
        attn = _attention(qt, k, vt, q_norm_g[l], k_norm_g[l], batch=batch, seq=seq, ctx_len=ctx_len,
                          with_ctx_queries=not last)
        n_rows = n_lat if last else xall.shape[0]
        xall = _out_proj(attn, four, cb, t, gm, conv_w8[l], w_out_b, xall, modt, l, lay=lay, n_rows=n_rows)
        xall = _ffn(xall, modt, l, norm2_g[l].reshape(1, d), w_up_b, ffn_conv_w8[l],
                    ffn_conv_b[l].reshape(1, d_ff), w_down_b, fg, lay=lay_ffn, n_rows=n_rows, final_norm=last)
    return xall.reshape(batch, seq, d)
```

```python
import functools
import math

import jax
import jax.numpy as jnp
from jax import lax
from jax.experimental import pallas as pl
from jax.experimental.pallas import tpu as pltpu

F32 = jnp.float32
BF16 = jnp.bfloat16

GRID_W = 64
HEAD_DIM = 128
N_HEADS = 8
N_KV_HEADS = 2
Q_PER_KV = N_HEADS // N_KV_HEADS
ATTN_W = N_HEADS * HEAD_DIM
KV_W = N_KV_HEADS * HEAD_DIM
ROPE_THETA = 10000.0
ROPE_AXIS_DIM = HEAD_DIM // 2
ATTN_SCALE = HEAD_DIM ** -0.5
Q_SCALE = ATTN_SCALE * math.log2(math.e)
GROUP_W = 4 * HEAD_DIM
CHUNK = 128
N_MOD = 6
EPS = 1e-6

OFF_K = ATTN_W
OFF_V = OFF_K + KV_W
OFF_F = OFF_V + KV_W
OFF_CB = OFF_F + GROUP_W
OFF_CC = OFF_CB + GROUP_W
OFF_CH = OFF_CC + GROUP_W
OFF_GU = OFF_CH + GROUP_W
OFF_GV = OFF_GU + GROUP_W
IN_W = OFF_GV + GROUP_W
MIX_W = ATTN_W + 3 * GROUP_W

V7X_VMEM_LIMIT_BYTES = 56 * 1024 * 1024
V7X_VMEM_FFN_LIMIT_BYTES = 61 * 1024 * 1024
SUBLANES = 8
MOD_ROWS = 8
ROW_TILE = 512
FFN_ROW_TILE = 1024
FF_CHUNK = 512
KV_CHUNK = 1024
CAST_STRIP = 128
BF16_SUBLANES = 16
NORM_ROWS = BF16_SUBLANES
VT_ROWS = HEAD_DIM + BF16_SUBLANES
MAX_FIXED_SHIFT = 60.0
DFT_ROWS = 512


def _params(n_axes, vmem_limit_bytes=V7X_VMEM_LIMIT_BYTES):
    return pltpu.CompilerParams(dimension_semantics=("arbitrary",) * n_axes, vmem_limit_bytes=vmem_limit_bytes)


def _resident(shape):
    return pl.BlockSpec(shape, lambda *_: (0,) * len(shape), pipeline_mode=pl.Buffered(1))


def _resident_layer(stacked_shape):
    rest = tuple(stacked_shape[1:])
    assert stacked_shape[0] == 1
    return pl.BlockSpec((None,) + rest, lambda *_: (0,) * (1 + len(rest)), pipeline_mode=pl.Buffered(1))


def _stream_specs(tm, d, n_lat_tiles, ctx_block0):
    lat = pl.BlockSpec((tm, d), lambda i, *_: (jnp.minimum(i, n_lat_tiles - 1), 0))
    ctx = pl.BlockSpec((tm, d), lambda i, *_: (ctx_block0 + jnp.maximum(i - n_lat_tiles, 0), 0))
    return lat, ctx


def _dot(a, b):
    return jnp.dot(a, b, preferred_element_type=F32)


def _gelu_tanh(x):
    return 0.5 * x * (1.0 + jnp.tanh(math.sqrt(2.0 / math.pi) * (x + 0.044715 * (x * x * x))))


def _silu(x):
    return x * (1.0 / (1.0 + jnp.exp(-x)))


def _rms(x, eps=EPS):
    return x * lax.rsqrt(jnp.mean(x * x, axis=-1, keepdims=True) + eps)


def _split_bf16(x):
    hi = x.astype(BF16)
    return hi, (x - hi.astype(F32)).astype(BF16)


def _mod_kernel(c_ref, w_ref, b_ref, o_ref):
    s_hi, s_lo = _split_bf16(_silu(c_ref[...]))
    w_hi, w_lo = _split_bf16(w_ref[...])
    r = _dot(jnp.concatenate([s_hi, s_lo], axis=0), w_hi)
    o_ref[...] = r[:MOD_ROWS] + r[MOD_ROWS:] + _dot(s_hi, w_lo) + b_ref[...]


def _modulation(cs, w_mod, b_mod):
    depth, d, n = w_mod.shape
    tn = next(t for t in (1024, 512, 256, 128) if n % t == 0)
    return pl.pallas_call(
        _mod_kernel,
        grid=(depth, n // tn),
        in_specs=[pl.BlockSpec((MOD_ROWS, d), lambda l, j: (0, 0)),
                  pl.BlockSpec((None, d, tn), lambda l, j: (l, 0, j)),
                  pl.BlockSpec((None, 1, tn), lambda l, j: (l, 0, j))],
        out_specs=pl.BlockSpec((None, MOD_ROWS, tn), lambda l, j: (l, 0, j)),
        out_shape=jax.ShapeDtypeStruct((depth, MOD_ROWS, n), F32),
        compiler_params=_params(2),
        name="modulation",
    )(cs, w_mod, b_mod.reshape(depth, 1, n))


def _in_kernel(xl_ref, xc_ref, m_ref, g1_ref, w_ref, qg_ref, kg_ref, cos_ref, sin_ref, dftc_ref,
               lng_ref, lnb_ref, ws_ref, bs_ref,
               qt_ref, k_ref, vt_ref, ab_ref, cb_ref, t_ref, gm_ref, *, n_lat_tiles):
    tm = xl_ref.shape[0]
    x = jnp.where(pl.program_id(0) < n_lat_tiles, xl_ref[...], xc_ref[...])
    h = _rms(x) * g1_ref[...]
    h = h * (1.0 + m_ref[1:2, :]) + m_ref[0:1, :]
    hb = h.astype(BF16)

    def proj(lo, width=GROUP_W):
        return _dot(hb, w_ref[:, lo:lo + width])

    cos = cos_ref[...]
    sin = sin_ref[...]
    lane = lax.broadcasted_iota(jnp.int32, (tm, HEAD_DIM), 1)
    first_half = (lane % (ROPE_AXIS_DIM)) < (ROPE_AXIS_DIM // 2)

    def norm_rope(ph, gain):
        y = _rms(ph) * gain
        partner = jnp.where(first_half,
                            pltpu.roll(y, HEAD_DIM - ROPE_AXIS_DIM // 2, 1),
                            pltpu.roll(y, ROPE_AXIS_DIM // 2, 1))
        return y * cos + partner * sin

    qg = qg_ref[...] * Q_SCALE
    for half in range(ATTN_W // GROUP_W):
        p = proj(half * GROUP_W)
        for hh in range(GROUP_W // HEAD_DIM):
            c0 = half * GROUP_W + hh * HEAD_DIM
            y = norm_rope(p[:, hh * HEAD_DIM:(hh + 1) * HEAD_DIM], qg)
            qt_ref[c0:c0 + HEAD_DIM, :] = y.T.astype(BF16)

    p = proj(OFF_K)
    kg = kg_ref[...]
    for hh in range(N_KV_HEADS):
        cols = slice(hh * HEAD_DIM, (hh + 1) * HEAD_DIM)
        k_ref[:, cols] = norm_rope(p[:, cols], kg).astype(BF16)
        r0 = hh * VT_ROWS
        vt_ref[r0:r0 + HEAD_DIM, :] = p[:, KV_W + hh * HEAD_DIM:KV_W + (hh + 1) * HEAD_DIM].T.astype(BF16)
        vt_ref[r0 + HEAD_DIM:r0 + VT_ROWS, :] = jnp.ones((BF16_SUBLANES, tm), BF16)

    p = proj(OFF_F).astype(BF16)
    dftc = dftc_ref[...]
    for g in range(GROUP_W // HEAD_DIM):
        r = _dot(p[:, g * HEAD_DIM:(g + 1) * HEAD_DIM], dftc)
        ab_ref[:, g * HEAD_DIM:(g + 1) * HEAD_DIM] = r[:, :HEAD_DIM].astype(BF16)
        ab_ref[:, GROUP_W + g * HEAD_DIM:GROUP_W + (g + 1) * HEAD_DIM] = r[:, HEAD_DIM:].astype(BF16)

    cb_ref[...] = proj(OFF_CB).astype(BF16)
    t_ref[...] = (proj(OFF_CC) * proj(OFF_CH)).astype(BF16)

    u = _gelu_tanh(proj(OFF_GU))
    gv = _gelu_tanh(proj(OFF_GV))
    gc = gv - jnp.mean(gv, axis=-1, keepdims=True)
    vn = gc * lax.rsqrt(jnp.mean(gc * gc, axis=-1, keepdims=True) + EPS) * lng_ref[...] + lnb_ref[...]
    vn = vn.astype(BF16)
    for g in range(GROUP_W // HEAD_DIM):
        wsg = ws_ref[g]
        bsg = bs_ref[g]
        for c in range(tm // CHUNK):
            rows = slice(c * CHUNK, (c + 1) * CHUNK)
            cols = slice(g * HEAD_DIM, (g + 1) * HEAD_DIM)
            s = _dot(wsg, vn[rows, cols]) + bsg
            gm_ref[rows, cols] = (u[rows, cols] * s).astype(BF16)


def _in_proj(rows_src, modt, layer, g1, w_in, qg, kg, cos_t, sin_t, dftc, lng, lnb, ws, bsf, *, lay):
    x_lat, x_ctx, ctx_block0 = rows_src
    r_rows, d = lay["n_lat"] + lay["n_ctx"], x_lat.shape[1]
    tm = lay["tm"]
    grp, nlt, tps = lay["grp"], lay["n_lat_tiles"], lay["tiles_per_seq"]
    row = lambda w: pl.BlockSpec((tm, w), lambda i: (i, 0))
    vec = lambda w: pl.BlockSpec((1, w), lambda i: (0, 0))
    tab = pl.BlockSpec((tm, HEAD_DIM), lambda i: (jnp.where(i < nlt, i % tps, tps), 0))
    col = lambda h: pl.BlockSpec((h, tm), lambda i: (0, i))
    rows_bf16 = lambda w: jax.ShapeDtypeStruct((r_rows, w), BF16)
    cols_bf16 = lambda h: jax.ShapeDtypeStruct((h, r_rows), BF16)
    return pl.pallas_call(
        functools.partial(_in_kernel, n_lat_tiles=nlt),
        grid=(r_rows // tm,),
        in_specs=[*_stream_specs(tm, d, nlt, ctx_block0),
                  pl.BlockSpec((None, MOD_ROWS, d), lambda i: (layer, 0, grp(i))),
                  vec(d), _resident_layer(w_in.shape), vec(HEAD_DIM), vec(HEAD_DIM), tab, tab,
                  _resident(dftc.shape), vec(GROUP_W), vec(GROUP_W),
                  _resident(ws.shape), _resident(bsf.shape)],
        out_specs=[col(ATTN_W), row(KV_W), col(N_KV_HEADS * VT_ROWS), row(2 * GROUP_W), row(GROUP_W), row(GROUP_W),
                   row(GROUP_W)],
        out_shape=[cols_bf16(ATTN_W), rows_bf16(KV_W), cols_bf16(N_KV_HEADS * VT_ROWS), rows_bf16(2 * GROUP_W),
                   rows_bf16(GROUP_W), rows_bf16(GROUP_W), rows_bf16(GROUP_W)],
        compiler_params=_params(1),
        name="in_proj",
    )(x_lat, x_ctx, modt, g1, w_in, qg, kg, cos_t, sin_t, dftc, lng, lnb, ws, bsf)


def _dft_kernel(fc_ref, fs_ref, ab_ref, cc_ref, cs_ref, abc_ref, o_ref, *, n_lat_steps):
    def dft(c, s, a, b):
        return (_dot(c, a) + _dot(s, b)).astype(BF16)

    @pl.when(pl.program_id(0) < n_lat_steps)
    def _():
        o_ref[...] = dft(fc_ref[...], fs_ref[...], ab_ref[:, :GROUP_W], ab_ref[:, GROUP_W:])

    @pl.when(pl.program_id(0) >= n_lat_steps)
    def _():
        n_ctx = cc_ref.shape[0]
        for r0 in range(0, o_ref.shape[0], n_ctx):
            rows = slice(r0, r0 + n_ctx)
            o_ref[rows, :] = dft(cc_ref[...], cs_ref[...], abc_ref[rows, :GROUP_W], abc_ref[rows, GROUP_W:])


def _pos_dft(f_lat, f_ctx, ab, *, batch, seq, ctx_len, with_ctx):
    tmd = min(DFT_ROWS, seq)
    assert tmd % ctx_len == 0 and (batch * ctx_len) % tmd == 0
    per_seq = seq // tmd
    n_lat_steps = batch * per_seq
    n_ctx_steps = batch * ctx_len // tmd if with_ctx else 0
    lat = lambda s: s < n_lat_steps
    fspec = pl.BlockSpec((tmd, seq), lambda s: (jnp.where(lat(s), s % per_seq, 0), 0))
    return pl.pallas_call(
        functools.partial(_dft_kernel, n_lat_steps=n_lat_steps),
        grid=(n_lat_steps + n_ctx_steps,),
        in_specs=[fspec, fspec,
                  pl.BlockSpec((seq, 2 * GROUP_W), lambda s: (jnp.where(lat(s), s // per_seq, 0), 0)),
                  _resident(f_ctx[0].shape), _resident(f_ctx[1].shape),
                  pl.BlockSpec((tmd, 2 * GROUP_W), lambda s: (jnp.where(lat(s), n_lat_steps, s), 0))],
        out_specs=pl.BlockSpec((tmd, GROUP_W), lambda s: (s, 0)),
        out_shape=jax.ShapeDtypeStruct(((n_lat_steps + n_ctx_steps) * tmd, GROUP_W), BF16),
        compiler_params=_params(1),
        name="pos_dft",
    )(*f_lat, ab, *f_ctx, ab)


def _attn_kernel(shift_ref, qt_ref, kl_ref, kc_ref, vtl_ref, vtc_ref, *rest, nq, with_ctx_queries, fixed_shift,
                 cast_blocks):
    n_cast = len(cast_blocks)
    o_ref = rest[n_cast]
    step = (pl.program_id(0) * pl.num_programs(1) + pl.program_id(1)) * pl.num_programs(2) + pl.program_id(2)
    for src_ref, dst_ref, n_blocks in zip(rest[:n_cast], rest[n_cast + 1:], cast_blocks):
        @pl.when(step < n_blocks)
        def _(src_ref=src_ref, dst_ref=dst_ref):
            dst_ref[...] = src_ref[...].astype(BF16)

    tq = qt_ref.shape[1]
    seq = kl_ref.shape[0]
    cols = Q_PER_KV * tq
    qt = jnp.concatenate([qt_ref[g * HEAD_DIM:(g + 1) * HEAD_DIM, :] for g in range(Q_PER_KV)], axis=1)

    def finish(l, acc):
        o = acc * (1.0 / l)
        for g in range(Q_PER_KV):
            o_ref[:, g * HEAD_DIM:(g + 1) * HEAD_DIM] = o[:, g * tq:(g + 1) * tq].T.astype(BF16)

    def key_chunks(with_latent_keys):
        chunks = [(kc_ref, vtc_ref, slice(None))]
        if with_latent_keys:
            chunks += [(kl_ref, vtl_ref, slice(j * KV_CHUNK, (j + 1) * KV_CHUNK)) for j in range(seq // KV_CHUNK)]
        return chunks

    def run_fixed(with_latent_keys):
        shift = shift_ref[0, 0]
        acc = None
        for k_ref, vt_ref, ks in key_chunks(with_latent_keys):
            p = jnp.exp2(_dot(k_ref[ks, :], qt) - shift).astype(BF16)
            part = _dot(vt_ref[:, ks], p)
            acc = part if acc is None else acc + part
        finish(acc[HEAD_DIM:HEAD_DIM + 1], acc[:HEAD_DIM])

    def run_online(with_latent_keys):
        m = jnp.full((1, cols), -jnp.inf, F32)
        l = jnp.zeros((1, cols), F32)
        acc = jnp.zeros((HEAD_DIM, cols), F32)
        for k_ref, vt_ref, ks in key_chunks(with_latent_keys):
            s = _dot(k_ref[ks, :], qt)
            m_new = jnp.maximum(m, jnp.max(s, axis=0, keepdims=True))
            alpha = jnp.exp2(m - m_new)
            p = jnp.exp2(s - m_new)
            l = alpha * l + jnp.sum(p, axis=0, keepdims=True)
            acc = alpha * acc + _dot(vt_ref[:HEAD_DIM, ks], p.astype(BF16))
            m = m_new
        finish(l, acc)

    run = run_fixed if fixed_shift else run_online
    if not with_ctx_queries:
        run(True)
    else:
        is_latent = pl.program_id(2) < nq
        pl.when(is_latent)(lambda: run(True))
        pl.when(jnp.logical_not(is_latent))(lambda: run(False))


def _attention(qt, k, vt, q_gain, k_gain, cast_weights, cast_layer, *, batch, seq, ctx_len, with_ctx_queries):
    r_rows = k.shape[0] if with_ctx_queries else batch * seq
    tq = ctx_len
    nq = seq // tq
    n_lat_q = batch * nq
    ctx_blk0 = batch * seq // ctx_len
    qw = Q_PER_KV * HEAD_DIM
    q_blk = lambda b, i: jnp.where(i < nq, b * nq + i, n_lat_q + b)
    n_i = nq + (1 if with_ctx_queries else 0)
    grid = (batch, N_KV_HEADS, n_i)

    n_steps = math.prod(grid)
    widths = [next(wd for wd in range(CAST_STRIP, w.shape[2] + 1, CAST_STRIP)
                   if w.shape[2] % wd == 0 and w.shape[2] // wd <= n_steps) for w in cast_weights]
    cast_blocks = tuple(w.shape[2] // wd for w, wd in zip(cast_weights, widths))
    strip = lambda nb: lambda b, hg, i: jnp.minimum((b * N_KV_HEADS + hg) * n_i + i, nb - 1)
    cast_in = [pl.BlockSpec((None, w.shape[1], wd), lambda b, hg, i, f=strip(nb): (cast_layer, 0, f(b, hg, i)))
               for w, wd, nb in zip(cast_weights, widths, cast_blocks)]
    cast_out = [pl.BlockSpec((w.shape[1], wd), lambda b, hg, i, f=strip(nb): (0, f(b, hg, i)))
                for w, wd, nb in zip(cast_weights, widths, cast_blocks)]
    cast_shapes = [jax.ShapeDtypeStruct(w.shape[1:], BF16) for w in cast_weights]

    def call(fixed_shift, shift):
        return pl.pallas_call(
            functools.partial(_attn_kernel, nq=nq, with_ctx_queries=with_ctx_queries, fixed_shift=fixed_shift,
                              cast_blocks=cast_blocks),
            grid=grid,
            in_specs=[pl.BlockSpec(memory_space=pltpu.SMEM),
                      pl.BlockSpec((qw, tq), lambda b, hg, i: (hg, q_blk(b, i))),
                      pl.BlockSpec((seq, HEAD_DIM), lambda b, hg, i: (b, hg)),
                      pl.BlockSpec((ctx_len, HEAD_DIM), lambda b, hg, i: (ctx_blk0 + b, hg)),
                      pl.BlockSpec((VT_ROWS, seq), lambda b, hg, i: (hg, b)),
                      pl.BlockSpec((VT_ROWS, ctx_len), lambda b, hg, i: (hg, ctx_blk0 + b))] + cast_in,
            out_specs=[pl.BlockSpec((tq, qw), lambda b, hg, i: (q_blk(b, i), hg))] + cast_out,
            out_shape=[jax.ShapeDtypeStruct((r_rows, ATTN_W), BF16)] + cast_shapes,
            compiler_params=_params(3),
            name="attention_fixed" if fixed_shift else "attention_online",
        )(shift, qt, k, k, vt, vt, *cast_weights)

    bound = (HEAD_DIM * Q_SCALE * (1.0 + 2.0 ** -6)) * jnp.max(jnp.abs(q_gain)) * jnp.max(jnp.abs(k_gain))
    shift = bound.astype(F32).reshape(1, 1)
    out = lax.cond(bound <= MAX_FIXED_SHIFT, lambda: call(True, shift), lambda: call(False, shift))
    return out[0], out[1:]


def _seq_edge_masks(tile, tm, n_lat_tiles, seq, ctx_len):
    r = lax.broadcasted_iota(jnp.int32, (tm, 1), 0)
    period = jnp.where(tile < n_lat_tiles, seq, ctx_len)
    pos = (tile * tm + r) & (period - 1)
    return pos == 0, pos == period - 1


def _dwconv3(center, prev_row, next_row, w_ref, is_start, is_end):
    tm = center.shape[0]
    r = lax.broadcasted_iota(jnp.int32, (tm, 1), 0)
    up = jnp.where(r == 0, prev_row, pltpu.roll(center, 1, 0))
    dn = jnp.where(r == tm - 1, next_row, pltpu.roll(center, tm - 1, 0))
    up = jnp.where(is_start, 0.0, up)
    dn = jnp.where(is_end, 0.0, dn)
    return up * w_ref[0:1, :] + center * w_ref[1:2, :] + dn * w_ref[2:3, :]


def _out_kernel(attn_ref, four_ref, cb_ref, t_ref, tp_ref, tn_ref, gm_ref, cw_ref, w_ref, xl_ref, xc_ref, m_ref,
                o_ref, mix_ref, *, n_lat_tiles, seq, ctx_len):
    tm = xl_ref.shape[0]
    i = pl.program_id(0)
    is_start, is_end = _seq_edge_masks(i, tm, n_lat_tiles, seq, ctx_len)
    conv = _dwconv3(t_ref[...].astype(F32), tp_ref[SUBLANES - 1:SUBLANES, :].astype(F32),
                    tn_ref[0:1, :].astype(F32), cw_ref, is_start, is_end)
    mix_ref[:, :ATTN_W] = attn_ref[...]
    mix_ref[:, ATTN_W:ATTN_W + GROUP_W] = four_ref[...]
    mix_ref[:, ATTN_W + GROUP_W:ATTN_W + 2 * GROUP_W] = (cb_ref[...].astype(F32) * conv).astype(BF16)
    mix_ref[:, ATTN_W + 2 * GROUP_W:] = gm_ref[...]
    x = jnp.where(i < n_lat_tiles, xl_ref[...], xc_ref[...])
    o_ref[...] = x + m_ref[2:3, :] * _dot(mix_ref[...], w_ref[...])


def _halo_specs(tm, width, n_rows):
    per = tm // SUBLANES
    last = n_rows // SUBLANES - 1
    prev = pl.BlockSpec((SUBLANES, width), lambda i, *_: (jnp.maximum(i * per - 1, 0), 0))
    nxt = pl.BlockSpec((SUBLANES, width), lambda i, *_: (jnp.minimum((i + 1) * per, last), 0))
    return prev, nxt


def _out_proj(attn, four, cb, t, gm, conv_w, w_out, rows_src, modt, layer, *, lay, n_rows):
    x_lat, x_ctx, ctx_block0 = rows_src
    d = x_lat.shape[1]
    tm = lay["tm"]
    grp = lay["grp"]
    row = lambda w: pl.BlockSpec((tm, w), lambda i: (i, 0))
    tprev, tnext = _halo_specs(tm, GROUP_W, t.shape[0])
    kern = functools.partial(_out_kernel, n_lat_tiles=lay["n_lat_tiles"], seq=lay["seq"], ctx_len=lay["ctx_len"])
    return pl.pallas_call(
        kern,
        grid=(n_rows // tm,),
        in_specs=[row(ATTN_W), row(GROUP_W), row(GROUP_W), row(GROUP_W), tprev, tnext, row(GROUP_W),
                  _resident(conv_w.shape), _resident_layer(w_out.shape),
                  *_stream_specs(tm, d, lay["n_lat_tiles"], ctx_block0),
                  pl.BlockSpec((None, MOD_ROWS, d), lambda i: (layer, 0, grp(i)))],
        out_specs=row(d),
        out_shape=jax.ShapeDtypeStruct((n_rows, d), F32),
        scratch_shapes=[pltpu.VMEM((tm, MIX_W), BF16)],
        compiler_params=_params(1),
        name="out_proj",
    )(attn, four, cb, t, t, t, gm, conv_w, w_out, x_lat, x_ctx, modt)


def _ffn_kernel(x_ref, xp_ref, xn_ref, m_ref, g2_ref, wg_ref, wu_ref, cw_ref, cb_ref, wd_ref, fg_ref,
                o_ref, h_ref, *, n_lat_tiles, seq, ctx_len, final_norm):
    tm = x_ref.shape[0]
    i = pl.program_id(0)
    j = pl.program_id(1)

    @pl.when(j == 0)
    def _():
        gain = g2_ref[...] * (1.0 + m_ref[4:5, :])
        shift = m_ref[3:4, :]

        def norm_mod(x):
            return (_rms(x) * gain + shift).astype(BF16)

        for r0 in range(0, tm, NORM_ROWS):
            h_ref[r0:r0 + NORM_ROWS, :] = norm_mod(x_ref[r0:r0 + NORM_ROWS, :])
        h_ref[tm:tm + SUBLANES, :] = norm_mod(xn_ref[...])
        h_ref[tm + SUBLANES:, :] = norm_mod(xp_ref[...])
        o_ref[...] = jnp.zeros_like(o_ref)

    is_start, is_end = _seq_edge_masks(i, tm, n_lat_tiles, seq, ctx_len)
    g = _dot(h_ref[...], wg_ref[...])
    u = _dot(h_ref[0:tm, :], wu_ref[...])
    conv = _dwconv3(g[0:tm], g[tm + 2 * SUBLANES - 1:tm + 2 * SUBLANES], g[tm:tm + 1], cw_ref, is_start, is_end)
    act = (_silu(conv + cb_ref[...]) * u).astype(BF16)
    o_ref[...] += _dot(act, wd_ref[...])

    @pl.when(j == pl.num_programs(1) - 1)
    def _():
        gate = m_ref[5:6, :]
        for r0 in range(0, tm, NORM_ROWS):
            rows = slice(r0, r0 + NORM_ROWS)
            y = x_ref[rows, :] + gate * o_ref[rows, :]
            if final_norm:
                y = _rms(y) * fg_ref[...]
            o_ref[rows, :] = y


def _ffn(xall, modt, layer, g2, w_up, conv_w, conv_b, w_down, final_g, *, lay, n_rows, final_norm):
    d = xall.shape[1]
    d_ff = w_down.shape[1]
    tm = lay["tm"]
    grp = lay["grp"]
    nj = d_ff // FF_CHUNK
    xprev, xnext = _halo_specs(tm, d, xall.shape[0])
    kern = functools.partial(_ffn_kernel, n_lat_tiles=lay["n_lat_tiles"], seq=lay["seq"],
                             ctx_len=lay["ctx_len"], final_norm=final_norm)
    return pl.pallas_call(
        kern,
        grid=(n_rows // tm, nj),
        in_specs=[pl.BlockSpec((tm, d), lambda i, j: (i, 0)), xprev, xnext,
                  pl.BlockSpec((None, MOD_ROWS, d), lambda i, j: (layer, 0, grp(i))),
                  pl.BlockSpec((1, d), lambda i, j: (0, 0)),
                  pl.BlockSpec((None, d, FF_CHUNK), lambda i, j: (0, 0, j)),
                  pl.BlockSpec((None, d, FF_CHUNK), lambda i, j: (0, 0, nj + j)),
                  pl.BlockSpec((SUBLANES, FF_CHUNK), lambda i, j: (0, j)),
                  pl.BlockSpec((1, FF_CHUNK), lambda i, j: (0, j)),
                  pl.BlockSpec((None, FF_CHUNK, d), lambda i, j: (0, j, 0)),
                  pl.BlockSpec((1, d), lambda i, j: (0, 0))],
        out_specs=pl.BlockSpec((tm, d), lambda i, j: (i, 0)),
        out_shape=jax.ShapeDtypeStruct((n_rows, d), F32),
        scratch_shapes=[pltpu.VMEM((tm + 2 * SUBLANES, d), BF16)],
        compiler_params=_params(2, V7X_VMEM_FFN_LIMIT_BYTES),
        name="ffn",
    )(xall, xall, xall, modt, g2, w_up, w_up, conv_w, conv_b, w_down, final_g)


def _rope_tables(seq, tm):
    pos = jnp.arange(seq, dtype=jnp.int32)
    row = (pos // GRID_W).astype(F32)
    col = (pos % GRID_W).astype(F32)
    freqs = ROPE_THETA ** (-jnp.arange(0, ROPE_AXIS_DIM, 2, dtype=F32) / ROPE_AXIS_DIM)
    ang_r = row[:, None] * freqs[None, :]
    ang_c = col[:, None] * freqs[None, :]
    cos = jnp.concatenate([jnp.cos(ang_r)] * 2 + [jnp.cos(ang_c)] * 2, axis=-1)
    sin = jnp.concatenate([-jnp.sin(ang_r), jnp.sin(ang_r), -jnp.sin(ang_c), jnp.sin(ang_c)], axis=-1)
    cos = jnp.concatenate([cos, jnp.ones((tm, HEAD_DIM), F32)], axis=0)
    sin = jnp.concatenate([sin, jnp.zeros((tm, HEAD_DIM), F32)], axis=0)
    return cos, sin


def _dft_matrix(n, scale):
    n1 = 1
    while n1 * n1 < n:
        n1 *= 2
    n2 = n // n1
    t = jnp.arange(n, dtype=jnp.int32)[None, :]
    a = jnp.arange(n1, dtype=jnp.int32)[:, None]
    b = jnp.arange(n2, dtype=jnp.int32)[:, None]
    ang_a = ((a * t) % n1).astype(F32) * (2.0 * math.pi / n1)
    ang_b = ((b * t) % n).astype(F32) * (2.0 * math.pi / n)
    ca, sa = jnp.cos(ang_a)[:, None, :], jnp.sin(ang_a)[:, None, :]
    cb, sb = jnp.cos(ang_b)[None, :, :] * scale, jnp.sin(ang_b)[None, :, :] * scale
    c = (ca * cb - sa * sb).reshape(n, n)
    s = (sa * cb + ca * sb).reshape(n, n)
    return c.astype(BF16), (-s).astype(BF16)


def _channel_dft(scale):
    k = jnp.arange(HEAD_DIM, dtype=jnp.int32)
    ang = ((k[:, None] * k[None, :]) % HEAD_DIM).astype(F32) * (2.0 * math.pi / HEAD_DIM)
    return (jnp.concatenate([jnp.cos(ang), jnp.sin(ang)], axis=1) * scale).astype(BF16)


def _layout(batch, seq, ctx_len, max_tm):
    n_lat, n_ctx = batch * seq, batch * ctx_len
    tm = max_tm
    while n_ctx % tm or seq % tm:
        tm //= 2
    assert tm >= CHUNK and seq & (seq - 1) == 0 and ctx_len & (ctx_len - 1) == 0
    assert seq % GRID_W == 0 and ctx_len % CHUNK == 0 and n_lat % ctx_len == 0 and n_lat % seq == 0
    n_lat_tiles, tps = n_lat // tm, seq // tm
    grp = lambda i: jnp.where(i < n_lat_tiles, 1 + i // tps, 0)
    return dict(tm=tm, n_lat_tiles=n_lat_tiles, tiles_per_seq=tps, grp=grp, seq=seq, ctx_len=ctx_len,
                n_lat=n_lat, n_ctx=n_ctx)


def kernel(x, c, ctx, c_ctx, w_mod, b_mod, norm1_g, norm2_g, w_in, q_norm_g, k_norm_g, conv_w, gm_ln_g,
           gm_ln_b, gm_ws, gm_b, w_out, w_up, ffn_conv_w, ffn_conv_b, w_down, final_norm_g):
    batch, seq, d = x.shape
    ctx_len = ctx.shape[1]
    depth = w_mod.shape[0]
    d_ff = w_down.shape[1]
    assert batch + 1 <= MOD_ROWS and w_in.shape[2] == IN_W and w_down.shape[1] % FF_CHUNK == 0
    lay = _layout(batch, seq, ctx_len, ROW_TILE)
    lay_ffn = _layout(batch, seq, ctx_len, FFN_ROW_TILE)
    n_lat, tm = lay["n_lat"], lay["tm"]

    rows_src = (x.reshape(n_lat, d), ctx.reshape(batch * ctx_len, d), 0)

    cs = jnp.concatenate([c_ctx[None, :], c, jnp.zeros((MOD_ROWS - 1 - batch, d), F32)], axis=0)
    mod = _modulation(cs, w_mod, b_mod)
    modt = mod.reshape(depth, MOD_ROWS, N_MOD, d).transpose(0, 2, 1, 3).reshape(depth, N_MOD, MOD_ROWS * d)
    modt = jnp.pad(modt, ((0, 0), (0, MOD_ROWS - N_MOD), (0, 0)))

    cos_t, sin_t = _rope_tables(seq, tm)
    f_lat = _dft_matrix(seq, seq ** -0.5)
    f_ctx = _dft_matrix(ctx_len, ctx_len ** -0.5)
    dftc = _channel_dft(HEAD_DIM ** -0.5)

    pad_rows = lambda w: jnp.pad(w, ((0, 0), (0, SUBLANES - w.shape[1]), (0, 0)))
    conv_w8 = pad_rows(conv_w)
    ffn_conv_w8 = pad_rows(ffn_conv_w)
    bsf = jnp.broadcast_to(gm_b[..., None], gm_b.shape + (HEAD_DIM,))
    gm_ws_b = gm_ws.astype(BF16)
    fg = final_norm_g.reshape(1, d)

    big_weights = (w_in, w_out, w_up, w_down)
    w_in_b, w_out_b, w_up_b, w_down_b = (w[:1].astype(BF16) for w in big_weights)

    for l in range(depth):
        last = l == depth - 1
        qt, k, vt, ab, cb, t, gm = _in_proj(
            rows_src, modt, l, norm1_g[l].reshape(1, d), w_in_b, q_norm_g[l].reshape(1, HEAD_DIM),
            k_norm_g[l].reshape(1, HEAD_DIM), cos_t, sin_t, dftc, gm_ln_g[l].reshape(1, GROUP_W),
            gm_ln_b[l].reshape(1, GROUP_W), gm_ws_b[l], bsf[l], lay=lay)
        four = _pos_dft(f_lat, f_ctx, ab, batch=batch, seq=seq, ctx_len=ctx_len, with_ctx=not last)
        attn, next_weights = _attention(qt, k, vt, q_norm_g[l], k_norm_g[l], () if last else big_weights, l + 1,
                                        batch=batch, seq=seq, ctx_len=ctx_len, with_ctx_queries=not last)
        n_rows = n_lat if last else n_lat + lay["n_ctx"]
        xall = _out_proj(attn, four, cb, t, gm, conv_w8[l], w_out_b, rows_src, modt, l, lay=lay, n_rows=n_rows)
        xall = _ffn(xall, modt, l, norm2_g[l].reshape(1, d), w_up_b, ffn_conv_w8[l],
                    ffn_conv_b[l].reshape(1, d_ff), w_down_b, fg, lay=lay_ffn, n_rows=n_rows, final_norm=last)
        rows_src = (xall, xall, lay["n_lat_tiles"])
        if not last:
            w_in_b, w_out_b, w_up_b, w_down_b = (w[None] for w in next_weights)
    return xall.reshape(batch, seq, d)
```

```python
import functools
import math

import jax
import jax.numpy as jnp
from jax import lax
from jax.experimental import pallas as pl
from jax.experimental.pallas import tpu as pltpu

F32 = jnp.float32
BF16 = jnp.bfloat16

GRID_W = 64
HEAD_DIM = 128
N_HEADS = 8
N_KV_HEADS = 2
Q_PER_KV = N_HEADS // N_KV_HEADS
ATTN_W = N_HEADS * HEAD_DIM
KV_W = N_KV_HEADS * HEAD_DIM
ROPE_THETA = 10000.0
ROPE_AXIS_DIM = HEAD_DIM // 2
ATTN_SCALE = HEAD_DIM ** -0.5
Q_SCALE = ATTN_SCALE * math.log2(math.e)
GROUP_W = 4 * HEAD_DIM
CHUNK = 128
N_MOD = 6
EPS = 1e-6

OFF_K = ATTN_W
OFF_V = OFF_K + KV_W
OFF_F = OFF_V + KV_W
OFF_CB = OFF_F + GROUP_W
OFF_CC = OFF_CB + GROUP_W
OFF_CH = OFF_CC + GROUP_W
OFF_GU = OFF_CH + GROUP_W
OFF_GV = OFF_GU + GROUP_W
IN_W = OFF_GV + GROUP_W
MIX_W = ATTN_W + 3 * GROUP_W

V7X_VMEM_LIMIT_BYTES = 56 * 1024 * 1024
V7X_VMEM_FFN_LIMIT_BYTES = 61 * 1024 * 1024
SUBLANES = 8
MOD_ROWS = 8
ROW_TILE = 512
FFN_ROW_TILE = 1024
FF_CHUNK = 512
FFN_FIRST_BLOCKS = 4
KV_CHUNK = 1024
CAST_STRIP = 128
BF16_SUBLANES = 16
NORM_ROWS = BF16_SUBLANES
VT_ROWS = HEAD_DIM + BF16_SUBLANES
MAX_FIXED_SHIFT = 60.0
DFT_ROWS = 512


def _params(n_axes, vmem_limit_bytes=V7X_VMEM_LIMIT_BYTES):
    return pltpu.CompilerParams(dimension_semantics=("arbitrary",) * n_axes, vmem_limit_bytes=vmem_limit_bytes)


def _resident(shape):
    return pl.BlockSpec(shape, lambda *_: (0,) * len(shape), pipeline_mode=pl.Buffered(1))


def _resident_layer(stacked_shape):
    rest = tuple(stacked_shape[1:])
    assert stacked_shape[0] == 1
    return pl.BlockSpec((None,) + rest, lambda *_: (0,) * (1 + len(rest)), pipeline_mode=pl.Buffered(1))


def _stream_specs(tm, d, n_lat_tiles, split):
    if not split:
        return pl.BlockSpec((tm, d), lambda i, *_: (i, 0)), pl.BlockSpec((SUBLANES, d), lambda i, *_: (0, 0))
    lat = pl.BlockSpec((tm, d), lambda i, *_: (jnp.minimum(i, n_lat_tiles - 1), 0))
    ctx = pl.BlockSpec((tm, d), lambda i, *_: (jnp.maximum(i - n_lat_tiles, 0), 0))
    return lat, ctx


def _stream_rows(tile, n_lat_tiles, split, rows_ref, ctx_ref):
    return jnp.where(tile < n_lat_tiles, rows_ref[...], ctx_ref[...]) if split else rows_ref[...]


def _dot(a, b):
    return jnp.dot(a, b, preferred_element_type=F32)


def _gelu_tanh(x):
    return 0.5 * x * (1.0 + jnp.tanh(math.sqrt(2.0 / math.pi) * (x + 0.044715 * (x * x * x))))


def _silu(x):
    return x * (1.0 / (1.0 + jnp.exp(-x)))


def _rms(x, eps=EPS):
    return x * lax.rsqrt(jnp.mean(x * x, axis=-1, keepdims=True) + eps)


def _split_bf16(x):
    hi = x.astype(BF16)
    return hi, (x - hi.astype(F32)).astype(BF16)


def _mod_kernel(c_ref, w_ref, b_ref, o_ref):
    s_hi, s_lo = _split_bf16(_silu(c_ref[...]))
    w_hi, w_lo = _split_bf16(w_ref[...])
    r = _dot(jnp.concatenate([s_hi, s_lo], axis=0), w_hi)
    o_ref[...] = r[:MOD_ROWS] + r[MOD_ROWS:] + _dot(s_hi, w_lo) + b_ref[...]


def _modulation(cs, w_mod, b_mod):
    depth, d, n = w_mod.shape
    tn = next(t for t in (1024, 512, 256, 128) if n % t == 0)
    return pl.pallas_call(
        _mod_kernel,
        grid=(depth, n // tn),
        in_specs=[pl.BlockSpec((MOD_ROWS, d), lambda l, j: (0, 0)),
                  pl.BlockSpec((None, d, tn), lambda l, j: (l, 0, j)),
                  pl.BlockSpec((None, 1, tn), lambda l, j: (l, 0, j))],
        out_specs=pl.BlockSpec((None, MOD_ROWS, tn), lambda l, j: (l, 0, j)),
        out_shape=jax.ShapeDtypeStruct((depth, MOD_ROWS, n), F32),
        compiler_params=_params(2),
        name="modulation",
    )(cs, w_mod, b_mod.reshape(depth, 1, n))


def _in_kernel(xl_ref, xc_ref, m_ref, g1_ref, w_ref, qg_ref, kg_ref, cos_ref, sin_ref, dftc_ref,
               lng_ref, lnb_ref, ws_ref, bs_ref,
               qt_ref, k_ref, vt_ref, ab_ref, cb_ref, t_ref, gm_ref, *, n_lat_tiles, split_rows):
    tm = xl_ref.shape[0]
    x = _stream_rows(pl.program_id(0), n_lat_tiles, split_rows, xl_ref, xc_ref)
    h = _rms(x) * g1_ref[...]
    h = h * (1.0 + m_ref[1:2, :]) + m_ref[0:1, :]
    hb = h.astype(BF16)

    def proj(lo, width=GROUP_W):
        return _dot(hb, w_ref[:, lo:lo + width])

    cos = cos_ref[...]
    sin = sin_ref[...]
    lane = lax.broadcasted_iota(jnp.int32, (tm, HEAD_DIM), 1)
    first_half = (lane % (ROPE_AXIS_DIM)) < (ROPE_AXIS_DIM // 2)

    def norm_rope(ph, gain):
        y = _rms(ph) * gain
        partner = jnp.where(first_half,
                            pltpu.roll(y, HEAD_DIM - ROPE_AXIS_DIM // 2, 1),
                            pltpu.roll(y, ROPE_AXIS_DIM // 2, 1))
        return y * cos + partner * sin

    qg = qg_ref[...] * Q_SCALE
    for half in range(ATTN_W // GROUP_W):
        p = proj(half * GROUP_W)
        for hh in range(GROUP_W // HEAD_DIM):
            c0 = half * GROUP_W + hh * HEAD_DIM
            y = norm_rope(p[:, hh * HEAD_DIM:(hh + 1) * HEAD_DIM], qg)
            qt_ref[c0:c0 + HEAD_DIM, :] = y.T.astype(BF16)

    p = proj(OFF_K)
    kg = kg_ref[...]
    for hh in range(N_KV_HEADS):
        cols = slice(hh * HEAD_DIM, (hh + 1) * HEAD_DIM)
        k_ref[:, cols] = norm_rope(p[:, cols], kg).astype(BF16)
        r0 = hh * VT_ROWS
        vt_ref[r0:r0 + HEAD_DIM, :] = p[:, KV_W + hh * HEAD_DIM:KV_W + (hh + 1) * HEAD_DIM].T.astype(BF16)
        vt_ref[r0 + HEAD_DIM:r0 + VT_ROWS, :] = jnp.ones((BF16_SUBLANES, tm), BF16)

    p = proj(OFF_F).astype(BF16)
    dftc = dftc_ref[...]
    for g in range(GROUP_W // HEAD_DIM):
        r = _dot(p[:, g * HEAD_DIM:(g + 1) * HEAD_DIM], dftc)
        ab_ref[:, g * HEAD_DIM:(g + 1) * HEAD_DIM] = r[:, :HEAD_DIM].astype(BF16)
        ab_ref[:, GROUP_W + g * HEAD_DIM:GROUP_W + (g + 1) * HEAD_DIM] = r[:, HEAD_DIM:].astype(BF16)

    cb_ref[...] = proj(OFF_CB).astype(BF16)
    t_ref[...] = (proj(OFF_CC) * proj(OFF_CH)).astype(BF16)

    u = _gelu_tanh(proj(OFF_GU))
    gv = _gelu_tanh(proj(OFF_GV))
    gc = gv - jnp.mean(gv, axis=-1, keepdims=True)
    vn = gc * lax.rsqrt(jnp.mean(gc * gc, axis=-1, keepdims=True) + EPS) * lng_ref[...] + lnb_ref[...]
    vn = vn.astype(BF16)
    for g in range(GROUP_W // HEAD_DIM):
        wsg = ws_ref[g]
        bsg = bs_ref[g]
        for c in range(tm // CHUNK):
            rows = slice(c * CHUNK, (c + 1) * CHUNK)
            cols = slice(g * HEAD_DIM, (g + 1) * HEAD_DIM)
            s = _dot(wsg, vn[rows, cols]) + bsg
            gm_ref[rows, cols] = (u[rows, cols] * s).astype(BF16)


def _in_proj(rows_src, modt, layer, g1, w_in, qg, kg, cos_t, sin_t, dftc, lng, lnb, ws, bsf, *, lay):
    x_lat, x_ctx, split_rows = rows_src
    r_rows, d = lay["n_lat"] + lay["n_ctx"], x_lat.shape[1]
    tm = lay["tm"]
    grp, nlt, tps = lay["grp"], lay["n_lat_tiles"], lay["tiles_per_seq"]
    row = lambda w: pl.BlockSpec((tm, w), lambda i: (i, 0))
    vec = lambda w: pl.BlockSpec((1, w), lambda i: (0, 0))
    tab = pl.BlockSpec((tm, HEAD_DIM), lambda i: (jnp.where(i < nlt, i % tps, tps), 0))
    col = lambda h: pl.BlockSpec((h, tm), lambda i: (0, i))
    rows_bf16 = lambda w: jax.ShapeDtypeStruct((r_rows, w), BF16)
    cols_bf16 = lambda h: jax.ShapeDtypeStruct((h, r_rows), BF16)
    return pl.pallas_call(
        functools.partial(_in_kernel, n_lat_tiles=nlt, split_rows=split_rows),
        grid=(r_rows // tm,),
        in_specs=[*_stream_specs(tm, d, nlt, split_rows),
                  pl.BlockSpec((None, MOD_ROWS, d), lambda i: (layer, 0, grp(i))),
                  vec(d), _resident_layer(w_in.shape), vec(HEAD_DIM), vec(HEAD_DIM), tab, tab,
                  _resident(dftc.shape), vec(GROUP_W), vec(GROUP_W),
                  _resident(ws.shape), _resident(bsf.shape)],
        out_specs=[col(ATTN_W), row(KV_W), col(N_KV_HEADS * VT_ROWS), row(2 * GROUP_W), row(GROUP_W), row(GROUP_W),
                   row(GROUP_W)],
        out_shape=[cols_bf16(ATTN_W), rows_bf16(KV_W), cols_bf16(N_KV_HEADS * VT_ROWS), rows_bf16(2 * GROUP_W),
                   rows_bf16(GROUP_W), rows_bf16(GROUP_W), rows_bf16(GROUP_W)],
        compiler_params=_params(1),
        name="in_proj",
    )(x_lat, x_ctx, modt, g1, w_in, qg, kg, cos_t, sin_t, dftc, lng, lnb, ws, bsf)


def _dft_kernel(fc_ref, fs_ref, ab_ref, cc_ref, cs_ref, abc_ref, o_ref, *, n_lat_steps):
    def dft(c, s, a, b):
        return (_dot(c, a) + _dot(s, b)).astype(BF16)

    @pl.when(pl.program_id(0) < n_lat_steps)
    def _():
        o_ref[...] = dft(fc_ref[...], fs_ref[...], ab_ref[:, :GROUP_W], ab_ref[:, GROUP_W:])

    @pl.when(pl.program_id(0) >= n_lat_steps)
    def _():
        n_ctx = cc_ref.shape[0]
        for r0 in range(0, o_ref.shape[0], n_ctx):
            rows = slice(r0, r0 + n_ctx)
            o_ref[rows, :] = dft(cc_ref[...], cs_ref[...], abc_ref[rows, :GROUP_W], abc_ref[rows, GROUP_W:])


def _pos_dft(f_lat, f_ctx, ab, *, batch, seq, ctx_len, with_ctx):
    tmd = min(DFT_ROWS, seq)
    assert tmd % ctx_len == 0 and (batch * ctx_len) % tmd == 0
    per_seq = seq // tmd
    n_lat_steps = batch * per_seq
    n_ctx_steps = batch * ctx_len // tmd if with_ctx else 0
    lat = lambda s: s < n_lat_steps
    fspec = pl.BlockSpec((tmd, seq), lambda s: (jnp.where(lat(s), s % per_seq, 0), 0))
    return pl.pallas_call(
        functools.partial(_dft_kernel, n_lat_steps=n_lat_steps),
        grid=(n_lat_steps + n_ctx_steps,),
        in_specs=[fspec, fspec,
                  pl.BlockSpec((seq, 2 * GROUP_W), lambda s: (jnp.where(lat(s), s // per_seq, 0), 0)),
                  _resident(f_ctx[0].shape), _resident(f_ctx[1].shape),
                  pl.BlockSpec((tmd, 2 * GROUP_W), lambda s: (jnp.where(lat(s), n_lat_steps, s), 0))],
        out_specs=pl.BlockSpec((tmd, GROUP_W), lambda s: (s, 0)),
        out_shape=jax.ShapeDtypeStruct(((n_lat_steps + n_ctx_steps) * tmd, GROUP_W), BF16),
        compiler_params=_params(1),
        name="pos_dft",
    )(*f_lat, ab, *f_ctx, ab)


def _attn_kernel(shift_ref, qt_ref, kl_ref, kc_ref, vtl_ref, vtc_ref, *rest, nq, with_ctx_queries, fixed_shift,
                 cast_blocks):
    n_cast = len(cast_blocks)
    o_ref = rest[n_cast]
    step = (pl.program_id(0) * pl.num_programs(1) + pl.program_id(1)) * pl.num_programs(2) + pl.program_id(2)
    for src_ref, dst_ref, n_blocks in zip(rest[:n_cast], rest[n_cast + 1:], cast_blocks):
        @pl.when(step < n_blocks)
        def _(src_ref=src_ref, dst_ref=dst_ref):
            dst_ref[...] = src_ref[...].astype(BF16)

    tq = qt_ref.shape[1]
    seq = kl_ref.shape[0]
    cols = Q_PER_KV * tq
    qt = jnp.concatenate([qt_ref[g * HEAD_DIM:(g + 1) * HEAD_DIM, :] for g in range(Q_PER_KV)], axis=1)

    def finish(l, acc):
        o = acc * (1.0 / l)
        for g in range(Q_PER_KV):
            o_ref[:, g * HEAD_DIM:(g + 1) * HEAD_DIM] = o[:, g * tq:(g + 1) * tq].T.astype(BF16)

    def key_chunks(with_latent_keys):
        chunks = [(kc_ref, vtc_ref, slice(None))]
        if with_latent_keys:
            chunks += [(kl_ref, vtl_ref, slice(j * KV_CHUNK, (j + 1) * KV_CHUNK)) for j in range(seq // KV_CHUNK)]
        return chunks

    def run_fixed(with_latent_keys):
        shift = shift_ref[0, 0]
        acc = None
        for k_ref, vt_ref, ks in key_chunks(with_latent_keys):
            p = jnp.exp2(_dot(k_ref[ks, :], qt) - shift).astype(BF16)
            part = _dot(vt_ref[:, ks], p)
            acc = part if acc is None else acc + part
        finish(acc[HEAD_DIM:HEAD_DIM + 1], acc[:HEAD_DIM])

    def run_online(with_latent_keys):
        m = jnp.full((1, cols), -jnp.inf, F32)
        l = jnp.zeros((1, cols), F32)
        acc = jnp.zeros((HEAD_DIM, cols), F32)
        for k_ref, vt_ref, ks in key_chunks(with_latent_keys):
            s = _dot(k_ref[ks, :], qt)
            m_new = jnp.maximum(m, jnp.max(s, axis=0, keepdims=True))
            alpha = jnp.exp2(m - m_new)
            p = jnp.exp2(s - m_new)
            l = alpha * l + jnp.sum(p, axis=0, keepdims=True)
            acc = alpha * acc + _dot(vt_ref[:HEAD_DIM, ks], p.astype(BF16))
            m = m_new
        finish(l, acc)

    run = run_fixed if fixed_shift else run_online
    if not with_ctx_queries:
        run(True)
    else:
        is_latent = pl.program_id(2) < nq
        pl.when(is_latent)(lambda: run(True))
        pl.when(jnp.logical_not(is_latent))(lambda: run(False))


def _attention(qt, k, vt, q_gain, k_gain, cast_jobs, *, batch, seq, ctx_len, with_ctx_queries):
    cast_weights = [w for w, _ in cast_jobs]
    cast_layers = [layer for _, layer in cast_jobs]
    r_rows = k.shape[0] if with_ctx_queries else batch * seq
    tq = ctx_len
    nq = seq // tq
    n_lat_q = batch * nq
    ctx_blk0 = batch * seq // ctx_len
    qw = Q_PER_KV * HEAD_DIM
    q_blk = lambda b, i: jnp.where(i < nq, b * nq + i, n_lat_q + b)
    n_i = nq + (1 if with_ctx_queries else 0)
    grid = (batch, N_KV_HEADS, n_i)

    n_steps = math.prod(grid)
    widths = [next(wd for wd in range(CAST_STRIP, w.shape[2] + 1, CAST_STRIP)
                   if w.shape[2] % wd == 0 and w.shape[2] // wd <= n_steps) for w in cast_weights]
    cast_blocks = tuple(w.shape[2] // wd for w, wd in zip(cast_weights, widths))
    strip = lambda nb: lambda b, hg, i: jnp.minimum((b * N_KV_HEADS + hg) * n_i + i, nb - 1)
    cast_in = [pl.BlockSpec((None, w.shape[1], wd), lambda b, hg, i, f=strip(nb), layer=layer: (layer, 0, f(b, hg, i)))
               for w, wd, nb, layer in zip(cast_weights, widths, cast_blocks, cast_layers)]
    cast_out = [pl.BlockSpec((w.shape[1], wd), lambda b, hg, i, f=strip(nb): (0, f(b, hg, i)))
                for w, wd, nb in zip(cast_weights, widths, cast_blocks)]
    cast_shapes = [jax.ShapeDtypeStruct(w.shape[1:], BF16) for w in cast_weights]

    def call(fixed_shift, shift):
        return pl.pallas_call(
            functools.partial(_attn_kernel, nq=nq, with_ctx_queries=with_ctx_queries, fixed_shift=fixed_shift,
                              cast_blocks=cast_blocks),
            grid=grid,
            in_specs=[pl.BlockSpec(memory_space=pltpu.SMEM),
                      pl.BlockSpec((qw, tq), lambda b, hg, i: (hg, q_blk(b, i))),
                      pl.BlockSpec((seq, HEAD_DIM), lambda b, hg, i: (b, hg)),
                      pl.BlockSpec((ctx_len, HEAD_DIM), lambda b, hg, i: (ctx_blk0 + b, hg)),
                      pl.BlockSpec((VT_ROWS, seq), lambda b, hg, i: (hg, b)),
                      pl.BlockSpec((VT_ROWS, ctx_len), lambda b, hg, i: (hg, ctx_blk0 + b))] + cast_in,
            out_specs=[pl.BlockSpec((tq, qw), lambda b, hg, i: (q_blk(b, i), hg))] + cast_out,
            out_shape=[jax.ShapeDtypeStruct((r_rows, ATTN_W), BF16)] + cast_shapes,
            compiler_params=_params(3),
            name="attention_fixed" if fixed_shift else "attention_online",
        )(shift, qt, k, k, vt, vt, *cast_weights)

    bound = (HEAD_DIM * Q_SCALE * (1.0 + 2.0 ** -6)) * jnp.max(jnp.abs(q_gain)) * jnp.max(jnp.abs(k_gain))
    shift = bound.astype(F32).reshape(1, 1)
    out = lax.cond(bound <= MAX_FIXED_SHIFT, lambda: call(True, shift), lambda: call(False, shift))
    return out[0], out[1:]


def _seq_edge_masks(tile, tm, n_lat_tiles, seq, ctx_len):
    r = lax.broadcasted_iota(jnp.int32, (tm, 1), 0)
    period = jnp.where(tile < n_lat_tiles, seq, ctx_len)
    pos = (tile * tm + r) & (period - 1)
    return pos == 0, pos == period - 1


def _dwconv3(center, prev_row, next_row, w_ref, is_start, is_end):
    tm = center.shape[0]
    r = lax.broadcasted_iota(jnp.int32, (tm, 1), 0)
    up = jnp.where(r == 0, prev_row, pltpu.roll(center, 1, 0))
    dn = jnp.where(r == tm - 1, next_row, pltpu.roll(center, tm - 1, 0))
    up = jnp.where(is_start, 0.0, up)
    dn = jnp.where(is_end, 0.0, dn)
    return up * w_ref[0:1, :] + center * w_ref[1:2, :] + dn * w_ref[2:3, :]


def _out_kernel(attn_ref, four_ref, cb_ref, t_ref, tp_ref, tn_ref, gm_ref, cw_ref, w_ref, xl_ref, xc_ref, m_ref,
                o_ref, mix_ref, *, n_lat_tiles, seq, ctx_len, split_rows):
    tm = xl_ref.shape[0]
    i = pl.program_id(0)
    is_start, is_end = _seq_edge_masks(i, tm, n_lat_tiles, seq, ctx_len)
    conv = _dwconv3(t_ref[...].astype(F32), tp_ref[SUBLANES - 1:SUBLANES, :].astype(F32),
                    tn_ref[0:1, :].astype(F32), cw_ref, is_start, is_end)
    mix_ref[:, :ATTN_W] = attn_ref[...]
    mix_ref[:, ATTN_W:ATTN_W + GROUP_W] = four_ref[...]
    mix_ref[:, ATTN_W + GROUP_W:ATTN_W + 2 * GROUP_W] = (cb_ref[...].astype(F32) * conv).astype(BF16)
    mix_ref[:, ATTN_W + 2 * GROUP_W:] = gm_ref[...]
    x = _stream_rows(i, n_lat_tiles, split_rows, xl_ref, xc_ref)
    o_ref[...] = x + m_ref[2:3, :] * _dot(mix_ref[...], w_ref[...])


def _halo_specs(tm, width, n_rows):
    per = tm // SUBLANES
    last = n_rows // SUBLANES - 1
    prev = pl.BlockSpec((SUBLANES, width), lambda i, *_: (jnp.maximum(i * per - 1, 0), 0))
    nxt = pl.BlockSpec((SUBLANES, width), lambda i, *_: (jnp.minimum((i + 1) * per, last), 0))
    return prev, nxt


def _out_proj(attn, four, cb, t, gm, conv_w, w_out, rows_src, modt, layer, *, lay, n_rows):
    x_lat, x_ctx, split_rows = rows_src
    d = x_lat.shape[1]
    tm = lay["tm"]
    grp = lay["grp"]
    row = lambda w: pl.BlockSpec((tm, w), lambda i: (i, 0))
    tprev, tnext = _halo_specs(tm, GROUP_W, t.shape[0])
    kern = functools.partial(_out_kernel, n_lat_tiles=lay["n_lat_tiles"], seq=lay["seq"], ctx_len=lay["ctx_len"],
                             split_rows=split_rows)
    return pl.pallas_call(
        kern,
        grid=(n_rows // tm,),
        in_specs=[row(ATTN_W), row(GROUP_W), row(GROUP_W), row(GROUP_W), tprev, tnext, row(GROUP_W),
                  _resident(conv_w.shape), _resident_layer(w_out.shape),
                  *_stream_specs(tm, d, lay["n_lat_tiles"], split_rows),
                  pl.BlockSpec((None, MOD_ROWS, d), lambda i: (layer, 0, grp(i)))],
        out_specs=row(d),
        out_shape=jax.ShapeDtypeStruct((n_rows, d), F32),
        scratch_shapes=[pltpu.VMEM((tm, MIX_W), BF16)],
        compiler_params=_params(1),
        name="out_proj",
    )(attn, four, cb, t, t, t, gm, conv_w, w_out, x_lat, x_ctx, modt)


def _ffn_kernel(x_ref, xp_ref, xn_ref, m_ref, g2_ref, wg_ref, wu_ref, cw_ref, cb_ref, wd_ref, fg_ref,
                o_ref, h_ref, *, n_lat_tiles, seq, ctx_len, final_norm):
    tm = x_ref.shape[0]
    i = pl.program_id(0)
    j = pl.program_id(1)

    def chunk(n_blocks, first):
        rs = tm // n_blocks
        ends = [(b + 1) * rs for b in range(n_blocks - 1)] + [tm + 2 * SUBLANES]
        g = jnp.concatenate([_dot(h_ref[b * rs:e, :], wg_ref[...]) for b, e in enumerate(ends)], axis=0)
        u = jnp.concatenate([_dot(h_ref[b * rs:(b + 1) * rs, :], wu_ref[...]) for b in range(n_blocks)], axis=0)
        is_start, is_end = _seq_edge_masks(i, tm, n_lat_tiles, seq, ctx_len)
        conv = _dwconv3(g[0:tm], g[tm + 2 * SUBLANES - 1:tm + 2 * SUBLANES], g[tm:tm + 1], cw_ref,
                        is_start, is_end)
        act = (_silu(conv + cb_ref[...]) * u).astype(BF16)
        part = _dot(act, wd_ref[...])
        o_ref[...] = part if first else o_ref[...] + part

    @pl.when(j == 0)
    def _():
        gain = g2_ref[...] * (1.0 + m_ref[4:5, :])
        shift = m_ref[3:4, :]

        def norm_mod(x):
            return (_rms(x) * gain + shift).astype(BF16)

        h_ref[tm:tm + SUBLANES, :] = norm_mod(xn_ref[...])
        h_ref[tm + SUBLANES:, :] = norm_mod(xp_ref[...])
        for r0 in range(0, tm, NORM_ROWS):
            h_ref[r0:r0 + NORM_ROWS, :] = norm_mod(x_ref[r0:r0 + NORM_ROWS, :])
        chunk(FFN_FIRST_BLOCKS, first=True)

    @pl.when(j > 0)
    def _():
        chunk(1, first=False)

    @pl.when(j == pl.num_programs(1) - 1)
    def _():
        gate = m_ref[5:6, :]
        for r0 in range(0, tm, NORM_ROWS):
            rows = slice(r0, r0 + NORM_ROWS)
            y = x_ref[rows, :] + gate * o_ref[rows, :]
            if final_norm:
                y = _rms(y) * fg_ref[...]
            o_ref[rows, :] = y


def _ffn(xall, modt, layer, g2, w_up, conv_w, conv_b, w_down, final_g, *, lay, n_rows, final_norm):
    d = xall.shape[1]
    d_ff = w_down.shape[1]
    tm = lay["tm"]
    grp = lay["grp"]
    nj = d_ff // FF_CHUNK
    xprev, xnext = _halo_specs(tm, d, xall.shape[0])
    kern = functools.partial(_ffn_kernel, n_lat_tiles=lay["n_lat_tiles"], seq=lay["seq"],
                             ctx_len=lay["ctx_len"], final_norm=final_norm)
    return pl.pallas_call(
        kern,
        grid=(n_rows // tm, nj),
        in_specs=[pl.BlockSpec((tm, d), lambda i, j: (i, 0)), xprev, xnext,
                  pl.BlockSpec((None, MOD_ROWS, d), lambda i, j: (layer, 0, grp(i))),
                  pl.BlockSpec((1, d), lambda i, j: (0, 0)),
                  pl.BlockSpec((None, d, FF_CHUNK), lambda i, j: (0, 0, j)),
                  pl.BlockSpec((None, d, FF_CHUNK), lambda i, j: (0, 0, nj + j)),
                  pl.BlockSpec((SUBLANES, FF_CHUNK), lambda i, j: (0, j)),
                  pl.BlockSpec((1, FF_CHUNK), lambda i, j: (0, j)),
                  pl.BlockSpec((None, FF_CHUNK, d), lambda i, j: (0, j, 0)),
                  pl.BlockSpec((1, d), lambda i, j: (0, 0))],
        out_specs=pl.BlockSpec((tm, d), lambda i, j: (i, 0)),
        out_shape=jax.ShapeDtypeStruct((n_rows, d), F32),
        scratch_shapes=[pltpu.VMEM((tm + 2 * SUBLANES, d), BF16)],
        compiler_params=_params(2, V7X_VMEM_FFN_LIMIT_BYTES),
        name="ffn",
    )(xall, xall, xall, modt, g2, w_up, w_up, conv_w, conv_b, w_down, final_g)


def _rope_tables(seq, tm):
    pos = jnp.arange(seq, dtype=jnp.int32)
    row = (pos // GRID_W).astype(F32)
    col = (pos % GRID_W).astype(F32)
    freqs = ROPE_THETA ** (-jnp.arange(0, ROPE_AXIS_DIM, 2, dtype=F32) / ROPE_AXIS_DIM)
    ang_r = row[:, None] * freqs[None, :]
    ang_c = col[:, None] * freqs[None, :]
    cos = jnp.concatenate([jnp.cos(ang_r)] * 2 + [jnp.cos(ang_c)] * 2, axis=-1)
    sin = jnp.concatenate([-jnp.sin(ang_r), jnp.sin(ang_r), -jnp.sin(ang_c), jnp.sin(ang_c)], axis=-1)
    cos = jnp.concatenate([cos, jnp.ones((tm, HEAD_DIM), F32)], axis=0)
    sin = jnp.concatenate([sin, jnp.zeros((tm, HEAD_DIM), F32)], axis=0)
    return cos, sin


def _dft_matrix(n, scale):
    n1 = 1
    while n1 * n1 < n:
        n1 *= 2
    n2 = n // n1
    t = jnp.arange(n, dtype=jnp.int32)[None, :]
    a = jnp.arange(n1, dtype=jnp.int32)[:, None]
    b = jnp.arange(n2, dtype=jnp.int32)[:, None]
    ang_a = ((a * t) % n1).astype(F32) * (2.0 * math.pi / n1)
    ang_b = ((b * t) % n).astype(F32) * (2.0 * math.pi / n)
    ca, sa = jnp.cos(ang_a)[:, None, :], jnp.sin(ang_a)[:, None, :]
    cb, sb = jnp.cos(ang_b)[None, :, :] * scale, jnp.sin(ang_b)[None, :, :] * scale
    c = (ca * cb - sa * sb).reshape(n, n)
    s = (sa * cb + ca * sb).reshape(n, n)
    return c.astype(BF16), (-s).astype(BF16)


def _channel_dft(scale):
    k = jnp.arange(HEAD_DIM, dtype=jnp.int32)
    ang = ((k[:, None] * k[None, :]) % HEAD_DIM).astype(F32) * (2.0 * math.pi / HEAD_DIM)
    return (jnp.concatenate([jnp.cos(ang), jnp.sin(ang)], axis=1) * scale).astype(BF16)


def _layout(batch, seq, ctx_len, max_tm):
    n_lat, n_ctx = batch * seq, batch * ctx_len
    tm = max_tm
    while n_ctx % tm or seq % tm:
        tm //= 2
    assert tm >= CHUNK and seq & (seq - 1) == 0 and ctx_len & (ctx_len - 1) == 0
    assert seq % GRID_W == 0 and ctx_len % CHUNK == 0 and n_lat % ctx_len == 0 and n_lat % seq == 0
    n_lat_tiles, tps = n_lat // tm, seq // tm
    grp = lambda i: jnp.where(i < n_lat_tiles, 1 + i // tps, 0)
    return dict(tm=tm, n_lat_tiles=n_lat_tiles, tiles_per_seq=tps, grp=grp, seq=seq, ctx_len=ctx_len,
                n_lat=n_lat, n_ctx=n_ctx)


def kernel(x, c, ctx, c_ctx, w_mod, b_mod, norm1_g, norm2_g, w_in, q_norm_g, k_norm_g, conv_w, gm_ln_g,
           gm_ln_b, gm_ws, gm_b, w_out, w_up, ffn_conv_w, ffn_conv_b, w_down, final_norm_g):
    batch, seq, d = x.shape
    ctx_len = ctx.shape[1]
    depth = w_mod.shape[0]
    d_ff = w_down.shape[1]
    assert batch + 1 <= MOD_ROWS and w_in.shape[2] == IN_W and w_down.shape[1] % FF_CHUNK == 0
    lay = _layout(batch, seq, ctx_len, ROW_TILE)
    lay_ffn = _layout(batch, seq, ctx_len, FFN_ROW_TILE)
    n_lat, tm = lay["n_lat"], lay["tm"]

    rows_src = (x.reshape(n_lat, d), ctx.reshape(batch * ctx_len, d), True)

    cs = jnp.concatenate([c_ctx[None, :], c, jnp.zeros((MOD_ROWS - 1 - batch, d), F32)], axis=0)
    mod = _modulation(cs, w_mod, b_mod)
    modt = mod.reshape(depth, MOD_ROWS, N_MOD, d).transpose(0, 2, 1, 3).reshape(depth, N_MOD, MOD_ROWS * d)
    modt = jnp.pad(modt, ((0, 0), (0, MOD_ROWS - N_MOD), (0, 0)))

    cos_t, sin_t = _rope_tables(seq, tm)
    f_lat = _dft_matrix(seq, seq ** -0.5)
    f_ctx = _dft_matrix(ctx_len, ctx_len ** -0.5)
    dftc = _channel_dft(HEAD_DIM ** -0.5)

    pad_rows = lambda w: jnp.pad(w, ((0, 0), (0, SUBLANES - w.shape[1]), (0, 0)))
    conv_w8 = pad_rows(conv_w)
    ffn_conv_w8 = pad_rows(ffn_conv_w)
    bsf = jnp.broadcast_to(gm_b[..., None], gm_b.shape + (HEAD_DIM,))
    gm_ws_b = gm_ws.astype(BF16)
    fg = final_norm_g.reshape(1, d)

    big_weights = (w_in, w_out, w_up, w_down)
    w_in_b = w_in[:1].astype(BF16)

    for l in range(depth):
        last = l == depth - 1
        qt, k, vt, ab, cb, t, gm = _in_proj(
            rows_src, modt, l, norm1_g[l].reshape(1, d), w_in_b, q_norm_g[l].reshape(1, HEAD_DIM),
            k_norm_g[l].reshape(1, HEAD_DIM), cos_t, sin_t, dftc, gm_ln_g[l].reshape(1, GROUP_W),
            gm_ln_b[l].reshape(1, GROUP_W), gm_ws_b[l], bsf[l], lay=lay)
        four = _pos_dft(f_lat, f_ctx, ab, batch=batch, seq=seq, ctx_len=ctx_len, with_ctx=not last)
        cast_jobs = [(w, 0) for w in big_weights[1:]] if l == 0 else []
        cast_jobs += [] if last else [(w, l + 1) for w in big_weights]
        attn, casts = _attention(qt, k, vt, q_norm_g[l], k_norm_g[l], cast_jobs, batch=batch, seq=seq,
                                 ctx_len=ctx_len, with_ctx_queries=not last)
        if l == 0:
            w_out_b, w_up_b, w_down_b = (w[None] for w in casts[:3])
        next_weights = casts[-len(big_weights):]
        n_rows = n_lat if last else n_lat + lay["n_ctx"]
        xall = _out_proj(attn, four, cb, t, gm, conv_w8[l], w_out_b, rows_src, modt, l, lay=lay, n_rows=n_rows)
        xall = _ffn(xall, modt, l, norm2_g[l].reshape(1, d), w_up_b, ffn_conv_w8[l],
                    ffn_conv_b[l].reshape(1, d_ff), w_down_b, fg, lay=lay_ffn, n_rows=n_rows, final_norm=last)
        rows_src = (xall, xall, False)
        if not last:
            w_in_b, w_out_b, w_up_b, w_down_b = (w[None] for w in next_weights)
    return xall.reshape(batch, seq, d)
```

```python
import functools
import math

import jax
import jax.numpy as jnp
from jax import lax
from jax.experimental import pallas as pl
from jax.experimental.pallas import tpu as pltpu

F32 = jnp.float32
BF16 = jnp.bfloat16

GRID_W = 64
HEAD_DIM = 128
N_HEADS = 8
N_KV_HEADS = 2
Q_PER_KV = N_HEADS // N_KV_HEADS
ATTN_W = N_HEADS * HEAD_DIM
KV_W = N_KV_HEADS * HEAD_DIM
ROPE_THETA = 10000.0
ROPE_AXIS_DIM = HEAD_DIM // 2
ATTN_SCALE = HEAD_DIM ** -0.5
Q_SCALE = ATTN_SCALE * math.log2(math.e)
GROUP_W = 4 * HEAD_DIM
CHUNK = 128
N_MOD = 6
EPS = 1e-6

OFF_K = ATTN_W
OFF_V = OFF_K + KV_W
OFF_F = OFF_V + KV_W
OFF_CB = OFF_F + GROUP_W
OFF_CC = OFF_CB + GROUP_W
OFF_CH = OFF_CC + GROUP_W
OFF_GU = OFF_CH + GROUP_W
OFF_GV = OFF_GU + GROUP_W
IN_W = OFF_GV + GROUP_W
MIX_W = ATTN_W + 3 * GROUP_W

V7X_VMEM_LIMIT_BYTES = 56 * 1024 * 1024
V7X_VMEM_FFN_LIMIT_BYTES = 61 * 1024 * 1024
SUBLANES = 8
MOD_ROWS = 8
ROW_TILE = 512
FFN_ROW_TILE = 1024
FF_CHUNK = 512
FFN_FIRST_BLOCKS = 4
KV_CHUNK = 1024
CAST_STRIP = 128
BF16_SUBLANES = 16
NORM_ROWS = BF16_SUBLANES
VT_ROWS = HEAD_DIM + BF16_SUBLANES
MAX_FIXED_SHIFT = 60.0
DFT_ROWS = 1024


def _params(n_axes, vmem_limit_bytes=V7X_VMEM_LIMIT_BYTES):
    return pltpu.CompilerParams(dimension_semantics=("arbitrary",) * n_axes, vmem_limit_bytes=vmem_limit_bytes)


def _resident(shape):
    return pl.BlockSpec(shape, lambda *_: (0,) * len(shape), pipeline_mode=pl.Buffered(1))


def _resident_layer(stacked_shape):
    rest = tuple(stacked_shape[1:])
    assert stacked_shape[0] == 1
    return pl.BlockSpec((None,) + rest, lambda *_: (0,) * (1 + len(rest)), pipeline_mode=pl.Buffered(1))


def _stream_specs(tm, d, n_lat_tiles, split):
    if not split:
        return pl.BlockSpec((tm, d), lambda i, *_: (i, 0)), pl.BlockSpec((SUBLANES, d), lambda i, *_: (0, 0))
    lat = pl.BlockSpec((tm, d), lambda i, *_: (jnp.minimum(i, n_lat_tiles - 1), 0))
    ctx = pl.BlockSpec((tm, d), lambda i, *_: (jnp.maximum(i - n_lat_tiles, 0), 0))
    return lat, ctx


def _stream_rows(tile, n_lat_tiles, split, rows_ref, ctx_ref):
    return jnp.where(tile < n_lat_tiles, rows_ref[...], ctx_ref[...]) if split else rows_ref[...]


def _dot(a, b):
    return jnp.dot(a, b, preferred_element_type=F32)


def _gelu_tanh(x):
    return 0.5 * x * (1.0 + jnp.tanh(math.sqrt(2.0 / math.pi) * (x + 0.044715 * (x * x * x))))


def _silu(x):
    return x * (1.0 / (1.0 + jnp.exp(-x)))


def _rms(x, eps=EPS):
    return x * lax.rsqrt(jnp.mean(x * x, axis=-1, keepdims=True) + eps)


def _split_bf16(x):
    hi = x.astype(BF16)
    return hi, (x - hi.astype(F32)).astype(BF16)


def _mod_kernel(c_ref, w_ref, b_ref, o_ref):
    s_hi, s_lo = _split_bf16(_silu(c_ref[...]))
    w_hi, w_lo = _split_bf16(w_ref[...])
    r = _dot(jnp.concatenate([s_hi, s_lo], axis=0), w_hi)
    o_ref[...] = r[:MOD_ROWS] + r[MOD_ROWS:] + _dot(s_hi, w_lo) + b_ref[...]


def _modulation(cs, w_mod, b_mod):
    depth, d, n = w_mod.shape
    tn = next(t for t in (2048, 1024, 512, 256, 128) if n % t == 0)
    return pl.pallas_call(
        _mod_kernel,
        grid=(depth, n // tn),
        in_specs=[pl.BlockSpec((MOD_ROWS, d), lambda l, j: (0, 0)),
                  pl.BlockSpec((None, d, tn), lambda l, j: (l, 0, j)),
                  pl.BlockSpec((None, 1, tn), lambda l, j: (l, 0, j))],
        out_specs=pl.BlockSpec((None, MOD_ROWS, tn), lambda l, j: (l, 0, j)),
        out_shape=jax.ShapeDtypeStruct((depth, MOD_ROWS, n), F32),
        compiler_params=_params(2),
        name="modulation",
    )(cs, w_mod, b_mod.reshape(depth, 1, n))


def _in_kernel(xl_ref, xc_ref, m_ref, g1_ref, w_ref, qg_ref, kg_ref, cos_ref, sin_ref, dftc_ref,
               lng_ref, lnb_ref, ws_ref, bs_ref,
               qt_ref, k_ref, vt_ref, ab_ref, cb_ref, t_ref, gm_ref, *, n_lat_tiles, split_rows):
    tm = xl_ref.shape[0]
    x = _stream_rows(pl.program_id(0), n_lat_tiles, split_rows, xl_ref, xc_ref)
    h = _rms(x) * g1_ref[...]
    h = h * (1.0 + m_ref[1:2, :]) + m_ref[0:1, :]
    hb = h.astype(BF16)

    def proj(lo, width=GROUP_W):
        return _dot(hb, w_ref[:, lo:lo + width])

    cos = cos_ref[...]
    sin = sin_ref[...]
    lane = lax.broadcasted_iota(jnp.int32, (tm, HEAD_DIM), 1)
    first_half = (lane % (ROPE_AXIS_DIM)) < (ROPE_AXIS_DIM // 2)

    def norm_rope(ph, gain):
        y = _rms(ph) * gain
        partner = jnp.where(first_half,
                            pltpu.roll(y, HEAD_DIM - ROPE_AXIS_DIM // 2, 1),
                            pltpu.roll(y, ROPE_AXIS_DIM // 2, 1))
        return y * cos + partner * sin

    def queries():
        qg = qg_ref[...] * Q_SCALE
        for half in range(ATTN_W // GROUP_W):
            p = proj(half * GROUP_W)
            for hh in range(GROUP_W // HEAD_DIM):
                c0 = half * GROUP_W + hh * HEAD_DIM
                y = norm_rope(p[:, hh * HEAD_DIM:(hh + 1) * HEAD_DIM], qg)
                qt_ref[c0:c0 + HEAD_DIM, :] = y.T.astype(BF16)

    def keys_values():
        p = proj(OFF_K)
        kg = kg_ref[...]
        for hh in range(N_KV_HEADS):
            cols = slice(hh * HEAD_DIM, (hh + 1) * HEAD_DIM)
            k_ref[:, cols] = norm_rope(p[:, cols], kg).astype(BF16)
            r0 = hh * VT_ROWS
            vt_ref[r0:r0 + HEAD_DIM, :] = p[:, KV_W + hh * HEAD_DIM:KV_W + (hh + 1) * HEAD_DIM].T.astype(BF16)
            vt_ref[r0 + HEAD_DIM:r0 + VT_ROWS, :] = jnp.ones((BF16_SUBLANES, tm), BF16)

    def fourier():
        p = proj(OFF_F).astype(BF16)
        dftc = dftc_ref[...]
        for g in range(GROUP_W // HEAD_DIM):
            r = _dot(p[:, g * HEAD_DIM:(g + 1) * HEAD_DIM], dftc)
            ab_ref[:, g * HEAD_DIM:(g + 1) * HEAD_DIM] = r[:, :HEAD_DIM].astype(BF16)
            ab_ref[:, GROUP_W + g * HEAD_DIM:GROUP_W + (g + 1) * HEAD_DIM] = r[:, HEAD_DIM:].astype(BF16)

    def conv_gates():
        t_ref[...] = (proj(OFF_CC) * proj(OFF_CH)).astype(BF16)
        cb_ref[...] = proj(OFF_CB).astype(BF16)

    def spatial_gating():
        u = _gelu_tanh(proj(OFF_GU))
        gv = _gelu_tanh(proj(OFF_GV))
        gc = gv - jnp.mean(gv, axis=-1, keepdims=True)
        vn = gc * lax.rsqrt(jnp.mean(gc * gc, axis=-1, keepdims=True) + EPS) * lng_ref[...] + lnb_ref[...]
        vn = vn.astype(BF16)
        for g in range(GROUP_W // HEAD_DIM):
            wsg = ws_ref[g]
            bsg = bs_ref[g]
            for c in range(tm // CHUNK):
                rows = slice(c * CHUNK, (c + 1) * CHUNK)
                cols = slice(g * HEAD_DIM, (g + 1) * HEAD_DIM)
                s = _dot(wsg, vn[rows, cols]) + bsg
                gm_ref[rows, cols] = (u[rows, cols] * s).astype(BF16)

    spatial_gating()
    queries()
    keys_values()
    fourier()
    conv_gates()


def _in_proj(rows_src, modt, layer, g1, w_in, qg, kg, cos_t, sin_t, dftc, lng, lnb, ws, bsf, *, lay):
    x_lat, x_ctx, split_rows = rows_src
    r_rows, d = lay["n_lat"] + lay["n_ctx"], x_lat.shape[1]
    tm = lay["tm"]
    grp, nlt, tps = lay["grp"], lay["n_lat_tiles"], lay["tiles_per_seq"]
    row = lambda w: pl.BlockSpec((tm, w), lambda i: (i, 0))
    vec = lambda w: pl.BlockSpec((1, w), lambda i: (0, 0))
    tab = pl.BlockSpec((tm, HEAD_DIM), lambda i: (jnp.where(i < nlt, i % tps, tps), 0))
    col = lambda h: pl.BlockSpec((h, tm), lambda i: (0, i))
    rows_bf16 = lambda w: jax.ShapeDtypeStruct((r_rows, w), BF16)
    cols_bf16 = lambda h: jax.ShapeDtypeStruct((h, r_rows), BF16)
    return pl.pallas_call(
        functools.partial(_in_kernel, n_lat_tiles=nlt, split_rows=split_rows),
        grid=(r_rows // tm,),
        in_specs=[*_stream_specs(tm, d, nlt, split_rows),
                  pl.BlockSpec((None, MOD_ROWS, d), lambda i: (layer, 0, grp(i))),
                  vec(d), _resident_layer(w_in.shape), vec(HEAD_DIM), vec(HEAD_DIM), tab, tab,
                  _resident(dftc.shape), vec(GROUP_W), vec(GROUP_W),
                  _resident(ws.shape), _resident(bsf.shape)],
        out_specs=[col(ATTN_W), row(KV_W), col(N_KV_HEADS * VT_ROWS), row(2 * GROUP_W), row(GROUP_W), row(GROUP_W),
                   row(GROUP_W)],
        out_shape=[cols_bf16(ATTN_W), rows_bf16(KV_W), cols_bf16(N_KV_HEADS * VT_ROWS), rows_bf16(2 * GROUP_W),
                   rows_bf16(GROUP_W), rows_bf16(GROUP_W), rows_bf16(GROUP_W)],
        compiler_params=_params(1),
        name="in_proj",
    )(x_lat, x_ctx, modt, g1, w_in, qg, kg, cos_t, sin_t, dftc, lng, lnb, ws, bsf)


def _dft_kernel(fc_ref, fs_ref, ab_ref, cc_ref, cs_ref, abc_ref, o_ref, *, n_lat_steps):
    def dft(c, s, a, b):
        return (_dot(c, a) + _dot(s, b)).astype(BF16)

    @pl.when(pl.program_id(0) < n_lat_steps)
    def _():
        o_ref[...] = dft(fc_ref[...], fs_ref[...], ab_ref[:, :GROUP_W], ab_ref[:, GROUP_W:])

    @pl.when(pl.program_id(0) >= n_lat_steps)
    def _():
        n_ctx = cc_ref.shape[0]
        for r0 in range(0, o_ref.shape[0], n_ctx):
            rows = slice(r0, r0 + n_ctx)
            o_ref[rows, :] = dft(cc_ref[...], cs_ref[...], abc_ref[rows, :GROUP_W], abc_ref[rows, GROUP_W:])


def _pos_dft(f_lat, f_ctx, ab, *, batch, seq, ctx_len, with_ctx):
    tmd = min(DFT_ROWS, seq)
    while (batch * ctx_len) % tmd:
        tmd //= 2
    assert tmd % ctx_len == 0 and seq % tmd == 0
    per_seq = seq // tmd
    n_lat_steps = batch * per_seq
    n_ctx_steps = batch * ctx_len // tmd if with_ctx else 0
    lat = lambda s: s < n_lat_steps
    fspec = pl.BlockSpec((tmd, seq), lambda s: (jnp.where(lat(s), s % per_seq, 0), 0))
    return pl.pallas_call(
        functools.partial(_dft_kernel, n_lat_steps=n_lat_steps),
        grid=(n_lat_steps + n_ctx_steps,),
        in_specs=[fspec, fspec,
                  pl.BlockSpec((seq, 2 * GROUP_W), lambda s: (jnp.where(lat(s), s // per_seq, 0), 0)),
                  _resident(f_ctx[0].shape), _resident(f_ctx[1].shape),
                  pl.BlockSpec((tmd, 2 * GROUP_W), lambda s: (jnp.where(lat(s), n_lat_steps, s), 0))],
        out_specs=pl.BlockSpec((tmd, GROUP_W), lambda s: (s, 0)),
        out_shape=jax.ShapeDtypeStruct(((n_lat_steps + n_ctx_steps) * tmd, GROUP_W), BF16),
        compiler_params=_params(1),
        name="pos_dft",
    )(*f_lat, ab, *f_ctx, ab)


def _attn_kernel(shift_ref, qt_ref, kl_ref, kc_ref, vtl_ref, vtc_ref, *rest, nq, with_ctx_queries, fixed_shift,
                 cast_blocks):
    n_cast = len(cast_blocks)
    o_ref = rest[n_cast]
    step = (pl.program_id(0) * pl.num_programs(1) + pl.program_id(1)) * pl.num_programs(2) + pl.program_id(2)
    for src_ref, dst_ref, n_blocks in zip(rest[:n_cast], rest[n_cast + 1:], cast_blocks):
        @pl.when(step < n_blocks)
        def _(src_ref=src_ref, dst_ref=dst_ref):
            dst_ref[...] = src_ref[...].astype(BF16)

    tq = qt_ref.shape[1]
    seq = kl_ref.shape[0]
    cols = Q_PER_KV * tq
    qt = jnp.concatenate([qt_ref[g * HEAD_DIM:(g + 1) * HEAD_DIM, :] for g in range(Q_PER_KV)], axis=1)

    def finish(l, acc):
        o = acc * (1.0 / l)
        for g in range(Q_PER_KV):
            o_ref[:, g * HEAD_DIM:(g + 1) * HEAD_DIM] = o[:, g * tq:(g + 1) * tq].T.astype(BF16)

    def key_chunks(with_latent_keys):
        chunks = [(kc_ref, vtc_ref, slice(None))]
        if with_latent_keys:
            chunks += [(kl_ref, vtl_ref, slice(j * KV_CHUNK, (j + 1) * KV_CHUNK)) for j in range(seq // KV_CHUNK)]
        return chunks

    def run_fixed(with_latent_keys):
        shift = shift_ref[0, 0]
        acc = None
        for k_ref, vt_ref, ks in key_chunks(with_latent_keys):
            p = jnp.exp2(_dot(k_ref[ks, :], qt) - shift).astype(BF16)
            part = _dot(vt_ref[:, ks], p)
            acc = part if acc is None else acc + part
        finish(acc[HEAD_DIM:HEAD_DIM + 1], acc[:HEAD_DIM])

    def run_online(with_latent_keys):
        m = jnp.full((1, cols), -jnp.inf, F32)
        l = jnp.zeros((1, cols), F32)
        acc = jnp.zeros((HEAD_DIM, cols), F32)
        for k_ref, vt_ref, ks in key_chunks(with_latent_keys):
            s = _dot(k_ref[ks, :], qt)
            m_new = jnp.maximum(m, jnp.max(s, axis=0, keepdims=True))
            alpha = jnp.exp2(m - m_new)
            p = jnp.exp2(s - m_new)
            l = alpha * l + jnp.sum(p, axis=0, keepdims=True)
            acc = alpha * acc + _dot(vt_ref[:HEAD_DIM, ks], p.astype(BF16))
            m = m_new
        finish(l, acc)

    run = run_fixed if fixed_shift else run_online
    if not with_ctx_queries:
        run(True)
    else:
        is_latent = pl.program_id(2) < nq
        pl.when(is_latent)(lambda: run(True))
        pl.when(jnp.logical_not(is_latent))(lambda: run(False))


def _attention(qt, k, vt, q_gain, k_gain, cast_jobs, *, batch, seq, ctx_len, with_ctx_queries):
    cast_weights = [w for w, _ in cast_jobs]
    cast_layers = [layer for _, layer in cast_jobs]
    r_rows = k.shape[0] if with_ctx_queries else batch * seq
    tq = ctx_len
    nq = seq // tq
    n_lat_q = batch * nq
    ctx_blk0 = batch * seq // ctx_len
    qw = Q_PER_KV * HEAD_DIM
    q_blk = lambda b, i: jnp.where(i < nq, b * nq + i, n_lat_q + b)
    n_i = nq + (1 if with_ctx_queries else 0)
    grid = (batch, N_KV_HEADS, n_i)

    n_steps = math.prod(grid)
    widths = [next(wd for wd in range(CAST_STRIP, w.shape[2] + 1, CAST_STRIP)
                   if w.shape[2] % wd == 0 and w.shape[2] // wd <= n_steps) for w in cast_weights]
    cast_blocks = tuple(w.shape[2] // wd for w, wd in zip(cast_weights, widths))
    strip = lambda nb: lambda b, hg, i: jnp.minimum((b * N_KV_HEADS + hg) * n_i + i, nb - 1)
    cast_in = [pl.BlockSpec((None, w.shape[1], wd), lambda b, hg, i, f=strip(nb), layer=layer: (layer, 0, f(b, hg, i)))
               for w, wd, nb, layer in zip(cast_weights, widths, cast_blocks, cast_layers)]
    cast_out = [pl.BlockSpec((w.shape[1], wd), lambda b, hg, i, f=strip(nb): (0, f(b, hg, i)))
                for w, wd, nb in zip(cast_weights, widths, cast_blocks)]
    cast_shapes = [jax.ShapeDtypeStruct(w.shape[1:], BF16) for w in cast_weights]

    def call(fixed_shift, shift):
        return pl.pallas_call(
            functools.partial(_attn_kernel, nq=nq, with_ctx_queries=with_ctx_queries, fixed_shift=fixed_shift,
                              cast_blocks=cast_blocks),
            grid=grid,
            in_specs=[pl.BlockSpec(memory_space=pltpu.SMEM),
                      pl.BlockSpec((qw, tq), lambda b, hg, i: (hg, q_blk(b, i))),
                      pl.BlockSpec((seq, HEAD_DIM), lambda b, hg, i: (b, hg)),
                      pl.BlockSpec((ctx_len, HEAD_DIM), lambda b, hg, i: (ctx_blk0 + b, hg)),
                      pl.BlockSpec((VT_ROWS, seq), lambda b, hg, i: (hg, b)),
                      pl.BlockSpec((VT_ROWS, ctx_len), lambda b, hg, i: (hg, ctx_blk0 + b))] + cast_in,
            out_specs=[pl.BlockSpec((tq, qw), lambda b, hg, i: (q_blk(b, i), hg))] + cast_out,
            out_shape=[jax.ShapeDtypeStruct((r_rows, ATTN_W), BF16)] + cast_shapes,
            compiler_params=_params(3),
            name="attention_fixed" if fixed_shift else "attention_online",
        )(shift, qt, k, k, vt, vt, *cast_weights)

    bound = (HEAD_DIM * Q_SCALE * (1.0 + 2.0 ** -6)) * jnp.max(jnp.abs(q_gain)) * jnp.max(jnp.abs(k_gain))
    shift = bound.astype(F32).reshape(1, 1)
    out = lax.cond(bound <= MAX_FIXED_SHIFT, lambda: call(True, shift), lambda: call(False, shift))
    return out[0], out[1:]


def _seq_edge_masks(tile, tm, n_lat_tiles, seq, ctx_len):
    r = lax.broadcasted_iota(jnp.int32, (tm, 1), 0)
    period = jnp.where(tile < n_lat_tiles, seq, ctx_len)
    pos = (tile * tm + r) & (period - 1)
    return pos == 0, pos == period - 1


def _dwconv3(center, prev_row, next_row, w_ref, is_start, is_end):
    tm = center.shape[0]
    r = lax.broadcasted_iota(jnp.int32, (tm, 1), 0)
    up = jnp.where(r == 0, prev_row, pltpu.roll(center, 1, 0))
    dn = jnp.where(r == tm - 1, next_row, pltpu.roll(center, tm - 1, 0))
    up = jnp.where(is_start, 0.0, up)
    dn = jnp.where(is_end, 0.0, dn)
    return up * w_ref[0:1, :] + center * w_ref[1:2, :] + dn * w_ref[2:3, :]


def _out_kernel(attn_ref, four_ref, cb_ref, t_ref, tp_ref, tn_ref, gm_ref, cw_ref, w_ref, xl_ref, xc_ref, m_ref,
                o_ref, mix_ref, *, n_lat_tiles, seq, ctx_len, split_rows):
    tm = xl_ref.shape[0]
    i = pl.program_id(0)
    is_start, is_end = _seq_edge_masks(i, tm, n_lat_tiles, seq, ctx_len)
    conv = _dwconv3(t_ref[...].astype(F32), tp_ref[SUBLANES - 1:SUBLANES, :].astype(F32),
                    tn_ref[0:1, :].astype(F32), cw_ref, is_start, is_end)
    mix_ref[:, :ATTN_W] = attn_ref[...]
    mix_ref[:, ATTN_W:ATTN_W + GROUP_W] = four_ref[...]
    mix_ref[:, ATTN_W + GROUP_W:ATTN_W + 2 * GROUP_W] = (cb_ref[...].astype(F32) * conv).astype(BF16)
    mix_ref[:, ATTN_W + 2 * GROUP_W:] = gm_ref[...]
    x = _stream_rows(i, n_lat_tiles, split_rows, xl_ref, xc_ref)
    o_ref[...] = x + m_ref[2:3, :] * _dot(mix_ref[...], w_ref[...])


def _halo_specs(tm, width, n_rows):
    per = tm // SUBLANES
    last = n_rows // SUBLANES - 1
    prev = pl.BlockSpec((SUBLANES, width), lambda i, *_: (jnp.maximum(i * per - 1, 0), 0))
    nxt = pl.BlockSpec((SUBLANES, width), lambda i, *_: (jnp.minimum((i + 1) * per, last), 0))
    return prev, nxt


def _out_proj(attn, four, cb, t, gm, conv_w, w_out, rows_src, modt, layer, *, lay, n_rows):
    x_lat, x_ctx, split_rows = rows_src
    d = x_lat.shape[1]
    tm = lay["tm"]
    grp = lay["grp"]
    row = lambda w: pl.BlockSpec((tm, w), lambda i: (i, 0))
    tprev, tnext = _halo_specs(tm, GROUP_W, t.shape[0])
    kern = functools.partial(_out_kernel, n_lat_tiles=lay["n_lat_tiles"], seq=lay["seq"], ctx_len=lay["ctx_len"],
                             split_rows=split_rows)
    return pl.pallas_call(
        kern,
        grid=(n_rows // tm,),
        in_specs=[row(ATTN_W), row(GROUP_W), row(GROUP_W), row(GROUP_W), tprev, tnext, row(GROUP_W),
                  _resident(conv_w.shape), _resident_layer(w_out.shape),
                  *_stream_specs(tm, d, lay["n_lat_tiles"], split_rows),
                  pl.BlockSpec((None, MOD_ROWS, d), lambda i: (layer, 0, grp(i)))],
        out_specs=row(d),
        out_shape=jax.ShapeDtypeStruct((n_rows, d), F32),
        scratch_shapes=[pltpu.VMEM((tm, MIX_W), BF16)],
        compiler_params=_params(1),
        name="out_proj",
    )(attn, four, cb, t, t, t, gm, conv_w, w_out, x_lat, x_ctx, modt)


def _ffn_kernel(x_ref, xp_ref, xn_ref, m_ref, g2_ref, wg_ref, wu_ref, cw_ref, cb_ref, wd_ref, fg_ref,
                o_ref, h_ref, *, n_lat_tiles, seq, ctx_len, final_norm):
    tm = x_ref.shape[0]
    i = pl.program_id(0)
    j = pl.program_id(1)

    def chunk(n_blocks, first):
        rs = tm // n_blocks
        ends = [(b + 1) * rs for b in range(n_blocks - 1)] + [tm + 2 * SUBLANES]
        g = jnp.concatenate([_dot(h_ref[b * rs:e, :], wg_ref[...]) for b, e in enumerate(ends)], axis=0)
        u = jnp.concatenate([_dot(h_ref[b * rs:(b + 1) * rs, :], wu_ref[...]) for b in range(n_blocks)], axis=0)
        is_start, is_end = _seq_edge_masks(i, tm, n_lat_tiles, seq, ctx_len)
        conv = _dwconv3(g[0:tm], g[tm + 2 * SUBLANES - 1:tm + 2 * SUBLANES], g[tm:tm + 1], cw_ref,
                        is_start, is_end)
        act = (_silu(conv + cb_ref[...]) * u).astype(BF16)
        part = _dot(act, wd_ref[...])
        o_ref[...] = part if first else o_ref[...] + part

    @pl.when(j == 0)
    def _():
        gain = g2_ref[...] * (1.0 + m_ref[4:5, :])
        shift = m_ref[3:4, :]

        def norm_mod(x):
            return (_rms(x) * gain + shift).astype(BF16)

        h_ref[tm:tm + SUBLANES, :] = norm_mod(xn_ref[...])
        h_ref[tm + SUBLANES:, :] = norm_mod(xp_ref[...])
        for r0 in range(0, tm, NORM_ROWS):
            h_ref[r0:r0 + NORM_ROWS, :] = norm_mod(x_ref[r0:r0 + NORM_ROWS, :])
        chunk(FFN_FIRST_BLOCKS, first=True)

    @pl.when(j > 0)
    def _():
        chunk(1, first=False)

    @pl.when(j == pl.num_programs(1) - 1)
    def _():
        gate = m_ref[5:6, :]
        for r0 in range(0, tm, NORM_ROWS):
            rows = slice(r0, r0 + NORM_ROWS)
            y = x_ref[rows, :] + gate * o_ref[rows, :]
            if final_norm:
                y = _rms(y) * fg_ref[...]
            o_ref[rows, :] = y


def _ffn(xall, modt, layer, g2, w_up, conv_w, conv_b, w_down, final_g, *, lay, n_rows, final_norm):
    d = xall.shape[1]
    d_ff = w_down.shape[1]
    tm = lay["tm"]
    grp = lay["grp"]
    nj = d_ff // FF_CHUNK
    xprev, xnext = _halo_specs(tm, d, xall.shape[0])
    kern = functools.partial(_ffn_kernel, n_lat_tiles=lay["n_lat_tiles"], seq=lay["seq"],
                             ctx_len=lay["ctx_len"], final_norm=final_norm)
    return pl.pallas_call(
        kern,
        grid=(n_rows // tm, nj),
        in_specs=[pl.BlockSpec((tm, d), lambda i, j: (i, 0)), xprev, xnext,
                  pl.BlockSpec((None, MOD_ROWS, d), lambda i, j: (layer, 0, grp(i))),
                  pl.BlockSpec((1, d), lambda i, j: (0, 0)),
                  pl.BlockSpec((None, d, FF_CHUNK), lambda i, j: (0, 0, j)),
                  pl.BlockSpec((None, d, FF_CHUNK), lambda i, j: (0, 0, nj + j)),
                  pl.BlockSpec((SUBLANES, FF_CHUNK), lambda i, j: (0, j)),
                  pl.BlockSpec((1, FF_CHUNK), lambda i, j: (0, j)),
                  pl.BlockSpec((None, FF_CHUNK, d), lambda i, j: (0, j, 0)),
                  pl.BlockSpec((1, d), lambda i, j: (0, 0))],
        out_specs=pl.BlockSpec((tm, d), lambda i, j: (i, 0)),
        out_shape=jax.ShapeDtypeStruct((n_rows, d), F32),
        scratch_shapes=[pltpu.VMEM((tm + 2 * SUBLANES, d), BF16)],
        compiler_params=_params(2, V7X_VMEM_FFN_LIMIT_BYTES),
        name="ffn",
    )(xall, xall, xall, modt, g2, w_up, w_up, conv_w, conv_b, w_down, final_g)


def _rope_tables(seq, tm):
    pos = jnp.arange(seq, dtype=jnp.int32)
    row = (pos // GRID_W).astype(F32)
    col = (pos % GRID_W).astype(F32)
    freqs = ROPE_THETA ** (-jnp.arange(0, ROPE_AXIS_DIM, 2, dtype=F32) / ROPE_AXIS_DIM)
    ang_r = row[:, None] * freqs[None, :]
    ang_c = col[:, None] * freqs[None, :]
    cos = jnp.concatenate([jnp.cos(ang_r)] * 2 + [jnp.cos(ang_c)] * 2, axis=-1)
    sin = jnp.concatenate([-jnp.sin(ang_r), jnp.sin(ang_r), -jnp.sin(ang_c), jnp.sin(ang_c)], axis=-1)
    cos = jnp.concatenate([cos, jnp.ones((tm, HEAD_DIM), F32)], axis=0)
    sin = jnp.concatenate([sin, jnp.zeros((tm, HEAD_DIM), F32)], axis=0)
    return cos, sin


def _dft_matrix(n, scale):
    n1 = 1
    while n1 * n1 < n:
        n1 *= 2
    n2 = n // n1
    t = jnp.arange(n, dtype=jnp.int32)[None, :]
    a = jnp.arange(n1, dtype=jnp.int32)[:, None]
    b = jnp.arange(n2, dtype=jnp.int32)[:, None]
    ang_a = ((a * t) % n1).astype(F32) * (2.0 * math.pi / n1)
    ang_b = ((b * t) % n).astype(F32) * (2.0 * math.pi / n)
    ca, sa = jnp.cos(ang_a)[:, None, :], jnp.sin(ang_a)[:, None, :]
    cb, sb = jnp.cos(ang_b)[None, :, :] * scale, jnp.sin(ang_b)[None, :, :] * scale
    c = (ca * cb - sa * sb).reshape(n, n)
    neg_s = ((-sa) * cb - ca * sb).reshape(n, n)
    return c.astype(BF16), neg_s.astype(BF16)


def _channel_dft(scale):
    k = jnp.arange(HEAD_DIM, dtype=jnp.int32)
    ang = ((k[:, None] * k[None, :]) % HEAD_DIM).astype(F32) * (2.0 * math.pi / HEAD_DIM)
    return (jnp.concatenate([jnp.cos(ang), jnp.sin(ang)], axis=1) * scale).astype(BF16)


def _layout(batch, seq, ctx_len, max_tm):
    n_lat, n_ctx = batch * seq, batch * ctx_len
    tm = max_tm
    while n_ctx % tm or seq % tm:
        tm //= 2
    assert tm >= CHUNK and seq & (seq - 1) == 0 and ctx_len & (ctx_len - 1) == 0
    assert seq % GRID_W == 0 and ctx_len % CHUNK == 0 and n_lat % ctx_len == 0 and n_lat % seq == 0
    n_lat_tiles, tps = n_lat // tm, seq // tm
    grp = lambda i: jnp.where(i < n_lat_tiles, 1 + i // tps, 0)
    return dict(tm=tm, n_lat_tiles=n_lat_tiles, tiles_per_seq=tps, grp=grp, seq=seq, ctx_len=ctx_len,
                n_lat=n_lat, n_ctx=n_ctx)


def kernel(x, c, ctx, c_ctx, w_mod, b_mod, norm1_g, norm2_g, w_in, q_norm_g, k_norm_g, conv_w, gm_ln_g,
           gm_ln_b, gm_ws, gm_b, w_out, w_up, ffn_conv_w, ffn_conv_b, w_down, final_norm_g):
    batch, seq, d = x.shape
    ctx_len = ctx.shape[1]
    depth = w_mod.shape[0]
    d_ff = w_down.shape[1]
    assert batch + 1 <= MOD_ROWS and w_in.shape[2] == IN_W and w_down.shape[1] % FF_CHUNK == 0
    lay = _layout(batch, seq, ctx_len, ROW_TILE)
    lay_ffn = _layout(batch, seq, ctx_len, FFN_ROW_TILE)
    n_lat, tm = lay["n_lat"], lay["tm"]

    rows_src = (x.reshape(n_lat, d), ctx.reshape(batch * ctx_len, d), True)

    cs = jnp.concatenate([c_ctx[None, :], c, jnp.zeros((MOD_ROWS - 1 - batch, d), F32)], axis=0)
    mod = _modulation(cs, w_mod, b_mod)
    modt = mod.reshape(depth, MOD_ROWS, N_MOD, d).transpose(0, 2, 1, 3).reshape(depth, N_MOD, MOD_ROWS * d)
    modt = jnp.pad(modt, ((0, 0), (0, MOD_ROWS - N_MOD), (0, 0)))

    cos_t, sin_t = _rope_tables(seq, tm)
    f_lat = _dft_matrix(seq, seq ** -0.5)
    f_ctx = _dft_matrix(ctx_len, ctx_len ** -0.5)
    dftc = _channel_dft(HEAD_DIM ** -0.5)

    pad_rows = lambda w: jnp.pad(w, ((0, 0), (0, SUBLANES - w.shape[1]), (0, 0)))
    conv_w8 = pad_rows(conv_w)
    ffn_conv_w8 = pad_rows(ffn_conv_w)
    bsf = jnp.broadcast_to(gm_b[..., None], gm_b.shape + (HEAD_DIM,))
    gm_ws_b = gm_ws.astype(BF16)
    fg = final_norm_g.reshape(1, d)

    big_weights = (w_in, w_out, w_up, w_down)
    w_in_b = w_in[:1].astype(BF16)

    for l in range(depth):
        last = l == depth - 1
        qt, k, vt, ab, cb, t, gm = _in_proj(
            rows_src, modt, l, norm1_g[l].reshape(1, d), w_in_b, q_norm_g[l].reshape(1, HEAD_DIM),
            k_norm_g[l].reshape(1, HEAD_DIM), cos_t, sin_t, dftc, gm_ln_g[l].reshape(1, GROUP_W),
            gm_ln_b[l].reshape(1, GROUP_W), gm_ws_b[l], bsf[l], lay=lay)
        four = _pos_dft(f_lat, f_ctx, ab, batch=batch, seq=seq, ctx_len=ctx_len, with_ctx=not last)
        cast_jobs = [(w, 0) for w in big_weights[1:]] if l == 0 else []
        cast_jobs += [] if last else [(w, l + 1) for w in big_weights]
        attn, casts = _attention(qt, k, vt, q_norm_g[l], k_norm_g[l], cast_jobs, batch=batch, seq=seq,
                                 ctx_len=ctx_len, with_ctx_queries=not last)
        if l == 0:
            w_out_b, w_up_b, w_down_b = (w[None] for w in casts[:3])
        next_weights = casts[-len(big_weights):]
        n_rows = n_lat if last else n_lat + lay["n_ctx"]
        xall = _out_proj(attn, four, cb, t, gm, conv_w8[l], w_out_b, rows_src, modt, l, lay=lay, n_rows=n_rows)
        xall = _ffn(xall, modt, l, norm2_g[l].reshape(1, d), w_up_b, ffn_conv_w8[l],
                    ffn_conv_b[l].reshape(1, d_ff), w_down_b, fg, lay=lay_ffn, n_rows=n_rows, final_norm=last)
        rows_src = (xall, xall, False)
        if not last:
            w_in_b, w_out_b, w_up_b, w_down_b = (w[None] for w in next_weights)
    return xall.reshape(batch, seq, d)
```

```python
import functools
import math

import jax
import jax.numpy as jnp
from jax import lax
from jax.experimental import pallas as pl
from jax.experimental.pallas import tpu as pltpu

F32 = jnp.float32
BF16 = jnp.bfloat16

GRID_W = 64
HEAD_DIM = 128
N_HEADS = 8
N_KV_HEADS = 2
Q_PER_KV = N_HEADS // N_KV_HEADS
ATTN_W = N_HEADS * HEAD_DIM
KV_W = N_KV_HEADS * HEAD_DIM
ROPE_THETA = 10000.0
ROPE_AXIS_DIM = HEAD_DIM // 2
ATTN_SCALE = HEAD_DIM ** -0.5
Q_SCALE = ATTN_SCALE * math.log2(math.e)
GROUP_W = 4 * HEAD_DIM
CHUNK = 128
N_MOD = 6
EPS = 1e-6

OFF_K = ATTN_W
OFF_V = OFF_K + KV_W
OFF_F = OFF_V + KV_W
OFF_CB = OFF_F + GROUP_W
OFF_CC = OFF_CB + GROUP_W
OFF_CH = OFF_CC + GROUP_W
OFF_GU = OFF_CH + GROUP_W
OFF_GV = OFF_GU + GROUP_W
IN_W = OFF_GV + GROUP_W
MIX_W = ATTN_W + 3 * GROUP_W

V7X_VMEM_LIMIT_BYTES = 56 * 1024 * 1024
V7X_VMEM_FFN_LIMIT_BYTES = 61 * 1024 * 1024
SUBLANES = 8
MOD_ROWS = 8
ROW_TILE = 512
FFN_ROW_TILE = 1024
FF_CHUNK = 512
FFN_FIRST_BLOCKS = 4
KV_CHUNK = 1024
CAST_STRIP = 128
BF16_SUBLANES = 16
NORM_ROWS = BF16_SUBLANES
VT_ROWS = HEAD_DIM + BF16_SUBLANES
MAX_FIXED_SHIFT = 60.0
DFT_ROWS = 1024
DFT_FOLD_BLOCK = 256


def _params(n_axes, vmem_limit_bytes=V7X_VMEM_LIMIT_BYTES):
    return pltpu.CompilerParams(dimension_semantics=("arbitrary",) * n_axes, vmem_limit_bytes=vmem_limit_bytes)


def _resident(shape):
    return pl.BlockSpec(shape, lambda *_: (0,) * len(shape), pipeline_mode=pl.Buffered(1))


def _resident_layer(stacked_shape):
    rest = tuple(stacked_shape[1:])
    assert stacked_shape[0] == 1
    return pl.BlockSpec((None,) + rest, lambda *_: (0,) * (1 + len(rest)), pipeline_mode=pl.Buffered(1))


def _stream_specs(tm, d, n_lat_tiles, split):
    if not split:
        return pl.BlockSpec((tm, d), lambda i, *_: (i, 0)), pl.BlockSpec((SUBLANES, d), lambda i, *_: (0, 0))
    lat = pl.BlockSpec((tm, d), lambda i, *_: (jnp.minimum(i, n_lat_tiles - 1), 0))
    ctx = pl.BlockSpec((tm, d), lambda i, *_: (jnp.maximum(i - n_lat_tiles, 0), 0))
    return lat, ctx


def _stream_rows(tile, n_lat_tiles, split, rows_ref, ctx_ref):
    return jnp.where(tile < n_lat_tiles, rows_ref[...], ctx_ref[...]) if split else rows_ref[...]


def _dot(a, b):
    return jnp.dot(a, b, preferred_element_type=F32)


def _gelu_tanh(x):
    return 0.5 * x * (1.0 + jnp.tanh(math.sqrt(2.0 / math.pi) * (x + 0.044715 * (x * x * x))))


def _silu(x):
    return x * (1.0 / (1.0 + jnp.exp(-x)))


def _rms(x, eps=EPS):
    return x * lax.rsqrt(jnp.mean(x * x, axis=-1, keepdims=True) + eps)


def _split_bf16(x):
    hi = x.astype(BF16)
    return hi, (x - hi.astype(F32)).astype(BF16)


def _mod_kernel(c_ref, w_ref, b_ref, o_ref):
    s_hi, s_lo = _split_bf16(_silu(c_ref[...]))
    w_hi, w_lo = _split_bf16(w_ref[...])
    r = _dot(jnp.concatenate([s_hi, s_lo], axis=0), w_hi)
    o_ref[...] = r[:MOD_ROWS] + r[MOD_ROWS:] + _dot(s_hi, w_lo) + b_ref[...]


def _modulation(cs, w_mod, b_mod):
    depth, d, n = w_mod.shape
    tn = next(t for t in (2048, 1024, 512, 256, 128) if n % t == 0)
    return pl.pallas_call(
        _mod_kernel,
        grid=(depth, n // tn),
        in_specs=[pl.BlockSpec((MOD_ROWS, d), lambda l, j: (0, 0)),
                  pl.BlockSpec((None, d, tn), lambda l, j: (l, 0, j)),
                  pl.BlockSpec((None, 1, tn), lambda l, j: (l, 0, j))],
        out_specs=pl.BlockSpec((None, MOD_ROWS, tn), lambda l, j: (l, 0, j)),
        out_shape=jax.ShapeDtypeStruct((depth, MOD_ROWS, n), F32),
        compiler_params=_params(2),
        name="modulation",
    )(cs, w_mod, b_mod.reshape(depth, 1, n))


def _in_kernel(xl_ref, xc_ref, m_ref, g1_ref, w_ref, qg_ref, kg_ref, cos_ref, sin_ref, dftc_ref,
               lng_ref, lnb_ref, ws_ref, bs_ref,
               qt_ref, k_ref, vt_ref, ab_ref, cb_ref, t_ref, gm_ref, *, n_lat_tiles, split_rows):
    tm = xl_ref.shape[0]
    x = _stream_rows(pl.program_id(0), n_lat_tiles, split_rows, xl_ref, xc_ref)
    h = _rms(x) * g1_ref[...]
    h = h * (1.0 + m_ref[1:2, :]) + m_ref[0:1, :]
    hb = h.astype(BF16)

    def proj(lo, width=GROUP_W):
        return _dot(hb, w_ref[:, lo:lo + width])

    cos = cos_ref[...]
    sin = sin_ref[...]
    lane = lax.broadcasted_iota(jnp.int32, (tm, HEAD_DIM), 1)
    first_half = (lane % (ROPE_AXIS_DIM)) < (ROPE_AXIS_DIM // 2)

    def norm_rope(ph, gain):
        y = _rms(ph) * gain
        partner = jnp.where(first_half,
                            pltpu.roll(y, HEAD_DIM - ROPE_AXIS_DIM // 2, 1),
                            pltpu.roll(y, ROPE_AXIS_DIM // 2, 1))
        return y * cos + partner * sin

    def queries():
        qg = qg_ref[...] * Q_SCALE
        for half in range(ATTN_W // GROUP_W):
            p = proj(half * GROUP_W)
            for hh in range(GROUP_W // HEAD_DIM):
                c0 = half * GROUP_W + hh * HEAD_DIM
                y = norm_rope(p[:, hh * HEAD_DIM:(hh + 1) * HEAD_DIM], qg)
                qt_ref[c0:c0 + HEAD_DIM, :] = y.T.astype(BF16)

    def keys_values():
        p = proj(OFF_K)
        kg = kg_ref[...]
        for hh in range(N_KV_HEADS):
            cols = slice(hh * HEAD_DIM, (hh + 1) * HEAD_DIM)
            k_ref[:, cols] = norm_rope(p[:, cols], kg).astype(BF16)
            r0 = hh * VT_ROWS
            vt_ref[r0:r0 + HEAD_DIM, :] = p[:, KV_W + hh * HEAD_DIM:KV_W + (hh + 1) * HEAD_DIM].T.astype(BF16)
            vt_ref[r0 + HEAD_DIM:r0 + VT_ROWS, :] = jnp.ones((BF16_SUBLANES, tm), BF16)

    def fourier():
        p = proj(OFF_F).astype(BF16)
        dftc = dftc_ref[...]
        for g in range(GROUP_W // HEAD_DIM):
            r = _dot(p[:, g * HEAD_DIM:(g + 1) * HEAD_DIM], dftc)
            ab_ref[:, g * HEAD_DIM:(g + 1) * HEAD_DIM] = r[:, :HEAD_DIM].astype(BF16)
            ab_ref[:, GROUP_W + g * HEAD_DIM:GROUP_W + (g + 1) * HEAD_DIM] = r[:, HEAD_DIM:].astype(BF16)

    def conv_gates():
        t_ref[...] = (proj(OFF_CC) * proj(OFF_CH)).astype(BF16)
        cb_ref[...] = proj(OFF_CB).astype(BF16)

    def spatial_gating():
        u = _gelu_tanh(proj(OFF_GU))
        gv = _gelu_tanh(proj(OFF_GV))
        gc = gv - jnp.mean(gv, axis=-1, keepdims=True)
        vn = gc * lax.rsqrt(jnp.mean(gc * gc, axis=-1, keepdims=True) + EPS) * lng_ref[...] + lnb_ref[...]
        vn = vn.astype(BF16)
        for g in range(GROUP_W // HEAD_DIM):
            wsg = ws_ref[g]
            bsg = bs_ref[g]
            for c in range(tm // CHUNK):
                rows = slice(c * CHUNK, (c + 1) * CHUNK)
                cols = slice(g * HEAD_DIM, (g + 1) * HEAD_DIM)
                s = _dot(wsg, vn[rows, cols]) + bsg
                gm_ref[rows, cols] = (u[rows, cols] * s).astype(BF16)

    spatial_gating()
    queries()
    keys_values()
    fourier()
    conv_gates()


def _in_proj(rows_src, modt, layer, g1, w_in, qg, kg, cos_t, sin_t, dftc, lng, lnb, ws, bsf, *, lay):
    x_lat, x_ctx, split_rows = rows_src
    r_rows, d = lay["n_lat"] + lay["n_ctx"], x_lat.shape[1]
    tm = lay["tm"]
    grp, nlt, tps = lay["grp"], lay["n_lat_tiles"], lay["tiles_per_seq"]
    row = lambda w: pl.BlockSpec((tm, w), lambda i: (i, 0))
    vec = lambda w: pl.BlockSpec((1, w), lambda i: (0, 0))
    tab = pl.BlockSpec((tm, HEAD_DIM), lambda i: (jnp.where(i < nlt, i % tps, tps), 0))
    col = lambda h: pl.BlockSpec((h, tm), lambda i: (0, i))
    rows_bf16 = lambda w: jax.ShapeDtypeStruct((r_rows, w), BF16)
    cols_bf16 = lambda h: jax.ShapeDtypeStruct((h, r_rows), BF16)
    return pl.pallas_call(
        functools.partial(_in_kernel, n_lat_tiles=nlt, split_rows=split_rows),
        grid=(r_rows // tm,),
        in_specs=[*_stream_specs(tm, d, nlt, split_rows),
                  pl.BlockSpec((None, MOD_ROWS, d), lambda i: (layer, 0, grp(i))),
                  vec(d), _resident_layer(w_in.shape), vec(HEAD_DIM), vec(HEAD_DIM), tab, tab,
                  _resident(dftc.shape), vec(GROUP_W), vec(GROUP_W),
                  _resident(ws.shape), _resident(bsf.shape)],
        out_specs=[col(ATTN_W), row(KV_W), col(N_KV_HEADS * VT_ROWS), row(2 * GROUP_W), row(GROUP_W), row(GROUP_W),
                   row(GROUP_W)],
        out_shape=[cols_bf16(ATTN_W), rows_bf16(KV_W), cols_bf16(N_KV_HEADS * VT_ROWS), rows_bf16(2 * GROUP_W),
                   rows_bf16(GROUP_W), rows_bf16(GROUP_W), rows_bf16(GROUP_W)],
        compiler_params=_params(1),
        name="in_proj",
    )(x_lat, x_ctx, modt, g1, w_in, qg, kg, cos_t, sin_t, dftc, lng, lnb, ws, bsf)


def _dft_kernel(fc_ref, fs_ref, ab_ref, rev_ref, cc_ref, cs_ref, abc_ref, o_ref, rhs_ref, edge_ref, *,
                n_lat_steps, per_seq, scale):
    step = pl.program_id(0)
    n = ab_ref.shape[0]
    half = n // 2
    rb = rev_ref.shape[0]

    @pl.when(jnp.logical_and(step < n_lat_steps, step % per_seq == 0))
    def _():
        n_blocks = n // rb
        row0 = lax.broadcasted_iota(jnp.int32, (rb, 1), 0) == 0
        for j in range(half // rb):
            src = n_blocks - 1 - j
            wrap = ((n_blocks - j) % n_blocks) * rb
            mirrored = _dot(rev_ref[...], ab_ref[src * rb:(src + 1) * rb, :])
            first = ab_ref[wrap:wrap + BF16_SUBLANES, :][0:1, :].astype(F32)
            mirrored = jnp.where(row0, first, mirrored)
            own = ab_ref[j * rb:(j + 1) * rb, :].astype(F32)
            rhs_ref[j * rb:(j + 1) * rb, :] = (own[:, :GROUP_W] + mirrored[:, :GROUP_W]).astype(BF16)
            rhs_ref[half + j * rb:half + (j + 1) * rb, :] = (own[:, GROUP_W:] - mirrored[:, GROUP_W:]).astype(BF16)
        edge_ref[0:1, :] = ab_ref[0:BF16_SUBLANES, :GROUP_W][0:1, :].astype(F32)
        edge_ref[1:2, :] = ab_ref[half:half + BF16_SUBLANES, :GROUP_W][0:1, :].astype(F32)

    @pl.when(step < n_lat_steps)
    def _():
        tmd = o_ref.shape[0]
        k = (step % per_seq) * tmd + lax.broadcasted_iota(jnp.int32, (tmd, 1), 0)
        sign = (1 - 2 * (k & 1)).astype(F32)
        y = (_dot(fc_ref[...], rhs_ref[:half, :]) + _dot(fs_ref[...], rhs_ref[half:, :])
             + scale * (sign * edge_ref[1:2, :] - edge_ref[0:1, :]))
        o_ref[...] = y.astype(BF16)

    @pl.when(step >= n_lat_steps)
    def _():
        n_ctx = cc_ref.shape[0]
        for r0 in range(0, o_ref.shape[0], n_ctx):
            rows = slice(r0, r0 + n_ctx)
            y = _dot(cc_ref[...], abc_ref[rows, :GROUP_W]) + _dot(cs_ref[...], abc_ref[rows, GROUP_W:])
            o_ref[rows, :] = y.astype(BF16)


def _pos_dft(f_lat, rev, f_ctx, ab, *, batch, seq, ctx_len, with_ctx):
    tmd = min(DFT_ROWS, seq)
    while (batch * ctx_len) % tmd:
        tmd //= 2
    assert tmd % ctx_len == 0 and seq % tmd == 0 and (seq // 2) % rev.shape[0] == 0
    per_seq = seq // tmd
    n_lat_steps = batch * per_seq
    n_ctx_steps = batch * ctx_len // tmd if with_ctx else 0
    lat = lambda s: s < n_lat_steps
    fspec = pl.BlockSpec((tmd, seq // 2), lambda s: (jnp.where(lat(s), s % per_seq, 0), 0))
    return pl.pallas_call(
        functools.partial(_dft_kernel, n_lat_steps=n_lat_steps, per_seq=per_seq, scale=seq ** -0.5),
        grid=(n_lat_steps + n_ctx_steps,),
        in_specs=[fspec, fspec,
                  pl.BlockSpec((seq, 2 * GROUP_W), lambda s: (jnp.where(lat(s), s // per_seq, 0), 0)),
                  _resident(rev.shape), _resident(f_ctx[0].shape), _resident(f_ctx[1].shape),
                  pl.BlockSpec((tmd, 2 * GROUP_W), lambda s: (jnp.where(lat(s), n_lat_steps, s), 0))],
        out_specs=pl.BlockSpec((tmd, GROUP_W), lambda s: (s, 0)),
        out_shape=jax.ShapeDtypeStruct(((n_lat_steps + n_ctx_steps) * tmd, GROUP_W), BF16),
        scratch_shapes=[pltpu.VMEM((seq, GROUP_W), BF16), pltpu.VMEM((SUBLANES, GROUP_W), F32)],
        compiler_params=_params(1),
        name="pos_dft",
    )(*f_lat, ab, rev, *f_ctx, ab)


def _attn_kernel(shift_ref, qt_ref, kl_ref, kc_ref, vtl_ref, vtc_ref, *rest, nq, with_ctx_queries, fixed_shift,
                 cast_blocks):
    n_cast = len(cast_blocks)
    o_ref = rest[n_cast]
    step = (pl.program_id(0) * pl.num_programs(1) + pl.program_id(1)) * pl.num_programs(2) + pl.program_id(2)
    for src_ref, dst_ref, n_blocks in zip(rest[:n_cast], rest[n_cast + 1:], cast_blocks):
        @pl.when(step < n_blocks)
        def _(src_ref=src_ref, dst_ref=dst_ref):
            dst_ref[...] = src_ref[...].astype(BF16)

    tq = qt_ref.shape[1]
    seq = kl_ref.shape[0]
    cols = Q_PER_KV * tq
    qt = jnp.concatenate([qt_ref[g * HEAD_DIM:(g + 1) * HEAD_DIM, :] for g in range(Q_PER_KV)], axis=1)

    def finish(l, acc):
        o = acc * (1.0 / l)
        for g in range(Q_PER_KV):
            o_ref[:, g * HEAD_DIM:(g + 1) * HEAD_DIM] = o[:, g * tq:(g + 1) * tq].T.astype(BF16)

    def key_chunks(with_latent_keys):
        chunks = [(kc_ref, vtc_ref, slice(None))]
        if with_latent_keys:
            chunks += [(kl_ref, vtl_ref, slice(j * KV_CHUNK, (j + 1) * KV_CHUNK)) for j in range(seq // KV_CHUNK)]
        return chunks

    def run_fixed(with_latent_keys):
        shift = shift_ref[0, 0]
        acc = None
        for k_ref, vt_ref, ks in key_chunks(with_latent_keys):
            p = jnp.exp2(_dot(k_ref[ks, :], qt) - shift).astype(BF16)
            part = _dot(vt_ref[:, ks], p)
            acc = part if acc is None else acc + part
        finish(acc[HEAD_DIM:HEAD_DIM + 1], acc[:HEAD_DIM])

    def run_online(with_latent_keys):
        m = jnp.full((1, cols), -jnp.inf, F32)
        l = jnp.zeros((1, cols), F32)
        acc = jnp.zeros((HEAD_DIM, cols), F32)
        for k_ref, vt_ref, ks in key_chunks(with_latent_keys):
            s = _dot(k_ref[ks, :], qt)
            m_new = jnp.maximum(m, jnp.max(s, axis=0, keepdims=True))
            alpha = jnp.exp2(m - m_new)
            p = jnp.exp2(s - m_new)
            l = alpha * l + jnp.sum(p, axis=0, keepdims=True)
            acc = alpha * acc + _dot(vt_ref[:HEAD_DIM, ks], p.astype(BF16))
            m = m_new
        finish(l, acc)

    run = run_fixed if fixed_shift else run_online
    if not with_ctx_queries:
        run(True)
    else:
        is_latent = pl.program_id(2) < nq
        pl.when(is_latent)(lambda: run(True))
        pl.when(jnp.logical_not(is_latent))(lambda: run(False))


def _attention(qt, k, vt, q_gain, k_gain, cast_jobs, *, batch, seq, ctx_len, with_ctx_queries):
    cast_weights = [w for w, _ in cast_jobs]
    cast_layers = [layer for _, layer in cast_jobs]
    r_rows = k.shape[0] if with_ctx_queries else batch * seq
    tq = ctx_len
    nq = seq // tq
    n_lat_q = batch * nq
    ctx_blk0 = batch * seq // ctx_len
    qw = Q_PER_KV * HEAD_DIM
    q_blk = lambda b, i: jnp.where(i < nq, b * nq + i, n_lat_q + b)
    n_i = nq + (1 if with_ctx_queries else 0)
    grid = (batch, N_KV_HEADS, n_i)

    n_steps = math.prod(grid)
    widths = [next(wd for wd in range(CAST_STRIP, w.shape[2] + 1, CAST_STRIP)
                   if w.shape[2] % wd == 0 and w.shape[2] // wd <= n_steps) for w in cast_weights]
    cast_blocks = tuple(w.shape[2] // wd for w, wd in zip(cast_weights, widths))
    strip = lambda nb: lambda b, hg, i: jnp.minimum((b * N_KV_HEADS + hg) * n_i + i, nb - 1)
    cast_in = [pl.BlockSpec((None, w.shape[1], wd), lambda b, hg, i, f=strip(nb), layer=layer: (layer, 0, f(b, hg, i)))
               for w, wd, nb, layer in zip(cast_weights, widths, cast_blocks, cast_layers)]
    cast_out = [pl.BlockSpec((w.shape[1], wd), lambda b, hg, i, f=strip(nb): (0, f(b, hg, i)))
                for w, wd, nb in zip(cast_weights, widths, cast_blocks)]
    cast_shapes = [jax.ShapeDtypeStruct(w.shape[1:], BF16) for w in cast_weights]

    def call(fixed_shift, shift):
        return pl.pallas_call(
            functools.partial(_attn_kernel, nq=nq, with_ctx_queries=with_ctx_queries, fixed_shift=fixed_shift,
                              cast_blocks=cast_blocks),
            grid=grid,
            in_specs=[pl.BlockSpec(memory_space=pltpu.SMEM),
                      pl.BlockSpec((qw, tq), lambda b, hg, i: (hg, q_blk(b, i))),
                      pl.BlockSpec((seq, HEAD_DIM), lambda b, hg, i: (b, hg)),
                      pl.BlockSpec((ctx_len, HEAD_DIM), lambda b, hg, i: (ctx_blk0 + b, hg)),
                      pl.BlockSpec((VT_ROWS, seq), lambda b, hg, i: (hg, b)),
                      pl.BlockSpec((VT_ROWS, ctx_len), lambda b, hg, i: (hg, ctx_blk0 + b))] + cast_in,
            out_specs=[pl.BlockSpec((tq, qw), lambda b, hg, i: (q_blk(b, i), hg))] + cast_out,
            out_shape=[jax.ShapeDtypeStruct((r_rows, ATTN_W), BF16)] + cast_shapes,
            compiler_params=_params(3),
            name="attention_fixed" if fixed_shift else "attention_online",
        )(shift, qt, k, k, vt, vt, *cast_weights)

    bound = (HEAD_DIM * Q_SCALE * (1.0 + 2.0 ** -6)) * jnp.max(jnp.abs(q_gain)) * jnp.max(jnp.abs(k_gain))
    shift = bound.astype(F32).reshape(1, 1)
    out = lax.cond(bound <= MAX_FIXED_SHIFT, lambda: call(True, shift), lambda: call(False, shift))
    return out[0], out[1:]


def _seq_edge_masks(tile, tm, n_lat_tiles, seq, ctx_len):
    r = lax.broadcasted_iota(jnp.int32, (tm, 1), 0)
    period = jnp.where(tile < n_lat_tiles, seq, ctx_len)
    pos = (tile * tm + r) & (period - 1)
    return pos == 0, pos == period - 1


def _dwconv3(center, prev_row, next_row, w_ref, is_start, is_end):
    tm = center.shape[0]
    r = lax.broadcasted_iota(jnp.int32, (tm, 1), 0)
    up = jnp.where(r == 0, prev_row, pltpu.roll(center, 1, 0))
    dn = jnp.where(r == tm - 1, next_row, pltpu.roll(center, tm - 1, 0))
    up = jnp.where(is_start, 0.0, up)
    dn = jnp.where(is_end, 0.0, dn)
    return up * w_ref[0:1, :] + center * w_ref[1:2, :] + dn * w_ref[2:3, :]


def _out_kernel(attn_ref, four_ref, cb_ref, t_ref, tp_ref, tn_ref, gm_ref, cw_ref, w_ref, xl_ref, xc_ref, m_ref,
                o_ref, mix_ref, *, n_lat_tiles, seq, ctx_len, split_rows):
    tm = xl_ref.shape[0]
    i = pl.program_id(0)
    is_start, is_end = _seq_edge_masks(i, tm, n_lat_tiles, seq, ctx_len)
    conv = _dwconv3(t_ref[...].astype(F32), tp_ref[SUBLANES - 1:SUBLANES, :].astype(F32),
                    tn_ref[0:1, :].astype(F32), cw_ref, is_start, is_end)
    mix_ref[:, :ATTN_W] = attn_ref[...]
    mix_ref[:, ATTN_W:ATTN_W + GROUP_W] = four_ref[...]
    mix_ref[:, ATTN_W + GROUP_W:ATTN_W + 2 * GROUP_W] = (cb_ref[...].astype(F32) * conv).astype(BF16)
    mix_ref[:, ATTN_W + 2 * GROUP_W:] = gm_ref[...]
    x = _stream_rows(i, n_lat_tiles, split_rows, xl_ref, xc_ref)
    o_ref[...] = x + m_ref[2:3, :] * _dot(mix_ref[...], w_ref[...])


def _halo_specs(tm, width, n_rows):
    per = tm // SUBLANES
    last = n_rows // SUBLANES - 1
    prev = pl.BlockSpec((SUBLANES, width), lambda i, *_: (jnp.maximum(i * per - 1, 0), 0))
    nxt = pl.BlockSpec((SUBLANES, width), lambda i, *_: (jnp.minimum((i + 1) * per, last), 0))
    return prev, nxt


def _out_proj(attn, four, cb, t, gm, conv_w, w_out, rows_src, modt, layer, *, lay, n_rows):
    x_lat, x_ctx, split_rows = rows_src
    d = x_lat.shape[1]
    tm = lay["tm"]
    grp = lay["grp"]
    row = lambda w: pl.BlockSpec((tm, w), lambda i: (i, 0))
    tprev, tnext = _halo_specs(tm, GROUP_W, t.shape[0])
    kern = functools.partial(_out_kernel, n_lat_tiles=lay["n_lat_tiles"], seq=lay["seq"], ctx_len=lay["ctx_len"],
                             split_rows=split_rows)
    return pl.pallas_call(
        kern,
        grid=(n_rows // tm,),
        in_specs=[row(ATTN_W), row(GROUP_W), row(GROUP_W), row(GROUP_W), tprev, tnext, row(GROUP_W),
                  _resident(conv_w.shape), _resident_layer(w_out.shape),
                  *_stream_specs(tm, d, lay["n_lat_tiles"], split_rows),
                  pl.BlockSpec((None, MOD_ROWS, d), lambda i: (layer, 0, grp(i)))],
        out_specs=row(d),
        out_shape=jax.ShapeDtypeStruct((n_rows, d), F32),
        scratch_shapes=[pltpu.VMEM((tm, MIX_W), BF16)],
        compiler_params=_params(1),
        name="out_proj",
    )(attn, four, cb, t, t, t, gm, conv_w, w_out, x_lat, x_ctx, modt)


def _ffn_kernel(x_ref, xp_ref, xn_ref, m_ref, g2_ref, wg_ref, wu_ref, cw_ref, cb_ref, wd_ref, fg_ref,
                o_ref, h_ref, *, n_lat_tiles, seq, ctx_len, final_norm):
    tm = x_ref.shape[0]
    i = pl.program_id(0)
    j = pl.program_id(1)

    def chunk(n_blocks, first):
        rs = tm // n_blocks
        ends = [(b + 1) * rs for b in range(n_blocks - 1)] + [tm + 2 * SUBLANES]
        g = jnp.concatenate([_dot(h_ref[b * rs:e, :], wg_ref[...]) for b, e in enumerate(ends)], axis=0)
        u = jnp.concatenate([_dot(h_ref[b * rs:(b + 1) * rs, :], wu_ref[...]) for b in range(n_blocks)], axis=0)
        is_start, is_end = _seq_edge_masks(i, tm, n_lat_tiles, seq, ctx_len)
        conv = _dwconv3(g[0:tm], g[tm + 2 * SUBLANES - 1:tm + 2 * SUBLANES], g[tm:tm + 1], cw_ref,
                        is_start, is_end)
        act = (_silu(conv + cb_ref[...]) * u).astype(BF16)
        part = _dot(act, wd_ref[...])
        o_ref[...] = part if first else o_ref[...] + part

    @pl.when(j == 0)
    def _():
        gain = g2_ref[...] * (1.0 + m_ref[4:5, :])
        shift = m_ref[3:4, :]

        def norm_mod(x):
            return (_rms(x) * gain + shift).astype(BF16)

        h_ref[tm:tm + SUBLANES, :] = norm_mod(xn_ref[...])
        h_ref[tm + SUBLANES:, :] = norm_mod(xp_ref[...])
        for r0 in range(0, tm, NORM_ROWS):
            h_ref[r0:r0 + NORM_ROWS, :] = norm_mod(x_ref[r0:r0 + NORM_ROWS, :])
        chunk(FFN_FIRST_BLOCKS, first=True)

    @pl.when(j > 0)
    def _():
        chunk(1, first=False)

    @pl.when(j == pl.num_programs(1) - 1)
    def _():
        gate = m_ref[5:6, :]
        for r0 in range(0, tm, NORM_ROWS):
            rows = slice(r0, r0 + NORM_ROWS)
            y = x_ref[rows, :] + gate * o_ref[rows, :]
            if final_norm:
                y = _rms(y) * fg_ref[...]
            o_ref[rows, :] = y


def _ffn(xall, modt, layer, g2, w_up, conv_w, conv_b, w_down, final_g, *, lay, n_rows, final_norm):
    d = xall.shape[1]
    d_ff = w_down.shape[1]
    tm = lay["tm"]
    grp = lay["grp"]
    nj = d_ff // FF_CHUNK
    xprev, xnext = _halo_specs(tm, d, xall.shape[0])
    kern = functools.partial(_ffn_kernel, n_lat_tiles=lay["n_lat_tiles"], seq=lay["seq"],
                             ctx_len=lay["ctx_len"], final_norm=final_norm)
    return pl.pallas_call(
        kern,
        grid=(n_rows // tm, nj),
        in_specs=[pl.BlockSpec((tm, d), lambda i, j: (i, 0)), xprev, xnext,
                  pl.BlockSpec((None, MOD_ROWS, d), lambda i, j: (layer, 0, grp(i))),
                  pl.BlockSpec((1, d), lambda i, j: (0, 0)),
                  pl.BlockSpec((None, d, FF_CHUNK), lambda i, j: (0, 0, j)),
                  pl.BlockSpec((None, d, FF_CHUNK), lambda i, j: (0, 0, nj + j)),
                  pl.BlockSpec((SUBLANES, FF_CHUNK), lambda i, j: (0, j)),
                  pl.BlockSpec((1, FF_CHUNK), lambda i, j: (0, j)),
                  pl.BlockSpec((None, FF_CHUNK, d), lambda i, j: (0, j, 0)),
                  pl.BlockSpec((1, d), lambda i, j: (0, 0))],
        out_specs=pl.BlockSpec((tm, d), lambda i, j: (i, 0)),
        out_shape=jax.ShapeDtypeStruct((n_rows, d), F32),
        scratch_shapes=[pltpu.VMEM((tm + 2 * SUBLANES, d), BF16)],
        compiler_params=_params(2, V7X_VMEM_FFN_LIMIT_BYTES),
        name="ffn",
    )(xall, xall, xall, modt, g2, w_up, w_up, conv_w, conv_b, w_down, final_g)


def _rope_tables(seq, tm):
    pos = jnp.arange(seq, dtype=jnp.int32)
    row = (pos // GRID_W).astype(F32)
    col = (pos % GRID_W).astype(F32)
    freqs = ROPE_THETA ** (-jnp.arange(0, ROPE_AXIS_DIM, 2, dtype=F32) / ROPE_AXIS_DIM)
    ang_r = row[:, None] * freqs[None, :]
    ang_c = col[:, None] * freqs[None, :]
    cos = jnp.concatenate([jnp.cos(ang_r)] * 2 + [jnp.cos(ang_c)] * 2, axis=-1)
    sin = jnp.concatenate([-jnp.sin(ang_r), jnp.sin(ang_r), -jnp.sin(ang_c), jnp.sin(ang_c)], axis=-1)
    cos = jnp.concatenate([cos, jnp.ones((tm, HEAD_DIM), F32)], axis=0)
    sin = jnp.concatenate([sin, jnp.zeros((tm, HEAD_DIM), F32)], axis=0)
    return cos, sin


def _dft_matrix(n, scale, n_t=None):
    n_t = n if n_t is None else n_t
    n1 = 1
    while n1 * n1 < n:
        n1 *= 2
    n2 = n // n1
    t = jnp.arange(n_t, dtype=jnp.int32)[None, :]
    a = jnp.arange(n1, dtype=jnp.int32)[:, None]
    b = jnp.arange(n2, dtype=jnp.int32)[:, None]
    ang_a = ((a * t) % n1).astype(F32) * (2.0 * math.pi / n1)
    ang_b = ((b * t) % n).astype(F32) * (2.0 * math.pi / n)
    ca, sa = jnp.cos(ang_a)[:, None, :], jnp.sin(ang_a)[:, None, :]
    cb, sb = jnp.cos(ang_b)[None, :, :] * scale, jnp.sin(ang_b)[None, :, :] * scale
    c = (ca * cb - sa * sb).reshape(n, n_t)
    neg_s = ((-sa) * cb - ca * sb).reshape(n, n_t)
    return c.astype(BF16), neg_s.astype(BF16)


def _row_reversal(rb):
    r = jnp.arange(rb, dtype=jnp.int32)
    return (r[None, :] == (rb - r)[:, None]).astype(BF16)


def _channel_dft(scale):
    k = jnp.arange(HEAD_DIM, dtype=jnp.int32)
    ang = ((k[:, None] * k[None, :]) % HEAD_DIM).astype(F32) * (2.0 * math.pi / HEAD_DIM)
    return (jnp.concatenate([jnp.cos(ang), jnp.sin(ang)], axis=1) * scale).astype(BF16)


def _layout(batch, seq, ctx_len, max_tm):
    n_lat, n_ctx = batch * seq, batch * ctx_len
    tm = max_tm
    while n_ctx % tm or seq % tm:
        tm //= 2
    assert tm >= CHUNK and seq & (seq - 1) == 0 and ctx_len & (ctx_len - 1) == 0
    assert seq % GRID_W == 0 and ctx_len % CHUNK == 0 and n_lat % ctx_len == 0 and n_lat % seq == 0
    n_lat_tiles, tps = n_lat // tm, seq // tm
    grp = lambda i: jnp.where(i < n_lat_tiles, 1 + i // tps, 0)
    return dict(tm=tm, n_lat_tiles=n_lat_tiles, tiles_per_seq=tps, grp=grp, seq=seq, ctx_len=ctx_len,
                n_lat=n_lat, n_ctx=n_ctx)


def kernel(x, c, ctx, c_ctx, w_mod, b_mod, norm1_g, norm2_g, w_in, q_norm_g, k_norm_g, conv_w, gm_ln_g,
           gm_ln_b, gm_ws, gm_b, w_out, w_up, ffn_conv_w, ffn_conv_b, w_down, final_norm_g):
    batch, seq, d = x.shape
    ctx_len = ctx.shape[1]
    depth = w_mod.shape[0]
    d_ff = w_down.shape[1]
    assert batch + 1 <= MOD_ROWS and w_in.shape[2] == IN_W and w_down.shape[1] % FF_CHUNK == 0
    lay = _layout(batch, seq, ctx_len, ROW_TILE)
    lay_ffn = _layout(batch, seq, ctx_len, FFN_ROW_TILE)
    n_lat, tm = lay["n_lat"], lay["tm"]

    rows_src = (x.reshape(n_lat, d), ctx.reshape(batch * ctx_len, d), True)

    cs = jnp.concatenate([c_ctx[None, :], c, jnp.zeros((MOD_ROWS - 1 - batch, d), F32)], axis=0)
    mod = _modulation(cs, w_mod, b_mod)
    modt = mod.reshape(depth, MOD_ROWS, N_MOD, d).transpose(0, 2, 1, 3).reshape(depth, N_MOD, MOD_ROWS * d)
    modt = jnp.pad(modt, ((0, 0), (0, MOD_ROWS - N_MOD), (0, 0)))

    cos_t, sin_t = _rope_tables(seq, tm)
    f_lat = _dft_matrix(seq, seq ** -0.5, seq // 2)
    f_ctx = _dft_matrix(ctx_len, ctx_len ** -0.5)
    rev = _row_reversal(DFT_FOLD_BLOCK)
    dftc = _channel_dft(HEAD_DIM ** -0.5)

    pad_rows = lambda w: jnp.pad(w, ((0, 0), (0, SUBLANES - w.shape[1]), (0, 0)))
    conv_w8 = pad_rows(conv_w)
    ffn_conv_w8 = pad_rows(ffn_conv_w)
    bsf = jnp.broadcast_to(gm_b[..., None], gm_b.shape + (HEAD_DIM,))
    gm_ws_b = gm_ws.astype(BF16)
    fg = final_norm_g.reshape(1, d)

    big_weights = (w_in, w_out, w_up, w_down)
    w_in_b = w_in[:1].astype(BF16)

    for l in range(depth):
        last = l == depth - 1
        qt, k, vt, ab, cb, t, gm = _in_proj(
            rows_src, modt, l, norm1_g[l].reshape(1, d), w_in_b, q_norm_g[l].reshape(1, HEAD_DIM),
            k_norm_g[l].reshape(1, HEAD_DIM), cos_t, sin_t, dftc, gm_ln_g[l].reshape(1, GROUP_W),
            gm_ln_b[l].reshape(1, GROUP_W), gm_ws_b[l], bsf[l], lay=lay)
        four = _pos_dft(f_lat, rev, f_ctx, ab, batch=batch, seq=seq, ctx_len=ctx_len, with_ctx=not last)
        cast_jobs = [(w, 0) for w in big_weights[1:]] if l == 0 else []
        cast_jobs += [] if last else [(w, l + 1) for w in big_weights]
        attn, casts = _attention(qt, k, vt, q_norm_g[l], k_norm_g[l], cast_jobs, batch=batch, seq=seq,
                                 ctx_len=ctx_len, with_ctx_queries=not last)
        if l == 0:
            w_out_b, w_up_b, w_down_b = (w[None] for w in casts[:3])
        next_weights = casts[-len(big_weights):]
        n_rows = n_lat if last else n_lat + lay["n_ctx"]
        xall = _out_proj(attn, four, cb, t, gm, conv_w8[l], w_out_b, rows_src, modt, l, lay=lay, n_rows=n_rows)
        xall = _ffn(xall, modt, l, norm2_g[l].reshape(1, d), w_up_b, ffn_conv_w8[l],
                    ffn_conv_b[l].reshape(1, d_ff), w_down_b, fg, lay=lay_ffn, n_rows=n_rows, final_norm=last)
        rows_src = (xall, xall, False)
        if not last:
            w_in_b, w_out_b, w_up_b, w_down_b = (w[None] for w in next_weights)
    return xall.reshape(batch, seq, d)
```

```python
import functools
import math

import jax
import jax.numpy as jnp
from jax import lax
from jax.experimental import pallas as pl
from jax.experimental.pallas import tpu as pltpu

F32 = jnp.float32
BF16 = jnp.bfloat16

GRID_W = 64
HEAD_DIM = 128
N_HEADS = 8
N_KV_HEADS = 2
Q_PER_KV = N_HEADS // N_KV_HEADS
ATTN_W = N_HEADS * HEAD_DIM
KV_W = N_KV_HEADS * HEAD_DIM
ROPE_THETA = 10000.0
ROPE_AXIS_DIM = HEAD_DIM // 2
ATTN_SCALE = HEAD_DIM ** -0.5
Q_SCALE = ATTN_SCALE * math.log2(math.e)
GROUP_W = 4 * HEAD_DIM
CHUNK = 128
N_MOD = 6
EPS = 1e-6

OFF_K = ATTN_W
OFF_V = OFF_K + KV_W
OFF_F = OFF_V + KV_W
OFF_CB = OFF_F + GROUP_W
OFF_CC = OFF_CB + GROUP_W
OFF_CH = OFF_CC + GROUP_W
OFF_GU = OFF_CH + GROUP_W
OFF_GV = OFF_GU + GROUP_W
IN_W = OFF_GV + GROUP_W
MIX_W = ATTN_W + 3 * GROUP_W

V7X_VMEM_LIMIT_BYTES = 56 * 1024 * 1024
V7X_VMEM_FFN_LIMIT_BYTES = 61 * 1024 * 1024
SUBLANES = 8
MOD_ROWS = 8
ROW_TILE = 512
FFN_ROW_TILE = 1024
FF_CHUNK = 512
FFN_FIRST_BLOCKS = 4
KV_CHUNK = 1024
CAST_STRIP = 128
BF16_SUBLANES = 16
NORM_ROWS = BF16_SUBLANES
VT_ROWS = HEAD_DIM + BF16_SUBLANES
MAX_FIXED_SHIFT = 60.0
DFT_ROWS = 1024
DFT_FOLD_BLOCK = 256


def _params(n_axes, vmem_limit_bytes=V7X_VMEM_LIMIT_BYTES):
    return pltpu.CompilerParams(dimension_semantics=("arbitrary",) * n_axes, vmem_limit_bytes=vmem_limit_bytes)


def _resident(shape):
    return pl.BlockSpec(shape, lambda *_: (0,) * len(shape), pipeline_mode=pl.Buffered(1))


def _resident_layer(stacked_shape):
    rest = tuple(stacked_shape[1:])
    assert stacked_shape[0] == 1
    return pl.BlockSpec((None,) + rest, lambda *_: (0,) * (1 + len(rest)), pipeline_mode=pl.Buffered(1))


def _stream_specs(tm, d, n_lat_tiles, split):
    if not split:
        return pl.BlockSpec((tm, d), lambda i, *_: (i, 0)), pl.BlockSpec((SUBLANES, d), lambda i, *_: (0, 0))
    lat = pl.BlockSpec((tm, d), lambda i, *_: (jnp.minimum(i, n_lat_tiles - 1), 0))
    ctx = pl.BlockSpec((tm, d), lambda i, *_: (jnp.maximum(i - n_lat_tiles, 0), 0))
    return lat, ctx


def _stream_rows(tile, n_lat_tiles, split, rows_ref, ctx_ref):
    return jnp.where(tile < n_lat_tiles, rows_ref[...], ctx_ref[...]) if split else rows_ref[...]


def _dot(a, b):
    return jnp.dot(a, b, preferred_element_type=F32)


def _gelu_tanh(x):
    return 0.5 * x * (1.0 + jnp.tanh(math.sqrt(2.0 / math.pi) * (x + 0.044715 * (x * x * x))))


def _silu(x):
    return x * (1.0 / (1.0 + jnp.exp(-x)))


def _rms(x, eps=EPS):
    return x * lax.rsqrt(jnp.mean(x * x, axis=-1, keepdims=True) + eps)


def _split_bf16(x):
    hi = x.astype(BF16)
    return hi, (x - hi.astype(F32)).astype(BF16)


def _mod_kernel(c_ref, w_ref, b_ref, o_ref):
    s_hi, s_lo = _split_bf16(_silu(c_ref[...]))
    w_hi, w_lo = _split_bf16(w_ref[...])
    r = _dot(jnp.concatenate([s_hi, s_lo], axis=0), w_hi)
    o_ref[...] = r[:MOD_ROWS] + r[MOD_ROWS:] + _dot(s_hi, w_lo) + b_ref[...]


def _modulation(cs, w_mod, b_mod):
    depth, d, n = w_mod.shape
    tn = next(t for t in (2048, 1024, 512, 256, 128) if n % t == 0)
    return pl.pallas_call(
        _mod_kernel,
        grid=(depth, n // tn),
        in_specs=[pl.BlockSpec((MOD_ROWS, d), lambda l, j: (0, 0)),
                  pl.BlockSpec((None, d, tn), lambda l, j: (l, 0, j)),
                  pl.BlockSpec((None, 1, tn), lambda l, j: (l, 0, j))],
        out_specs=pl.BlockSpec((None, MOD_ROWS, tn), lambda l, j: (l, 0, j)),
        out_shape=jax.ShapeDtypeStruct((depth, MOD_ROWS, n), F32),
        compiler_params=_params(2),
        name="modulation",
    )(cs, w_mod, b_mod.reshape(depth, 1, n))


def _in_kernel(xl_ref, xc_ref, m_ref, g1_ref, w_ref, qg_ref, kg_ref, cos_ref, sin_ref, dftc_ref,
               lng_ref, lnb_ref, ws_ref, bs_ref,
               qt_ref, k_ref, vt_ref, ab_ref, cb_ref, t_ref, gm_ref, *, n_lat_tiles, split_rows):
    tm = xl_ref.shape[0]
    x = _stream_rows(pl.program_id(0), n_lat_tiles, split_rows, xl_ref, xc_ref)
    h = _rms(x) * g1_ref[...]
    h = h * (1.0 + m_ref[1:2, :]) + m_ref[0:1, :]
    hb = h.astype(BF16)

    def proj(lo, width=GROUP_W):
        return _dot(hb, w_ref[:, lo:lo + width])

    cos = cos_ref[...]
    sin = sin_ref[...]
    lane = lax.broadcasted_iota(jnp.int32, (tm, HEAD_DIM), 1)
    first_half = (lane % (ROPE_AXIS_DIM)) < (ROPE_AXIS_DIM // 2)

    def norm_rope(ph, gain):
        y = _rms(ph) * gain
        partner = jnp.where(first_half,
                            pltpu.roll(y, HEAD_DIM - ROPE_AXIS_DIM // 2, 1),
                            pltpu.roll(y, ROPE_AXIS_DIM // 2, 1))
        return y * cos + partner * sin

    def queries():
        qg = qg_ref[...] * Q_SCALE
        for half in range(ATTN_W // GROUP_W):
            p = proj(half * GROUP_W)
            for hh in range(GROUP_W // HEAD_DIM):
                c0 = half * GROUP_W + hh * HEAD_DIM
                y = norm_rope(p[:, hh * HEAD_DIM:(hh + 1) * HEAD_DIM], qg)
                qt_ref[c0:c0 + HEAD_DIM, :] = y.T.astype(BF16)

    def keys_values():
        p = proj(OFF_K)
        kg = kg_ref[...]
        for hh in range(N_KV_HEADS):
            cols = slice(hh * HEAD_DIM, (hh + 1) * HEAD_DIM)
            k_ref[:, cols] = norm_rope(p[:, cols], kg).astype(BF16)
            r0 = hh * VT_ROWS
            vt_ref[r0:r0 + HEAD_DIM, :] = p[:, KV_W + hh * HEAD_DIM:KV_W + (hh + 1) * HEAD_DIM].T.astype(BF16)
            vt_ref[r0 + HEAD_DIM:r0 + VT_ROWS, :] = jnp.ones((BF16_SUBLANES, tm), BF16)

    def fourier():
        p = proj(OFF_F).astype(BF16)
        dftc = dftc_ref[...]
        for g in range(GROUP_W // HEAD_DIM):
            r = _dot(p[:, g * HEAD_DIM:(g + 1) * HEAD_DIM], dftc)
            ab_ref[:, g * HEAD_DIM:(g + 1) * HEAD_DIM] = r[:, :HEAD_DIM].astype(BF16)
            ab_ref[:, GROUP_W + g * HEAD_DIM:GROUP_W + (g + 1) * HEAD_DIM] = r[:, HEAD_DIM:].astype(BF16)

    def conv_gates():
        t_ref[...] = (proj(OFF_CC) * proj(OFF_CH)).astype(BF16)
        cb_ref[...] = proj(OFF_CB).astype(BF16)

    def spatial_gating():
        u = _gelu_tanh(proj(OFF_GU))
        gv = _gelu_tanh(proj(OFF_GV))
        gc = gv - jnp.mean(gv, axis=-1, keepdims=True)
        vn = gc * lax.rsqrt(jnp.mean(gc * gc, axis=-1, keepdims=True) + EPS) * lng_ref[...] + lnb_ref[...]
        vn = vn.astype(BF16)
        for g in range(GROUP_W // HEAD_DIM):
            wsg = ws_ref[g]
            bsg = bs_ref[g]
            for c in range(tm // CHUNK):
                rows = slice(c * CHUNK, (c + 1) * CHUNK)
                cols = slice(g * HEAD_DIM, (g + 1) * HEAD_DIM)
                s = _dot(wsg, vn[rows, cols]) + bsg
                gm_ref[rows, cols] = (u[rows, cols] * s).astype(BF16)

    spatial_gating()
    queries()
    keys_values()
    fourier()
    conv_gates()


def _in_proj(rows_src, modt, layer, g1, w_in, qg, kg, cos_t, sin_t, dftc, lng, lnb, ws, bsf, *, lay):
    x_lat, x_ctx, split_rows = rows_src
    r_rows, d = lay["n_lat"] + lay["n_ctx"], x_lat.shape[1]
    tm = lay["tm"]
    grp, nlt, tps = lay["grp"], lay["n_lat_tiles"], lay["tiles_per_seq"]
    row = lambda w: pl.BlockSpec((tm, w), lambda i: (i, 0))
    vec = lambda w: pl.BlockSpec((1, w), lambda i: (0, 0))
    tab = pl.BlockSpec((tm, HEAD_DIM), lambda i: (jnp.where(i < nlt, i % tps, tps), 0))
    col = lambda h: pl.BlockSpec((h, tm), lambda i: (0, i))
    rows_bf16 = lambda w: jax.ShapeDtypeStruct((r_rows, w), BF16)
    cols_bf16 = lambda h: jax.ShapeDtypeStruct((h, r_rows), BF16)
    return pl.pallas_call(
        functools.partial(_in_kernel, n_lat_tiles=nlt, split_rows=split_rows),
        grid=(r_rows // tm,),
        in_specs=[*_stream_specs(tm, d, nlt, split_rows),
                  pl.BlockSpec((None, MOD_ROWS, d), lambda i: (layer, 0, grp(i))),
                  vec(d), _resident_layer(w_in.shape), vec(HEAD_DIM), vec(HEAD_DIM), tab, tab,
                  _resident(dftc.shape), vec(GROUP_W), vec(GROUP_W),
                  _resident(ws.shape), _resident(bsf.shape)],
        out_specs=[col(ATTN_W), row(KV_W), col(N_KV_HEADS * VT_ROWS), row(2 * GROUP_W), row(GROUP_W), row(GROUP_W),
                   row(GROUP_W)],
        out_shape=[cols_bf16(ATTN_W), rows_bf16(KV_W), cols_bf16(N_KV_HEADS * VT_ROWS), rows_bf16(2 * GROUP_W),
                   rows_bf16(GROUP_W), rows_bf16(GROUP_W), rows_bf16(GROUP_W)],
        compiler_params=_params(1),
        name="in_proj",
    )(x_lat, x_ctx, modt, g1, w_in, qg, kg, cos_t, sin_t, dftc, lng, lnb, ws, bsf)


def _dft_kernel(fc_ref, fs_ref, fkc_ref, fks_ref, ab_ref, rev_ref, cc_ref, cs_ref, abc_ref, o_ref, rhs_ref, y_ref,
                *, n_lat_steps, per_seq, scale):
    step = pl.program_id(0)
    n = ab_ref.shape[0]
    half = n // 2
    rb = rev_ref.shape[0]
    n_fold = half // rb

    @pl.when(jnp.logical_and(step < n_lat_steps, step % per_seq == 0))
    def _():
        n_blocks = n // rb
        rows_iota = lax.broadcasted_iota(jnp.int32, (rb, 1), 0)
        row0 = rows_iota == 0
        for j in range(n_fold):
            src = n_blocks - 1 - j
            wrap = ((n_blocks - j) % n_blocks) * rb
            mirrored = _dot(rev_ref[...], ab_ref[src * rb:(src + 1) * rb, :])
            first = ab_ref[wrap:wrap + BF16_SUBLANES, :][0:1, :].astype(F32)
            mirrored = jnp.where(row0, first, mirrored)
            own = ab_ref[j * rb:(j + 1) * rb, :].astype(F32)
            rhs_ref[j * rb:(j + 1) * rb, :] = (own[:, :GROUP_W] + mirrored[:, :GROUP_W]).astype(BF16)
            rhs_ref[half + j * rb:half + (j + 1) * rb, :] = (own[:, GROUP_W:] - mirrored[:, GROUP_W:]).astype(BF16)

        a_0 = ab_ref[0:BF16_SUBLANES, :GROUP_W][0:1, :].astype(F32)
        a_half = ab_ref[half:half + BF16_SUBLANES, :GROUP_W][0:1, :].astype(F32)
        sign = (1 - 2 * (rows_iota & 1)).astype(F32)
        edge = scale * (sign * a_half - a_0)
        starts = (_dot(fkc_ref[...], rhs_ref[:half, :]) - _dot(fks_ref[...], rhs_ref[half:, :])
                  + scale * (a_half - a_0))
        p = _dot(fc_ref[...], rhs_ref[:half, :])
        neg_q = _dot(fs_ref[...], rhs_ref[half:, :])
        for j in range(n_fold):
            rows = slice(j * rb, (j + 1) * rb)
            y_ref[rows, :] = (p[rows] + neg_q[rows] + edge).astype(BF16)
            upper = _dot(rev_ref[...], (p[rows] - neg_q[rows] + edge).astype(BF16))
            upper = jnp.where(row0, starts[j:j + 1, :], upper)
            dst = half + (n_fold - 1 - j) * rb
            y_ref[dst:dst + rb, :] = upper.astype(BF16)

    @pl.when(step < n_lat_steps)
    def _():
        tmd = o_ref.shape[0]
        o_ref[...] = y_ref[pl.ds(pl.multiple_of((step % per_seq) * tmd, tmd), tmd), :]

    @pl.when(step >= n_lat_steps)
    def _():
        n_ctx = cc_ref.shape[0]
        for r0 in range(0, o_ref.shape[0], n_ctx):
            rows = slice(r0, r0 + n_ctx)
            y = _dot(cc_ref[...], abc_ref[rows, :GROUP_W]) + _dot(cs_ref[...], abc_ref[rows, GROUP_W:])
            o_ref[rows, :] = y.astype(BF16)


def _pos_dft(f_lat, f_starts, rev, f_ctx, ab, *, batch, seq, ctx_len, with_ctx):
    tmd = min(DFT_ROWS, seq)
    while (batch * ctx_len) % tmd:
        tmd //= 2
    rb = rev.shape[0]
    assert tmd % ctx_len == 0 and seq % tmd == 0 and (seq // 2) % rb == 0 and rb % 2 == 0
    assert seq // 2 // rb <= f_starts[0].shape[0]
    per_seq = seq // tmd
    n_lat_steps = batch * per_seq
    n_ctx_steps = batch * ctx_len // tmd if with_ctx else 0
    lat = lambda s: s < n_lat_steps
    return pl.pallas_call(
        functools.partial(_dft_kernel, n_lat_steps=n_lat_steps, per_seq=per_seq, scale=seq ** -0.5),
        grid=(n_lat_steps + n_ctx_steps,),
        in_specs=[_resident(f_lat[0].shape), _resident(f_lat[1].shape),
                  _resident(f_starts[0].shape), _resident(f_starts[1].shape),
                  pl.BlockSpec((seq, 2 * GROUP_W), lambda s: (jnp.where(lat(s), s // per_seq, 0), 0)),
                  _resident(rev.shape), _resident(f_ctx[0].shape), _resident(f_ctx[1].shape),
                  pl.BlockSpec((tmd, 2 * GROUP_W), lambda s: (jnp.where(lat(s), n_lat_steps, s), 0))],
        out_specs=pl.BlockSpec((tmd, GROUP_W), lambda s: (s, 0)),
        out_shape=jax.ShapeDtypeStruct(((n_lat_steps + n_ctx_steps) * tmd, GROUP_W), BF16),
        scratch_shapes=[pltpu.VMEM((seq, GROUP_W), BF16), pltpu.VMEM((seq, GROUP_W), BF16)],
        compiler_params=_params(1),
        name="pos_dft",
    )(*f_lat, *f_starts, ab, rev, *f_ctx, ab)


def _attn_kernel(shift_ref, qt_ref, kl_ref, kc_ref, vtl_ref, vtc_ref, *rest, nq, with_ctx_queries, fixed_shift,
                 cast_blocks):
    n_cast = len(cast_blocks)
    o_ref = rest[n_cast]
    step = (pl.program_id(0) * pl.num_programs(1) + pl.program_id(1)) * pl.num_programs(2) + pl.program_id(2)
    for src_ref, dst_ref, n_blocks in zip(rest[:n_cast], rest[n_cast + 1:], cast_blocks):
        @pl.when(step < n_blocks)
        def _(src_ref=src_ref, dst_ref=dst_ref):
            dst_ref[...] = src_ref[...].astype(BF16)

    tq = qt_ref.shape[1]
    seq = kl_ref.shape[0]
    cols = Q_PER_KV * tq
    qt = jnp.concatenate([qt_ref[g * HEAD_DIM:(g + 1) * HEAD_DIM, :] for g in range(Q_PER_KV)], axis=1)

    def finish(l, acc):
        o = acc * (1.0 / l)
        for g in range(Q_PER_KV):
            o_ref[:, g * HEAD_DIM:(g + 1) * HEAD_DIM] = o[:, g * tq:(g + 1) * tq].T.astype(BF16)

    def key_chunks(with_latent_keys):
        chunks = [(kc_ref, vtc_ref, slice(None))]
        if with_latent_keys:
            chunks += [(kl_ref, vtl_ref, slice(j * KV_CHUNK, (j + 1) * KV_CHUNK)) for j in range(seq // KV_CHUNK)]
        return chunks

    def run_fixed(with_latent_keys):
        shift = shift_ref[0, 0]
        acc = None
        for k_ref, vt_ref, ks in key_chunks(with_latent_keys):
            p = jnp.exp2(_dot(k_ref[ks, :], qt) - shift).astype(BF16)
            part = _dot(vt_ref[:, ks], p)
            acc = part if acc is None else acc + part
        finish(acc[HEAD_DIM:HEAD_DIM + 1], acc[:HEAD_DIM])

    def run_online(with_latent_keys):
        m = jnp.full((1, cols), -jnp.inf, F32)
        l = jnp.zeros((1, cols), F32)
        acc = jnp.zeros((HEAD_DIM, cols), F32)
        for k_ref, vt_ref, ks in key_chunks(with_latent_keys):
            s = _dot(k_ref[ks, :], qt)
            m_new = jnp.maximum(m, jnp.max(s, axis=0, keepdims=True))
            alpha = jnp.exp2(m - m_new)
            p = jnp.exp2(s - m_new)
            l = alpha * l + jnp.sum(p, axis=0, keepdims=True)
            acc = alpha * acc + _dot(vt_ref[:HEAD_DIM, ks], p.astype(BF16))
            m = m_new
        finish(l, acc)

    run = run_fixed if fixed_shift else run_online
    if not with_ctx_queries:
        run(True)
    else:
        is_latent = pl.program_id(2) < nq
        pl.when(is_latent)(lambda: run(True))
        pl.when(jnp.logical_not(is_latent))(lambda: run(False))


def _attention(qt, k, vt, q_gain, k_gain, cast_jobs, *, batch, seq, ctx_len, with_ctx_queries):
    cast_weights = [w for w, _ in cast_jobs]
    cast_layers = [layer for _, layer in cast_jobs]
    r_rows = k.shape[0] if with_ctx_queries else batch * seq
    tq = ctx_len
    nq = seq // tq
    n_lat_q = batch * nq
    ctx_blk0 = batch * seq // ctx_len
    qw = Q_PER_KV * HEAD_DIM
    q_blk = lambda b, i: jnp.where(i < nq, b * nq + i, n_lat_q + b)
    n_i = nq + (1 if with_ctx_queries else 0)
    grid = (batch, N_KV_HEADS, n_i)

    n_steps = math.prod(grid)
    widths = [next(wd for wd in range(CAST_STRIP, w.shape[2] + 1, CAST_STRIP)
                   if w.shape[2] % wd == 0 and w.shape[2] // wd <= n_steps) for w in cast_weights]
    cast_blocks = tuple(w.shape[2] // wd for w, wd in zip(cast_weights, widths))
    strip = lambda nb: lambda b, hg, i: jnp.minimum((b * N_KV_HEADS + hg) * n_i + i, nb - 1)
    cast_in = [pl.BlockSpec((None, w.shape[1], wd), lambda b, hg, i, f=strip(nb), layer=layer: (layer, 0, f(b, hg, i)))
               for w, wd, nb, layer in zip(cast_weights, widths, cast_blocks, cast_layers)]
    cast_out = [pl.BlockSpec((w.shape[1], wd), lambda b, hg, i, f=strip(nb): (0, f(b, hg, i)))
                for w, wd, nb in zip(cast_weights, widths, cast_blocks)]
    cast_shapes = [jax.ShapeDtypeStruct(w.shape[1:], BF16) for w in cast_weights]

    def call(fixed_shift, shift):
        return pl.pallas_call(
            functools.partial(_attn_kernel, nq=nq, with_ctx_queries=with_ctx_queries, fixed_shift=fixed_shift,
                              cast_blocks=cast_blocks),
            grid=grid,
            in_specs=[pl.BlockSpec(memory_space=pltpu.SMEM),
                      pl.BlockSpec((qw, tq), lambda b, hg, i: (hg, q_blk(b, i))),
                      pl.BlockSpec((seq, HEAD_DIM), lambda b, hg, i: (b, hg)),
                      pl.BlockSpec((ctx_len, HEAD_DIM), lambda b, hg, i: (ctx_blk0 + b, hg)),
                      pl.BlockSpec((VT_ROWS, seq), lambda b, hg, i: (hg, b)),
                      pl.BlockSpec((VT_ROWS, ctx_len), lambda b, hg, i: (hg, ctx_blk0 + b))] + cast_in,
            out_specs=[pl.BlockSpec((tq, qw), lambda b, hg, i: (q_blk(b, i), hg))] + cast_out,
            out_shape=[jax.ShapeDtypeStruct((r_rows, ATTN_W), BF16)] + cast_shapes,
            compiler_params=_params(3),
            name="attention_fixed" if fixed_shift else "attention_online",
        )(shift, qt, k, k, vt, vt, *cast_weights)

    bound = (HEAD_DIM * Q_SCALE * (1.0 + 2.0 ** -6)) * jnp.max(jnp.abs(q_gain)) * jnp.max(jnp.abs(k_gain))
    shift = bound.astype(F32).reshape(1, 1)
    out = lax.cond(bound <= MAX_FIXED_SHIFT, lambda: call(True, shift), lambda: call(False, shift))
    return out[0], out[1:]


def _seq_edge_masks(tile, tm, n_lat_tiles, seq, ctx_len):
    r = lax.broadcasted_iota(jnp.int32, (tm, 1), 0)
    period = jnp.where(tile < n_lat_tiles, seq, ctx_len)
    pos = (tile * tm + r) & (period - 1)
    return pos == 0, pos == period - 1


def _dwconv3(center, prev_row, next_row, w_ref, is_start, is_end):
    tm = center.shape[0]
    r = lax.broadcasted_iota(jnp.int32, (tm, 1), 0)
    up = jnp.where(r == 0, prev_row, pltpu.roll(center, 1, 0))
    dn = jnp.where(r == tm - 1, next_row, pltpu.roll(center, tm - 1, 0))
    up = jnp.where(is_start, 0.0, up)
    dn = jnp.where(is_end, 0.0, dn)
    return up * w_ref[0:1, :] + center * w_ref[1:2, :] + dn * w_ref[2:3, :]


def _out_kernel(attn_ref, four_ref, cb_ref, t_ref, tp_ref, tn_ref, gm_ref, cw_ref, w_ref, xl_ref, xc_ref, m_ref,
                o_ref, mix_ref, *, n_lat_tiles, seq, ctx_len, split_rows):
    tm = xl_ref.shape[0]
    i = pl.program_id(0)
    is_start, is_end = _seq_edge_masks(i, tm, n_lat_tiles, seq, ctx_len)
    conv = _dwconv3(t_ref[...].astype(F32), tp_ref[SUBLANES - 1:SUBLANES, :].astype(F32),
                    tn_ref[0:1, :].astype(F32), cw_ref, is_start, is_end)
    mix_ref[:, :ATTN_W] = attn_ref[...]
    mix_ref[:, ATTN_W:ATTN_W + GROUP_W] = four_ref[...]
    mix_ref[:, ATTN_W + GROUP_W:ATTN_W + 2 * GROUP_W] = (cb_ref[...].astype(F32) * conv).astype(BF16)
    mix_ref[:, ATTN_W + 2 * GROUP_W:] = gm_ref[...]
    x = _stream_rows(i, n_lat_tiles, split_rows, xl_ref, xc_ref)
    o_ref[...] = x + m_ref[2:3, :] * _dot(mix_ref[...], w_ref[...])


def _halo_specs(tm, width, n_rows):
    per = tm // SUBLANES
    last = n_rows // SUBLANES - 1
    prev = pl.BlockSpec((SUBLANES, width), lambda i, *_: (jnp.maximum(i * per - 1, 0), 0))
    nxt = pl.BlockSpec((SUBLANES, width), lambda i, *_: (jnp.minimum((i + 1) * per, last), 0))
    return prev, nxt


def _out_proj(attn, four, cb, t, gm, conv_w, w_out, rows_src, modt, layer, *, lay, n_rows):
    x_lat, x_ctx, split_rows = rows_src
    d = x_lat.shape[1]
    tm = lay["tm"]
    grp = lay["grp"]
    row = lambda w: pl.BlockSpec((tm, w), lambda i: (i, 0))
    tprev, tnext = _halo_specs(tm, GROUP_W, t.shape[0])
    kern = functools.partial(_out_kernel, n_lat_tiles=lay["n_lat_tiles"], seq=lay["seq"], ctx_len=lay["ctx_len"],
                             split_rows=split_rows)
    return pl.pallas_call(
        kern,
        grid=(n_rows // tm,),
        in_specs=[row(ATTN_W), row(GROUP_W), row(GROUP_W), row(GROUP_W), tprev, tnext, row(GROUP_W),
                  _resident(conv_w.shape), _resident_layer(w_out.shape),
                  *_stream_specs(tm, d, lay["n_lat_tiles"], split_rows),
                  pl.BlockSpec((None, MOD_ROWS, d), lambda i: (layer, 0, grp(i)))],
        out_specs=row(d),
        out_shape=jax.ShapeDtypeStruct((n_rows, d), F32),
        scratch_shapes=[pltpu.VMEM((tm, MIX_W), BF16)],
        compiler_params=_params(1),
        name="out_proj",
    )(attn, four, cb, t, t, t, gm, conv_w, w_out, x_lat, x_ctx, modt)


def _ffn_kernel(x_ref, xp_ref, xn_ref, m_ref, g2_ref, wg_ref, wu_ref, cw_ref, cb_ref, wd_ref, fg_ref,
                o_ref, h_ref, *, n_lat_tiles, seq, ctx_len, final_norm):
    tm = x_ref.shape[0]
    i = pl.program_id(0)
    j = pl.program_id(1)

    def chunk(n_blocks, first):
        rs = tm // n_blocks
        ends = [(b + 1) * rs for b in range(n_blocks - 1)] + [tm + 2 * SUBLANES]
        g = jnp.concatenate([_dot(h_ref[b * rs:e, :], wg_ref[...]) for b, e in enumerate(ends)], axis=0)
        u = jnp.concatenate([_dot(h_ref[b * rs:(b + 1) * rs, :], wu_ref[...]) for b in range(n_blocks)], axis=0)
        is_start, is_end = _seq_edge_masks(i, tm, n_lat_tiles, seq, ctx_len)
        conv = _dwconv3(g[0:tm], g[tm + 2 * SUBLANES - 1:tm + 2 * SUBLANES], g[tm:tm + 1], cw_ref,
                        is_start, is_end)
        act = (_silu(conv + cb_ref[...]) * u).astype(BF16)
        part = _dot(act, wd_ref[...])
        o_ref[...] = part if first else o_ref[...] + part

    @pl.when(j == 0)
    def _():
        gain = g2_ref[...] * (1.0 + m_ref[4:5, :])
        shift = m_ref[3:4, :]

        def norm_mod(x):
            return (_rms(x) * gain + shift).astype(BF16)

        h_ref[tm:tm + SUBLANES, :] = norm_mod(xn_ref[...])
        h_ref[tm + SUBLANES:, :] = norm_mod(xp_ref[...])
        for r0 in range(0, tm, NORM_ROWS):
            h_ref[r0:r0 + NORM_ROWS, :] = norm_mod(x_ref[r0:r0 + NORM_ROWS, :])
        chunk(FFN_FIRST_BLOCKS, first=True)

    @pl.when(j > 0)
    def _():
        chunk(1, first=False)

    @pl.when(j == pl.num_programs(1) - 1)
    def _():
        gate = m_ref[5:6, :]
        for r0 in range(0, tm, NORM_ROWS):
            rows = slice(r0, r0 + NORM_ROWS)
            y = x_ref[rows, :] + gate * o_ref[rows, :]
            if final_norm:
                y = _rms(y) * fg_ref[...]
            o_ref[rows, :] = y


def _ffn(xall, modt, layer, g2, w_up, conv_w, conv_b, w_down, final_g, *, lay, n_rows, final_norm):
    d = xall.shape[1]
    d_ff = w_down.shape[1]
    tm = lay["tm"]
    grp = lay["grp"]
    nj = d_ff // FF_CHUNK
    xprev, xnext = _halo_specs(tm, d, xall.shape[0])
    kern = functools.partial(_ffn_kernel, n_lat_tiles=lay["n_lat_tiles"], seq=lay["seq"],
                             ctx_len=lay["ctx_len"], final_norm=final_norm)
    return pl.pallas_call(
        kern,
        grid=(n_rows // tm, nj),
        in_specs=[pl.BlockSpec((tm, d), lambda i, j: (i, 0)), xprev, xnext,
                  pl.BlockSpec((None, MOD_ROWS, d), lambda i, j: (layer, 0, grp(i))),
                  pl.BlockSpec((1, d), lambda i, j: (0, 0)),
                  pl.BlockSpec((None, d, FF_CHUNK), lambda i, j: (0, 0, j)),
                  pl.BlockSpec((None, d, FF_CHUNK), lambda i, j: (0, 0, nj + j)),
                  pl.BlockSpec((SUBLANES, FF_CHUNK), lambda i, j: (0, j)),
                  pl.BlockSpec((1, FF_CHUNK), lambda i, j: (0, j)),
                  pl.BlockSpec((None, FF_CHUNK, d), lambda i, j: (0, j, 0)),
                  pl.BlockSpec((1, d), lambda i, j: (0, 0))],
        out_specs=pl.BlockSpec((tm, d), lambda i, j: (i, 0)),
        out_shape=jax.ShapeDtypeStruct((n_rows, d), F32),
        scratch_shapes=[pltpu.VMEM((tm + 2 * SUBLANES, d), BF16)],
        compiler_params=_params(2, V7X_VMEM_FFN_LIMIT_BYTES),
        name="ffn",
    )(xall, xall, xall, modt, g2, w_up, w_up, conv_w, conv_b, w_down, final_g)


def _rope_tables(seq, tm):
    pos = jnp.arange(seq, dtype=jnp.int32)
    row = (pos // GRID_W).astype(F32)
    col = (pos % GRID_W).astype(F32)
    freqs = ROPE_THETA ** (-jnp.arange(0, ROPE_AXIS_DIM, 2, dtype=F32) / ROPE_AXIS_DIM)
    ang_r = row[:, None] * freqs[None, :]
    ang_c = col[:, None] * freqs[None, :]
    cos = jnp.concatenate([jnp.cos(ang_r)] * 2 + [jnp.cos(ang_c)] * 2, axis=-1)
    sin = jnp.concatenate([-jnp.sin(ang_r), jnp.sin(ang_r), -jnp.sin(ang_c), jnp.sin(ang_c)], axis=-1)
    cos = jnp.concatenate([cos, jnp.ones((tm, HEAD_DIM), F32)], axis=0)
    sin = jnp.concatenate([sin, jnp.zeros((tm, HEAD_DIM), F32)], axis=0)
    return cos, sin


def _dft_matrix(n, scale, n_kt=None):
    n_t = n_k = n if n_kt is None else n_kt
    n1 = 1
    while n1 * n1 < n:
        n1 *= 2
    n2 = n // n1
    assert n_k % n2 == 0
    t = jnp.arange(n_t, dtype=jnp.int32)[None, :]
    a = jnp.arange(n_k // n2, dtype=jnp.int32)[:, None]
    b = jnp.arange(n2, dtype=jnp.int32)[:, None]
    ang_a = ((a * t) % n1).astype(F32) * (2.0 * math.pi / n1)
    ang_b = ((b * t) % n).astype(F32) * (2.0 * math.pi / n)
    ca, sa = jnp.cos(ang_a)[:, None, :], jnp.sin(ang_a)[:, None, :]
    cb, sb = jnp.cos(ang_b)[None, :, :] * scale, jnp.sin(ang_b)[None, :, :] * scale
    c = (ca * cb - sa * sb).reshape(n_k, n_t)
    neg_s = ((-sa) * cb - ca * sb).reshape(n_k, n_t)
    return c.astype(BF16), neg_s.astype(BF16)


def _dft_rows(n, scale, ks, n_t):
    k = jnp.asarray(list(ks) + [0] * (SUBLANES - len(ks)), jnp.int32)[:, None]
    valid = (jnp.arange(SUBLANES) < len(ks))[:, None]
    ang = ((k * jnp.arange(n_t, dtype=jnp.int32)[None, :]) % n).astype(F32) * (2.0 * math.pi / n)
    return (jnp.where(valid, jnp.cos(ang) * scale, 0.0).astype(BF16),
            jnp.where(valid, -jnp.sin(ang) * scale, 0.0).astype(BF16))


def _row_reversal(rb):
    r = jnp.arange(rb, dtype=jnp.int32)
    return (r[None, :] == (rb - r)[:, None]).astype(BF16)


def _channel_dft(scale):
    k = jnp.arange(HEAD_DIM, dtype=jnp.int32)
    ang = ((k[:, None] * k[None, :]) % HEAD_DIM).astype(F32) * (2.0 * math.pi / HEAD_DIM)
    return (jnp.concatenate([jnp.cos(ang), jnp.sin(ang)], axis=1) * scale).astype(BF16)


def _layout(batch, seq, ctx_len, max_tm):
    n_lat, n_ctx = batch * seq, batch * ctx_len
    tm = max_tm
    while n_ctx % tm or seq % tm:
        tm //= 2
    assert tm >= CHUNK and seq & (seq - 1) == 0 and ctx_len & (ctx_len - 1) == 0
    assert seq % GRID_W == 0 and ctx_len % CHUNK == 0 and n_lat % ctx_len == 0 and n_lat % seq == 0
    n_lat_tiles, tps = n_lat // tm, seq // tm
    grp = lambda i: jnp.where(i < n_lat_tiles, 1 + i // tps, 0)
    return dict(tm=tm, n_lat_tiles=n_lat_tiles, tiles_per_seq=tps, grp=grp, seq=seq, ctx_len=ctx_len,
                n_lat=n_lat, n_ctx=n_ctx)


def kernel(x, c, ctx, c_ctx, w_mod, b_mod, norm1_g, norm2_g, w_in, q_norm_g, k_norm_g, conv_w, gm_ln_g,
           gm_ln_b, gm_ws, gm_b, w_out, w_up, ffn_conv_w, ffn_conv_b, w_down, final_norm_g):
    batch, seq, d = x.shape
    ctx_len = ctx.shape[1]
    depth = w_mod.shape[0]
    d_ff = w_down.shape[1]
    assert batch + 1 <= MOD_ROWS and w_in.shape[2] == IN_W and w_down.shape[1] % FF_CHUNK == 0
    lay = _layout(batch, seq, ctx_len, ROW_TILE)
    lay_ffn = _layout(batch, seq, ctx_len, FFN_ROW_TILE)
    n_lat, tm = lay["n_lat"], lay["tm"]

    rows_src = (x.reshape(n_lat, d), ctx.reshape(batch * ctx_len, d), True)

    cs = jnp.concatenate([c_ctx[None, :], c, jnp.zeros((MOD_ROWS - 1 - batch, d), F32)], axis=0)
    mod = _modulation(cs, w_mod, b_mod)
    modt = mod.reshape(depth, MOD_ROWS, N_MOD, d).transpose(0, 2, 1, 3).reshape(depth, N_MOD, MOD_ROWS * d)
    modt = jnp.pad(modt, ((0, 0), (0, MOD_ROWS - N_MOD), (0, 0)))

    cos_t, sin_t = _rope_tables(seq, tm)
    f_lat = _dft_matrix(seq, seq ** -0.5, seq // 2)
    fold_starts = range(DFT_FOLD_BLOCK, seq // 2 + 1, DFT_FOLD_BLOCK)
    f_starts = _dft_rows(seq, seq ** -0.5, fold_starts, seq // 2)
    f_ctx = _dft_matrix(ctx_len, ctx_len ** -0.5)
    rev = _row_reversal(DFT_FOLD_BLOCK)
    dftc = _channel_dft(HEAD_DIM ** -0.5)

    pad_rows = lambda w: jnp.pad(w, ((0, 0), (0, SUBLANES - w.shape[1]), (0, 0)))
    conv_w8 = pad_rows(conv_w)
    ffn_conv_w8 = pad_rows(ffn_conv_w)
    bsf = jnp.broadcast_to(gm_b[..., None], gm_b.shape + (HEAD_DIM,))
    gm_ws_b = gm_ws.astype(BF16)
    fg = final_norm_g.reshape(1, d)

    big_weights = (w_in, w_out, w_up, w_down)
    w_in_b = w_in[:1].astype(BF16)

    for l in range(depth):
        last = l == depth - 1
        qt, k, vt, ab, cb, t, gm = _in_proj(
            rows_src, modt, l, norm1_g[l].reshape(1, d), w_in_b, q_norm_g[l].reshape(1, HEAD_DIM),
            k_norm_g[l].reshape(1, HEAD_DIM), cos_t, sin_t, dftc, gm_ln_g[l].reshape(1, GROUP_W),
            gm_ln_b[l].reshape(1, GROUP_W), gm_ws_b[l], bsf[l], lay=lay)
        four = _pos_dft(f_lat, f_starts, rev, f_ctx, ab, batch=batch, seq=seq, ctx_len=ctx_len, with_ctx=not last)
        cast_jobs = [(w, 0) for w in big_weights[1:]] if l == 0 else []
        cast_jobs += [] if last else [(w, l + 1) for w in big_weights]
        attn, casts = _attention(qt, k, vt, q_norm_g[l], k_norm_g[l], cast_jobs, batch=batch, seq=seq,
                                 ctx_len=ctx_len, with_ctx_queries=not last)
        if l == 0:
            w_out_b, w_up_b, w_down_b = (w[None] for w in casts[:3])
        next_weights = casts[-len(big_weights):]
        n_rows = n_lat if last else n_lat + lay["n_ctx"]
        xall = _out_proj(attn, four, cb, t, gm, conv_w8[l], w_out_b, rows_src, modt, l, lay=lay, n_rows=n_rows)
        xall = _ffn(xall, modt, l, norm2_g[l].reshape(1, d), w_up_b, ffn_conv_w8[l],
                    ffn_conv_b[l].reshape(1, d_ff), w_down_b, fg, lay=lay_ffn, n_rows=n_rows, final_norm=last)
        rows_src = (xall, xall, False)
        if not last:
            w_in_b, w_out_b, w_up_b, w_down_b = (w[None] for w in next_weights)
    return xall.reshape(batch, seq, d)
```

```python
import functools
import math

import jax
import jax.numpy as jnp
from jax import lax
from jax.experimental import pallas as pl
from jax.experimental.pallas import tpu as pltpu

F32 = jnp.float32
BF16 = jnp.bfloat16

GRID_W = 64
HEAD_DIM = 128
N_HEADS = 8
N_KV_HEADS = 2
Q_PER_KV = N_HEADS // N_KV_HEADS
ATTN_W = N_HEADS * HEAD_DIM
KV_W = N_KV_HEADS * HEAD_DIM
ROPE_THETA = 10000.0
ROPE_AXIS_DIM = HEAD_DIM // 2
ATTN_SCALE = HEAD_DIM ** -0.5
Q_SCALE = ATTN_SCALE * math.log2(math.e)
GROUP_W = 4 * HEAD_DIM
CHUNK = 128
N_MOD = 6
EPS = 1e-6

OFF_K = ATTN_W
OFF_V = OFF_K + KV_W
OFF_F = OFF_V + KV_W
OFF_CB = OFF_F + GROUP_W
OFF_CC = OFF_CB + GROUP_W
OFF_CH = OFF_CC + GROUP_W
OFF_GU = OFF_CH + GROUP_W
OFF_GV = OFF_GU + GROUP_W
IN_W = OFF_GV + GROUP_W
MIX_W = ATTN_W + 3 * GROUP_W

V7X_VMEM_LIMIT_BYTES = 56 * 1024 * 1024
V7X_VMEM_FFN_LIMIT_BYTES = 61 * 1024 * 1024
SUBLANES = 8
MOD_ROWS = 8
ROW_TILE = 512
FFN_ROW_TILE = 1024
FF_CHUNK = 512
FFN_FIRST_BLOCKS = 4
ATTN_Q_TILE = 512
KV_CHUNK = 1024
CAST_STRIP = 128
BF16_SUBLANES = 16
NORM_ROWS = BF16_SUBLANES
VT_ROWS = HEAD_DIM + BF16_SUBLANES
MAX_FIXED_SHIFT = 60.0
DFT_ROWS = 1024
DFT_FOLD_BLOCK = 256


def _params(n_axes, vmem_limit_bytes=V7X_VMEM_LIMIT_BYTES):
    return pltpu.CompilerParams(dimension_semantics=("arbitrary",) * n_axes, vmem_limit_bytes=vmem_limit_bytes)


def _resident(shape):
    return pl.BlockSpec(shape, lambda *_: (0,) * len(shape), pipeline_mode=pl.Buffered(1))


def _resident_layer(stacked_shape):
    rest = tuple(stacked_shape[1:])
    assert stacked_shape[0] == 1
    return pl.BlockSpec((None,) + rest, lambda *_: (0,) * (1 + len(rest)), pipeline_mode=pl.Buffered(1))


def _stream_specs(tm, d, n_lat_tiles, split):
    if not split:
        return pl.BlockSpec((tm, d), lambda i, *_: (i, 0)), pl.BlockSpec((SUBLANES, d), lambda i, *_: (0, 0))
    lat = pl.BlockSpec((tm, d), lambda i, *_: (jnp.minimum(i, n_lat_tiles - 1), 0))
    ctx = pl.BlockSpec((tm, d), lambda i, *_: (jnp.maximum(i - n_lat_tiles, 0), 0))
    return lat, ctx


def _stream_rows(tile, n_lat_tiles, split, rows_ref, ctx_ref):
    return jnp.where(tile < n_lat_tiles, rows_ref[...], ctx_ref[...]) if split else rows_ref[...]


def _dot(a, b):
    return jnp.dot(a, b, preferred_element_type=F32)


def _gelu_tanh(x):
    return 0.5 * x * (1.0 + jnp.tanh(math.sqrt(2.0 / math.pi) * (x + 0.044715 * (x * x * x))))


def _silu(x):
    return x * (1.0 / (1.0 + jnp.exp(-x)))


def _rms(x, eps=EPS):
    return x * lax.rsqrt(jnp.mean(x * x, axis=-1, keepdims=True) + eps)


def _split_bf16(x):
    hi = x.astype(BF16)
    return hi, (x - hi.astype(F32)).astype(BF16)


def _mod_kernel(c_ref, w_ref, b_ref, o_ref):
    s_hi, s_lo = _split_bf16(_silu(c_ref[...]))
    w_hi, w_lo = _split_bf16(w_ref[...])
    r = _dot(jnp.concatenate([s_hi, s_lo], axis=0), w_hi)
    o_ref[...] = r[:MOD_ROWS] + r[MOD_ROWS:] + _dot(s_hi, w_lo) + b_ref[...]


def _modulation(cs, w_mod, b_mod):
    depth, d, n = w_mod.shape
    tn = next(t for t in (2048, 1024, 512, 256, 128) if n % t == 0)
    return pl.pallas_call(
        _mod_kernel,
        grid=(depth, n // tn),
        in_specs=[pl.BlockSpec((MOD_ROWS, d), lambda l, j: (0, 0)),
                  pl.BlockSpec((None, d, tn), lambda l, j: (l, 0, j)),
                  pl.BlockSpec((None, 1, tn), lambda l, j: (l, 0, j))],
        out_specs=pl.BlockSpec((None, MOD_ROWS, tn), lambda l, j: (l, 0, j)),
        out_shape=jax.ShapeDtypeStruct((depth, MOD_ROWS, n), F32),
        compiler_params=_params(2),
        name="modulation",
    )(cs, w_mod, b_mod.reshape(depth, 1, n))


def _in_kernel(xl_ref, xc_ref, m_ref, g1_ref, w_ref, qg_ref, kg_ref, cos_ref, sin_ref, dftc_ref,
               lng_ref, lnb_ref, ws_ref, bs_ref,
               qt_ref, k_ref, vt_ref, ab_ref, cb_ref, t_ref, gm_ref, *, n_lat_tiles, split_rows):
    tm = xl_ref.shape[0]
    x = _stream_rows(pl.program_id(0), n_lat_tiles, split_rows, xl_ref, xc_ref)
    h = _rms(x) * g1_ref[...]
    h = h * (1.0 + m_ref[1:2, :]) + m_ref[0:1, :]
    hb = h.astype(BF16)

    def proj(lo, width=GROUP_W):
        return _dot(hb, w_ref[:, lo:lo + width])

    cos = cos_ref[...]
    sin = sin_ref[...]
    lane = lax.broadcasted_iota(jnp.int32, (tm, HEAD_DIM), 1)
    first_half = (lane % (ROPE_AXIS_DIM)) < (ROPE_AXIS_DIM // 2)

    def norm_rope(ph, gain):
        y = _rms(ph) * gain
        partner = jnp.where(first_half,
                            pltpu.roll(y, HEAD_DIM - ROPE_AXIS_DIM // 2, 1),
                            pltpu.roll(y, ROPE_AXIS_DIM // 2, 1))
        return y * cos + partner * sin

    def queries():
        qg = qg_ref[...] * Q_SCALE
        for half in range(ATTN_W // GROUP_W):
            p = proj(half * GROUP_W)
            for hh in range(GROUP_W // HEAD_DIM):
                c0 = half * GROUP_W + hh * HEAD_DIM
                y = norm_rope(p[:, hh * HEAD_DIM:(hh + 1) * HEAD_DIM], qg)
                qt_ref[c0:c0 + HEAD_DIM, :] = y.T.astype(BF16)

    def keys_values():
        p = proj(OFF_K)
        kg = kg_ref[...]
        for hh in range(N_KV_HEADS):
            cols = slice(hh * HEAD_DIM, (hh + 1) * HEAD_DIM)
            k_ref[:, cols] = norm_rope(p[:, cols], kg).astype(BF16)
            r0 = hh * VT_ROWS
            vt_ref[r0:r0 + HEAD_DIM, :] = p[:, KV_W + hh * HEAD_DIM:KV_W + (hh + 1) * HEAD_DIM].T.astype(BF16)
            vt_ref[r0 + HEAD_DIM:r0 + VT_ROWS, :] = jnp.ones((BF16_SUBLANES, tm), BF16)

    def fourier():
        p = proj(OFF_F).astype(BF16)
        dftc = dftc_ref[...]
        for g in range(GROUP_W // HEAD_DIM):
            r = _dot(p[:, g * HEAD_DIM:(g + 1) * HEAD_DIM], dftc)
            ab_ref[:, g * HEAD_DIM:(g + 1) * HEAD_DIM] = r[:, :HEAD_DIM].astype(BF16)
            ab_ref[:, GROUP_W + g * HEAD_DIM:GROUP_W + (g + 1) * HEAD_DIM] = r[:, HEAD_DIM:].astype(BF16)

    def conv_gates():
        t_ref[...] = (proj(OFF_CC) * proj(OFF_CH)).astype(BF16)
        cb_ref[...] = proj(OFF_CB).astype(BF16)

    def spatial_gating():
        u = _gelu_tanh(proj(OFF_GU))
        gv = _gelu_tanh(proj(OFF_GV))
        gc = gv - jnp.mean(gv, axis=-1, keepdims=True)
        vn = gc * lax.rsqrt(jnp.mean(gc * gc, axis=-1, keepdims=True) + EPS) * lng_ref[...] + lnb_ref[...]
        vn = vn.astype(BF16)
        for g in range(GROUP_W // HEAD_DIM):
            wsg = ws_ref[g]
            bsg = bs_ref[g]
            for c in range(tm // CHUNK):
                rows = slice(c * CHUNK, (c + 1) * CHUNK)
                cols = slice(g * HEAD_DIM, (g + 1) * HEAD_DIM)
                s = _dot(wsg, vn[rows, cols]) + bsg
                gm_ref[rows, cols] = (u[rows, cols] * s).astype(BF16)

    spatial_gating()
    queries()
    keys_values()
    fourier()
    conv_gates()


def _in_proj(rows_src, modt, layer, g1, w_in, qg, kg, cos_t, sin_t, dftc, lng, lnb, ws, bsf, *, lay):
    x_lat, x_ctx, split_rows = rows_src
    r_rows, d = lay["n_lat"] + lay["n_ctx"], x_lat.shape[1]
    tm = lay["tm"]
    grp, nlt, tps = lay["grp"], lay["n_lat_tiles"], lay["tiles_per_seq"]
    row = lambda w: pl.BlockSpec((tm, w), lambda i: (i, 0))
    vec = lambda w: pl.BlockSpec((1, w), lambda i: (0, 0))
    tab = pl.BlockSpec((tm, HEAD_DIM), lambda i: (jnp.where(i < nlt, i % tps, tps), 0))
    col = lambda h: pl.BlockSpec((h, tm), lambda i: (0, i))
    rows_bf16 = lambda w: jax.ShapeDtypeStruct((r_rows, w), BF16)
    cols_bf16 = lambda h: jax.ShapeDtypeStruct((h, r_rows), BF16)
    return pl.pallas_call(
        functools.partial(_in_kernel, n_lat_tiles=nlt, split_rows=split_rows),
        grid=(r_rows // tm,),
        in_specs=[*_stream_specs(tm, d, nlt, split_rows),
                  pl.BlockSpec((None, MOD_ROWS, d), lambda i: (layer, 0, grp(i))),
                  vec(d), _resident_layer(w_in.shape), vec(HEAD_DIM), vec(HEAD_DIM), tab, tab,
                  _resident(dftc.shape), vec(GROUP_W), vec(GROUP_W),
                  _resident(ws.shape), _resident(bsf.shape)],
        out_specs=[col(ATTN_W), row(KV_W), col(N_KV_HEADS * VT_ROWS), row(2 * GROUP_W), row(GROUP_W), row(GROUP_W),
                   row(GROUP_W)],
        out_shape=[cols_bf16(ATTN_W), rows_bf16(KV_W), cols_bf16(N_KV_HEADS * VT_ROWS), rows_bf16(2 * GROUP_W),
                   rows_bf16(GROUP_W), rows_bf16(GROUP_W), rows_bf16(GROUP_W)],
        compiler_params=_params(1),
        name="in_proj",
    )(x_lat, x_ctx, modt, g1, w_in, qg, kg, cos_t, sin_t, dftc, lng, lnb, ws, bsf)


def _dft_kernel(fc_ref, fs_ref, fkc_ref, fks_ref, ab_ref, rev_ref, cc_ref, cs_ref, abc_ref, o_ref, rhs_ref, y_ref,
                *, n_lat_steps, per_seq, scale):
    step = pl.program_id(0)
    n = ab_ref.shape[0]
    half = n // 2
    rb = rev_ref.shape[0]
    n_fold = half // rb

    @pl.when(jnp.logical_and(step < n_lat_steps, step % per_seq == 0))
    def _():
        n_blocks = n // rb
        rows_iota = lax.broadcasted_iota(jnp.int32, (rb, 1), 0)
        row0 = rows_iota == 0
        for j in range(n_fold):
            src = n_blocks - 1 - j
            wrap = ((n_blocks - j) % n_blocks) * rb
            mirrored = _dot(rev_ref[...], ab_ref[src * rb:(src + 1) * rb, :])
            first = ab_ref[wrap:wrap + BF16_SUBLANES, :][0:1, :].astype(F32)
            mirrored = jnp.where(row0, first, mirrored)
            own = ab_ref[j * rb:(j + 1) * rb, :].astype(F32)
            rhs_ref[j * rb:(j + 1) * rb, :] = (own[:, :GROUP_W] + mirrored[:, :GROUP_W]).astype(BF16)
            rhs_ref[half + j * rb:half + (j + 1) * rb, :] = (own[:, GROUP_W:] - mirrored[:, GROUP_W:]).astype(BF16)

        a_0 = ab_ref[0:BF16_SUBLANES, :GROUP_W][0:1, :].astype(F32)
        a_half = ab_ref[half:half + BF16_SUBLANES, :GROUP_W][0:1, :].astype(F32)
        sign = (1 - 2 * (rows_iota & 1)).astype(F32)
        edge = scale * (sign * a_half - a_0)
        starts = (_dot(fkc_ref[...], rhs_ref[:half, :]) - _dot(fks_ref[...], rhs_ref[half:, :])
                  + scale * (a_half - a_0))
        p = _dot(fc_ref[...], rhs_ref[:half, :])
        neg_q = _dot(fs_ref[...], rhs_ref[half:, :])
        for j in range(n_fold):
            rows = slice(j * rb, (j + 1) * rb)
            y_ref[rows, :] = (p[rows] + neg_q[rows] + edge).astype(BF16)
            upper = _dot(rev_ref[...], (p[rows] - neg_q[rows] + edge).astype(BF16))
            upper = jnp.where(row0, starts[j:j + 1, :], upper)
            dst = half + (n_fold - 1 - j) * rb
            y_ref[dst:dst + rb, :] = upper.astype(BF16)

    @pl.when(step < n_lat_steps)
    def _():
        tmd = o_ref.shape[0]
        o_ref[...] = y_ref[pl.ds(pl.multiple_of((step % per_seq) * tmd, tmd), tmd), :]

    @pl.when(step >= n_lat_steps)
    def _():
        n_ctx = cc_ref.shape[0]
        for r0 in range(0, o_ref.shape[0], n_ctx):
            rows = slice(r0, r0 + n_ctx)
            y = _dot(cc_ref[...], abc_ref[rows, :GROUP_W]) + _dot(cs_ref[...], abc_ref[rows, GROUP_W:])
            o_ref[rows, :] = y.astype(BF16)


def _pos_dft(f_lat, f_starts, rev, f_ctx, ab, *, batch, seq, ctx_len, with_ctx):
    tmd = min(DFT_ROWS, seq)
    while (batch * ctx_len) % tmd:
        tmd //= 2
    rb = rev.shape[0]
    assert tmd % ctx_len == 0 and seq % tmd == 0 and (seq // 2) % rb == 0 and rb % 2 == 0
    assert seq // 2 // rb <= f_starts[0].shape[0]
    per_seq = seq // tmd
    n_lat_steps = batch * per_seq
    n_ctx_steps = batch * ctx_len // tmd if with_ctx else 0
    lat = lambda s: s < n_lat_steps
    return pl.pallas_call(
        functools.partial(_dft_kernel, n_lat_steps=n_lat_steps, per_seq=per_seq, scale=seq ** -0.5),
        grid=(n_lat_steps + n_ctx_steps,),
        in_specs=[_resident(f_lat[0].shape), _resident(f_lat[1].shape),
                  _resident(f_starts[0].shape), _resident(f_starts[1].shape),
                  pl.BlockSpec((seq, 2 * GROUP_W), lambda s: (jnp.where(lat(s), s // per_seq, 0), 0)),
                  _resident(rev.shape), _resident(f_ctx[0].shape), _resident(f_ctx[1].shape),
                  pl.BlockSpec((tmd, 2 * GROUP_W), lambda s: (jnp.where(lat(s), n_lat_steps, s), 0))],
        out_specs=pl.BlockSpec((tmd, GROUP_W), lambda s: (s, 0)),
        out_shape=jax.ShapeDtypeStruct(((n_lat_steps + n_ctx_steps) * tmd, GROUP_W), BF16),
        scratch_shapes=[pltpu.VMEM((seq, GROUP_W), BF16), pltpu.VMEM((seq, GROUP_W), BF16)],
        compiler_params=_params(1),
        name="pos_dft",
    )(*f_lat, *f_starts, ab, rev, *f_ctx, ab)


def _attn_kernel(shift_ref, qt_ref, qtc_ref, kl_ref, kc_ref, vtl_ref, vtc_ref, *rest, with_ctx_queries,
                 fixed_shift, cast_blocks):
    n_cast = len(cast_blocks)
    n_out = 2 if with_ctx_queries else 1
    o_ref = rest[n_cast]
    step = (pl.program_id(0) * pl.num_programs(1) + pl.program_id(1)) * pl.num_programs(2) + pl.program_id(2)
    for src_ref, dst_ref, n_blocks in zip(rest[:n_cast], rest[n_cast + n_out:], cast_blocks):
        @pl.when(step < n_blocks)
        def _(src_ref=src_ref, dst_ref=dst_ref):
            dst_ref[...] = src_ref[...].astype(BF16)

    seq = kl_ref.shape[0]

    kv_chunk = KV_CHUNK if fixed_shift else KV_CHUNK // 4

    def key_chunks(with_latent_keys):
        chunks = [(kc_ref, vtc_ref, slice(None))]
        if with_latent_keys:
            chunks += [(kl_ref, vtl_ref, slice(j * kv_chunk, (j + 1) * kv_chunk)) for j in range(seq // kv_chunk)]
        return chunks

    def attend(q_ref, out_ref, with_latent_keys):
        tq = q_ref.shape[1]
        cols = Q_PER_KV * tq
        qt = jnp.concatenate([q_ref[g * HEAD_DIM:(g + 1) * HEAD_DIM, :] for g in range(Q_PER_KV)], axis=1)
        if fixed_shift:
            shift = shift_ref[0, 0]
            acc = None
            for k_ref, vt_ref, ks in key_chunks(with_latent_keys):
                p = jnp.exp2(_dot(k_ref[ks, :], qt) - shift).astype(BF16)
                part = _dot(vt_ref[:, ks], p)
                acc = part if acc is None else acc + part
            l, acc = acc[HEAD_DIM:HEAD_DIM + 1], acc[:HEAD_DIM]
        else:
            m = jnp.full((1, cols), -jnp.inf, F32)
            l = jnp.zeros((1, cols), F32)
            acc = jnp.zeros((HEAD_DIM, cols), F32)
            for k_ref, vt_ref, ks in key_chunks(with_latent_keys):
                s = _dot(k_ref[ks, :], qt)
                m_new = jnp.maximum(m, jnp.max(s, axis=0, keepdims=True))
                alpha = jnp.exp2(m - m_new)
                p = jnp.exp2(s - m_new)
                l = alpha * l + jnp.sum(p, axis=0, keepdims=True)
                acc = alpha * acc + _dot(vt_ref[:HEAD_DIM, ks], p.astype(BF16))
                m = m_new
        o = acc * (1.0 / l)
        for g in range(Q_PER_KV):
            out_ref[:, g * HEAD_DIM:(g + 1) * HEAD_DIM] = o[:, g * tq:(g + 1) * tq].T.astype(BF16)

    attend(qt_ref, o_ref, True)
    if with_ctx_queries:
        pl.when(pl.program_id(2) == 0)(lambda: attend(qtc_ref, rest[n_cast + 1], False))


def _attention(qt, k, vt, q_gain, k_gain, cast_jobs, *, batch, seq, ctx_len, with_ctx_queries):
    cast_weights = [w for w, _ in cast_jobs]
    cast_layers = [layer for _, layer in cast_jobs]
    tq = min(ATTN_Q_TILE, seq)
    assert seq % tq == 0
    nq = seq // tq
    ctx_blk0 = batch * seq // ctx_len
    qw = Q_PER_KV * HEAD_DIM
    n_i = nq
    grid = (batch, N_KV_HEADS, n_i)

    n_steps = math.prod(grid)
    widths = [next(wd for wd in range(CAST_STRIP, w.shape[2] + 1, CAST_STRIP)
                   if w.shape[2] % wd == 0 and w.shape[2] // wd <= n_steps) for w in cast_weights]
    cast_blocks = tuple(w.shape[2] // wd for w, wd in zip(cast_weights, widths))
    strip = lambda nb: lambda b, hg, i: jnp.minimum((b * N_KV_HEADS + hg) * n_i + i, nb - 1)
    cast_in = [pl.BlockSpec((None, w.shape[1], wd), lambda b, hg, i, f=strip(nb), layer=layer: (layer, 0, f(b, hg, i)))
               for w, wd, nb, layer in zip(cast_weights, widths, cast_blocks, cast_layers)]
    cast_out = [pl.BlockSpec((w.shape[1], wd), lambda b, hg, i, f=strip(nb): (0, f(b, hg, i)))
                for w, wd, nb in zip(cast_weights, widths, cast_blocks)]
    cast_shapes = [jax.ShapeDtypeStruct(w.shape[1:], BF16) for w in cast_weights]

    ctx_out_specs = [pl.BlockSpec((ctx_len, qw), lambda b, hg, i: (b, hg))] if with_ctx_queries else []
    ctx_out_shapes = [jax.ShapeDtypeStruct((batch * ctx_len, ATTN_W), BF16)] if with_ctx_queries else []

    def call(fixed_shift, shift):
        return pl.pallas_call(
            functools.partial(_attn_kernel, with_ctx_queries=with_ctx_queries, fixed_shift=fixed_shift,
                              cast_blocks=cast_blocks),
            grid=grid,
            in_specs=[pl.BlockSpec(memory_space=pltpu.SMEM),
                      pl.BlockSpec((qw, tq), lambda b, hg, i: (hg, b * nq + i)),
                      pl.BlockSpec((qw, ctx_len), lambda b, hg, i: (hg, ctx_blk0 + b)),
                      pl.BlockSpec((seq, HEAD_DIM), lambda b, hg, i: (b, hg)),
                      pl.BlockSpec((ctx_len, HEAD_DIM), lambda b, hg, i: (ctx_blk0 + b, hg)),
                      pl.BlockSpec((VT_ROWS, seq), lambda b, hg, i: (hg, b)),
                      pl.BlockSpec((VT_ROWS, ctx_len), lambda b, hg, i: (hg, ctx_blk0 + b))] + cast_in,
            out_specs=[pl.BlockSpec((tq, qw), lambda b, hg, i: (b * nq + i, hg))] + ctx_out_specs + cast_out,
            out_shape=[jax.ShapeDtypeStruct((batch * seq, ATTN_W), BF16)] + ctx_out_shapes + cast_shapes,
            compiler_params=_params(3),
            name="attention_fixed" if fixed_shift else "attention_online",
        )(shift, qt, qt, k, k, vt, vt, *cast_weights)

    bound = (HEAD_DIM * Q_SCALE * (1.0 + 2.0 ** -6)) * jnp.max(jnp.abs(q_gain)) * jnp.max(jnp.abs(k_gain))
    shift = bound.astype(F32).reshape(1, 1)
    out = lax.cond(bound <= MAX_FIXED_SHIFT, lambda: call(True, shift), lambda: call(False, shift))
    n_attn = 1 + len(ctx_out_specs)
    return out[0], (out[1] if with_ctx_queries else None), out[n_attn:]


def _seq_edge_masks(tile, tm, n_lat_tiles, seq, ctx_len):
    r = lax.broadcasted_iota(jnp.int32, (tm, 1), 0)
    period = jnp.where(tile < n_lat_tiles, seq, ctx_len)
    pos = (tile * tm + r) & (period - 1)
    return pos == 0, pos == period - 1


def _dwconv3(center, prev_row, next_row, w_ref, is_start, is_end):
    tm = center.shape[0]
    r = lax.broadcasted_iota(jnp.int32, (tm, 1), 0)
    up = jnp.where(r == 0, prev_row, pltpu.roll(center, 1, 0))
    dn = jnp.where(r == tm - 1, next_row, pltpu.roll(center, tm - 1, 0))
    up = jnp.where(is_start, 0.0, up)
    dn = jnp.where(is_end, 0.0, dn)
    return up * w_ref[0:1, :] + center * w_ref[1:2, :] + dn * w_ref[2:3, :]


def _out_kernel(al_ref, ac_ref, four_ref, cb_ref, t_ref, tp_ref, tn_ref, gm_ref, cw_ref, w_ref, xl_ref, xc_ref,
                m_ref, o_ref, mix_ref, *, n_lat_tiles, seq, ctx_len, split_rows, split_attn):
    tm = xl_ref.shape[0]
    i = pl.program_id(0)
    is_start, is_end = _seq_edge_masks(i, tm, n_lat_tiles, seq, ctx_len)
    conv = _dwconv3(t_ref[...].astype(F32), tp_ref[SUBLANES - 1:SUBLANES, :].astype(F32),
                    tn_ref[0:1, :].astype(F32), cw_ref, is_start, is_end)
    mix_ref[:, :ATTN_W] = _stream_rows(i, n_lat_tiles, split_attn, al_ref, ac_ref)
    mix_ref[:, ATTN_W:ATTN_W + GROUP_W] = four_ref[...]
    mix_ref[:, ATTN_W + GROUP_W:ATTN_W + 2 * GROUP_W] = (cb_ref[...].astype(F32) * conv).astype(BF16)
    mix_ref[:, ATTN_W + 2 * GROUP_W:] = gm_ref[...]
    x = _stream_rows(i, n_lat_tiles, split_rows, xl_ref, xc_ref)
    o_ref[...] = x + m_ref[2:3, :] * _dot(mix_ref[...], w_ref[...])


def _halo_specs(tm, width, n_rows):
    per = tm // SUBLANES
    last = n_rows // SUBLANES - 1
    prev = pl.BlockSpec((SUBLANES, width), lambda i, *_: (jnp.maximum(i * per - 1, 0), 0))
    nxt = pl.BlockSpec((SUBLANES, width), lambda i, *_: (jnp.minimum((i + 1) * per, last), 0))
    return prev, nxt


def _out_proj(attn_src, four, cb, t, gm, conv_w, w_out, rows_src, modt, layer, *, lay, n_rows):
    attn_lat, attn_ctx, split_attn = attn_src
    x_lat, x_ctx, split_rows = rows_src
    d = x_lat.shape[1]
    tm = lay["tm"]
    grp = lay["grp"]
    row = lambda w: pl.BlockSpec((tm, w), lambda i: (i, 0))
    tprev, tnext = _halo_specs(tm, GROUP_W, t.shape[0])
    kern = functools.partial(_out_kernel, n_lat_tiles=lay["n_lat_tiles"], seq=lay["seq"], ctx_len=lay["ctx_len"],
                             split_rows=split_rows, split_attn=split_attn)
    return pl.pallas_call(
        kern,
        grid=(n_rows // tm,),
        in_specs=[*_stream_specs(tm, ATTN_W, lay["n_lat_tiles"], split_attn), row(GROUP_W), row(GROUP_W), row(GROUP_W), tprev, tnext, row(GROUP_W),
                  _resident(conv_w.shape), _resident_layer(w_out.shape),
                  *_stream_specs(tm, d, lay["n_lat_tiles"], split_rows),
                  pl.BlockSpec((None, MOD_ROWS, d), lambda i: (layer, 0, grp(i)))],
        out_specs=row(d),
        out_shape=jax.ShapeDtypeStruct((n_rows, d), F32),
        scratch_shapes=[pltpu.VMEM((tm, MIX_W), BF16)],
        compiler_params=_params(1),
        name="out_proj",
    )(attn_lat, attn_ctx, four, cb, t, t, t, gm, conv_w, w_out, x_lat, x_ctx, modt)


def _ffn_kernel(x_ref, xp_ref, xn_ref, m_ref, g2_ref, wg_ref, wu_ref, cw_ref, cb_ref, wd_ref, fg_ref,
                o_ref, h_ref, *, n_lat_tiles, seq, ctx_len, final_norm):
    tm = x_ref.shape[0]
    i = pl.program_id(0)
    j = pl.program_id(1)

    def chunk(n_blocks, first):
        rs = tm // n_blocks
        ends = [(b + 1) * rs for b in range(n_blocks - 1)] + [tm + 2 * SUBLANES]
        g = jnp.concatenate([_dot(h_ref[b * rs:e, :], wg_ref[...]) for b, e in enumerate(ends)], axis=0)
        u = jnp.concatenate([_dot(h_ref[b * rs:(b + 1) * rs, :], wu_ref[...]) for b in range(n_blocks)], axis=0)
        is_start, is_end = _seq_edge_masks(i, tm, n_lat_tiles, seq, ctx_len)
        conv = _dwconv3(g[0:tm], g[tm + 2 * SUBLANES - 1:tm + 2 * SUBLANES], g[tm:tm + 1], cw_ref,
                        is_start, is_end)
        act = (_silu(conv + cb_ref[...]) * u).astype(BF16)
        part = _dot(act, wd_ref[...])
        o_ref[...] = part if first else o_ref[...] + part

    @pl.when(j == 0)
    def _():
        gain = g2_ref[...] * (1.0 + m_ref[4:5, :])
        shift = m_ref[3:4, :]

        def norm_mod(x):
            return (_rms(x) * gain + shift).astype(BF16)

        h_ref[tm:tm + SUBLANES, :] = norm_mod(xn_ref[...])
        h_ref[tm + SUBLANES:, :] = norm_mod(xp_ref[...])
        for r0 in range(0, tm, NORM_ROWS):
            h_ref[r0:r0 + NORM_ROWS, :] = norm_mod(x_ref[r0:r0 + NORM_ROWS, :])
        chunk(FFN_FIRST_BLOCKS, first=True)

    @pl.when(j > 0)
    def _():
        chunk(1, first=False)

    @pl.when(j == pl.num_programs(1) - 1)
    def _():
        gate = m_ref[5:6, :]
        for r0 in range(0, tm, NORM_ROWS):
            rows = slice(r0, r0 + NORM_ROWS)
            y = x_ref[rows, :] + gate * o_ref[rows, :]
            if final_norm:
                y = _rms(y) * fg_ref[...]
            o_ref[rows, :] = y


def _ffn(xall, modt, layer, g2, w_up, conv_w, conv_b, w_down, final_g, *, lay, n_rows, final_norm):
    d = xall.shape[1]
    d_ff = w_down.shape[1]
    tm = lay["tm"]
    grp = lay["grp"]
    nj = d_ff // FF_CHUNK
    xprev, xnext = _halo_specs(tm, d, xall.shape[0])
    kern = functools.partial(_ffn_kernel, n_lat_tiles=lay["n_lat_tiles"], seq=lay["seq"],
                             ctx_len=lay["ctx_len"], final_norm=final_norm)
    return pl.pallas_call(
        kern,
        grid=(n_rows // tm, nj),
        in_specs=[pl.BlockSpec((tm, d), lambda i, j: (i, 0)), xprev, xnext,
                  pl.BlockSpec((None, MOD_ROWS, d), lambda i, j: (layer, 0, grp(i))),
                  pl.BlockSpec((1, d), lambda i, j: (0, 0)),
                  pl.BlockSpec((None, d, FF_CHUNK), lambda i, j: (0, 0, j)),
                  pl.BlockSpec((None, d, FF_CHUNK), lambda i, j: (0, 0, nj + j)),
                  pl.BlockSpec((SUBLANES, FF_CHUNK), lambda i, j: (0, j)),
                  pl.BlockSpec((1, FF_CHUNK), lambda i, j: (0, j)),
                  pl.BlockSpec((None, FF_CHUNK, d), lambda i, j: (0, j, 0)),
                  pl.BlockSpec((1, d), lambda i, j: (0, 0))],
        out_specs=pl.BlockSpec((tm, d), lambda i, j: (i, 0)),
        out_shape=jax.ShapeDtypeStruct((n_rows, d), F32),
        scratch_shapes=[pltpu.VMEM((tm + 2 * SUBLANES, d), BF16)],
        compiler_params=_params(2, V7X_VMEM_FFN_LIMIT_BYTES),
        name="ffn",
    )(xall, xall, xall, modt, g2, w_up, w_up, conv_w, conv_b, w_down, final_g)


def _rope_tables(seq, tm):
    pos = jnp.arange(seq, dtype=jnp.int32)
    row = (pos // GRID_W).astype(F32)
    col = (pos % GRID_W).astype(F32)
    freqs = ROPE_THETA ** (-jnp.arange(0, ROPE_AXIS_DIM, 2, dtype=F32) / ROPE_AXIS_DIM)
    ang_r = row[:, None] * freqs[None, :]
    ang_c = col[:, None] * freqs[None, :]
    cos = jnp.concatenate([jnp.cos(ang_r)] * 2 + [jnp.cos(ang_c)] * 2, axis=-1)
    sin = jnp.concatenate([-jnp.sin(ang_r), jnp.sin(ang_r), -jnp.sin(ang_c), jnp.sin(ang_c)], axis=-1)
    cos = jnp.concatenate([cos, jnp.ones((tm, HEAD_DIM), F32)], axis=0)
    sin = jnp.concatenate([sin, jnp.zeros((tm, HEAD_DIM), F32)], axis=0)
    return cos, sin


def _dft_matrix(n, scale, n_kt=None):
    n_t = n_k = n if n_kt is None else n_kt
    n1 = 1
    while n1 * n1 < n:
        n1 *= 2
    n2 = n // n1
    assert n_k % n2 == 0
    t = jnp.arange(n_t, dtype=jnp.int32)[None, :]
    a = jnp.arange(n_k // n2, dtype=jnp.int32)[:, None]
    b = jnp.arange(n2, dtype=jnp.int32)[:, None]
    ang_a = ((a * t) % n1).astype(F32) * (2.0 * math.pi / n1)
    ang_b = ((b * t) % n).astype(F32) * (2.0 * math.pi / n)
    ca, sa = jnp.cos(ang_a)[:, None, :], jnp.sin(ang_a)[:, None, :]
    cb, sb = jnp.cos(ang_b)[None, :, :] * scale, jnp.sin(ang_b)[None, :, :] * scale
    c = (ca * cb - sa * sb).reshape(n_k, n_t)
    neg_s = ((-sa) * cb - ca * sb).reshape(n_k, n_t)
    return c.astype(BF16), neg_s.astype(BF16)


def _dft_rows(n, scale, ks, n_t):
    k = jnp.asarray(list(ks) + [0] * (SUBLANES - len(ks)), jnp.int32)[:, None]
    valid = (jnp.arange(SUBLANES) < len(ks))[:, None]
    ang = ((k * jnp.arange(n_t, dtype=jnp.int32)[None, :]) % n).astype(F32) * (2.0 * math.pi / n)
    return (jnp.where(valid, jnp.cos(ang) * scale, 0.0).astype(BF16),
            jnp.where(valid, -jnp.sin(ang) * scale, 0.0).astype(BF16))


def _row_reversal(rb):
    r = jnp.arange(rb, dtype=jnp.int32)
    return (r[None, :] == (rb - r)[:, None]).astype(BF16)


def _channel_dft(scale):
    k = jnp.arange(HEAD_DIM, dtype=jnp.int32)
    ang = ((k[:, None] * k[None, :]) % HEAD_DIM).astype(F32) * (2.0 * math.pi / HEAD_DIM)
    return (jnp.concatenate([jnp.cos(ang), jnp.sin(ang)], axis=1) * scale).astype(BF16)


def _layout(batch, seq, ctx_len, max_tm):
    n_lat, n_ctx = batch * seq, batch * ctx_len
    tm = max_tm
    while n_ctx % tm or seq % tm:
        tm //= 2
    assert tm >= CHUNK and seq & (seq - 1) == 0 and ctx_len & (ctx_len - 1) == 0
    assert seq % GRID_W == 0 and ctx_len % CHUNK == 0 and n_lat % ctx_len == 0 and n_lat % seq == 0
    n_lat_tiles, tps = n_lat // tm, seq // tm
    grp = lambda i: jnp.where(i < n_lat_tiles, 1 + i // tps, 0)
    return dict(tm=tm, n_lat_tiles=n_lat_tiles, tiles_per_seq=tps, grp=grp, seq=seq, ctx_len=ctx_len,
                n_lat=n_lat, n_ctx=n_ctx)


def kernel(x, c, ctx, c_ctx, w_mod, b_mod, norm1_g, norm2_g, w_in, q_norm_g, k_norm_g, conv_w, gm_ln_g,
           gm_ln_b, gm_ws, gm_b, w_out, w_up, ffn_conv_w, ffn_conv_b, w_down, final_norm_g):
    batch, seq, d = x.shape
    ctx_len = ctx.shape[1]
    depth = w_mod.shape[0]
    d_ff = w_down.shape[1]
    assert batch + 1 <= MOD_ROWS and w_in.shape[2] == IN_W and w_down.shape[1] % FF_CHUNK == 0
    lay = _layout(batch, seq, ctx_len, ROW_TILE)
    lay_ffn = _layout(batch, seq, ctx_len, FFN_ROW_TILE)
    n_lat, tm = lay["n_lat"], lay["tm"]

    rows_src = (x.reshape(n_lat, d), ctx.reshape(batch * ctx_len, d), True)

    cs = jnp.concatenate([c_ctx[None, :], c, jnp.zeros((MOD_ROWS - 1 - batch, d), F32)], axis=0)
    mod = _modulation(cs, w_mod, b_mod)
    modt = mod.reshape(depth, MOD_ROWS, N_MOD, d).transpose(0, 2, 1, 3).reshape(depth, N_MOD, MOD_ROWS * d)
    modt = jnp.pad(modt, ((0, 0), (0, MOD_ROWS - N_MOD), (0, 0)))

    cos_t, sin_t = _rope_tables(seq, tm)
    f_lat = _dft_matrix(seq, seq ** -0.5, seq // 2)
    fold_starts = range(DFT_FOLD_BLOCK, seq // 2 + 1, DFT_FOLD_BLOCK)
    f_starts = _dft_rows(seq, seq ** -0.5, fold_starts, seq // 2)
    f_ctx = _dft_matrix(ctx_len, ctx_len ** -0.5)
    rev = _row_reversal(DFT_FOLD_BLOCK)
    dftc = _channel_dft(HEAD_DIM ** -0.5)

    pad_rows = lambda w: jnp.pad(w, ((0, 0), (0, SUBLANES - w.shape[1]), (0, 0)))
    conv_w8 = pad_rows(conv_w)
    ffn_conv_w8 = pad_rows(ffn_conv_w)
    bsf = jnp.broadcast_to(gm_b[..., None], gm_b.shape + (HEAD_DIM,))
    gm_ws_b = gm_ws.astype(BF16)
    fg = final_norm_g.reshape(1, d)

    big_weights = (w_in, w_out, w_up, w_down)
    w_in_b = w_in[:1].astype(BF16)

    for l in range(depth):
        last = l == depth - 1
        qt, k, vt, ab, cb, t, gm = _in_proj(
            rows_src, modt, l, norm1_g[l].reshape(1, d), w_in_b, q_norm_g[l].reshape(1, HEAD_DIM),
            k_norm_g[l].reshape(1, HEAD_DIM), cos_t, sin_t, dftc, gm_ln_g[l].reshape(1, GROUP_W),
            gm_ln_b[l].reshape(1, GROUP_W), gm_ws_b[l], bsf[l], lay=lay)
        four = _pos_dft(f_lat, f_starts, rev, f_ctx, ab, batch=batch, seq=seq, ctx_len=ctx_len, with_ctx=not last)
        cast_jobs = [(w, 0) for w in big_weights[1:]] if l == 0 else []
        cast_jobs += [] if last else [(w, l + 1) for w in big_weights]
        attn, attn_ctx, casts = _attention(qt, k, vt, q_norm_g[l], k_norm_g[l], cast_jobs, batch=batch, seq=seq,
                                           ctx_len=ctx_len, with_ctx_queries=not last)
        attn_src = (attn, attn, False) if last else (attn, attn_ctx, True)
        if l == 0:
            w_out_b, w_up_b, w_down_b = (w[None] for w in casts[:3])
        next_weights = casts[-len(big_weights):]
        n_rows = n_lat if last else n_lat + lay["n_ctx"]
        xall = _out_proj(attn_src, four, cb, t, gm, conv_w8[l], w_out_b, rows_src, modt, l, lay=lay, n_rows=n_rows)
        xall = _ffn(xall, modt, l, norm2_g[l].reshape(1, d), w_up_b, ffn_conv_w8[l],
                    ffn_conv_b[l].reshape(1, d_ff), w_down_b, fg, lay=lay_ffn, n_rows=n_rows, final_norm=last)
        rows_src = (xall, xall, False)
        if not last:
            w_in_b, w_out_b, w_up_b, w_down_b = (w[None] for w in next_weights)
    return xall.reshape(batch, seq, d)
```

```python
import functools
import math

import jax
import jax.numpy as jnp
from jax import lax
from jax.experimental import pallas as pl
from jax.experimental.pallas import tpu as pltpu

F32 = jnp.float32
BF16 = jnp.bfloat16

GRID_W = 64
HEAD_DIM = 128
N_HEADS = 8
N_KV_HEADS = 2
Q_PER_KV = N_HEADS // N_KV_HEADS
ATTN_W = N_HEADS * HEAD_DIM
KV_W = N_KV_HEADS * HEAD_DIM
ROPE_THETA = 10000.0
ROPE_AXIS_DIM = HEAD_DIM // 2
ATTN_SCALE = HEAD_DIM ** -0.5
Q_SCALE = ATTN_SCALE * math.log2(math.e)
GROUP_W = 4 * HEAD_DIM
CHUNK = 128
N_MOD = 6
EPS = 1e-6

OFF_K = ATTN_W
OFF_V = OFF_K + KV_W
OFF_F = OFF_V + KV_W
OFF_CB = OFF_F + GROUP_W
OFF_CC = OFF_CB + GROUP_W
OFF_CH = OFF_CC + GROUP_W
OFF_GU = OFF_CH + GROUP_W
OFF_GV = OFF_GU + GROUP_W
IN_W = OFF_GV + GROUP_W
MIX_W = ATTN_W + 3 * GROUP_W

V7X_VMEM_LIMIT_BYTES = 56 * 1024 * 1024
V7X_VMEM_FFN_LIMIT_BYTES = 61 * 1024 * 1024
SUBLANES = 8
MOD_ROWS = 8
ROW_TILE = 512
FFN_ROW_TILE = 1024
FF_CHUNK = 512
FFN_FIRST_BLOCKS = 4
ATTN_Q_TILE = 512
KV_CHUNK = 1024
CAST_STRIP = 128
BF16_SUBLANES = 16
NORM_ROWS = BF16_SUBLANES
VT_ROWS = HEAD_DIM + BF16_SUBLANES
MAX_FIXED_SHIFT = 60.0
DFT_ROWS = 1024
DFT_FOLD_BLOCK = 256


def _params(n_axes, vmem_limit_bytes=V7X_VMEM_LIMIT_BYTES):
    return pltpu.CompilerParams(dimension_semantics=("arbitrary",) * n_axes, vmem_limit_bytes=vmem_limit_bytes)


def _resident(shape):
    return pl.BlockSpec(shape, lambda *_: (0,) * len(shape), pipeline_mode=pl.Buffered(1))


def _resident_layer(stacked_shape):
    rest = tuple(stacked_shape[1:])
    assert stacked_shape[0] == 1
    return pl.BlockSpec((None,) + rest, lambda *_: (0,) * (1 + len(rest)), pipeline_mode=pl.Buffered(1))


def _stream_specs(tm, d, n_lat_tiles, split):
    if not split:
        return pl.BlockSpec((tm, d), lambda i, *_: (i, 0)), pl.BlockSpec((SUBLANES, d), lambda i, *_: (0, 0))
    lat = pl.BlockSpec((tm, d), lambda i, *_: (jnp.minimum(i, n_lat_tiles - 1), 0))
    ctx = pl.BlockSpec((tm, d), lambda i, *_: (jnp.maximum(i - n_lat_tiles, 0), 0))
    return lat, ctx


def _stream_rows(tile, n_lat_tiles, split, rows_ref, ctx_ref):
    return jnp.where(tile < n_lat_tiles, rows_ref[...], ctx_ref[...]) if split else rows_ref[...]


def _dot(a, b):
    return jnp.dot(a, b, preferred_element_type=F32)


def _gelu_tanh(x):
    return 0.5 * x * (1.0 + jnp.tanh(math.sqrt(2.0 / math.pi) * (x + 0.044715 * (x * x * x))))


def _silu(x):
    return x * (1.0 / (1.0 + jnp.exp(-x)))


def _rms(x, eps=EPS):
    return x * lax.rsqrt(jnp.mean(x * x, axis=-1, keepdims=True) + eps)


def _split_bf16(x):
    hi = x.astype(BF16)
    return hi, (x - hi.astype(F32)).astype(BF16)


def _mod_kernel(c_ref, w_ref, b_ref, o_ref):
    s_hi, s_lo = _split_bf16(_silu(c_ref[...]))
    w_hi, w_lo = _split_bf16(w_ref[...])
    r = _dot(jnp.concatenate([s_hi, s_lo], axis=0), w_hi)
    o_ref[...] = r[:MOD_ROWS] + r[MOD_ROWS:] + _dot(s_hi, w_lo) + b_ref[...]


def _modulation(cs, w_mod, b_mod):
    depth, d, n = w_mod.shape
    tn = next(t for t in (2048, 1024, 512, 256, 128) if n % t == 0)
    return pl.pallas_call(
        _mod_kernel,
        grid=(depth, n // tn),
        in_specs=[pl.BlockSpec((MOD_ROWS, d), lambda l, j: (0, 0)),
                  pl.BlockSpec((None, d, tn), lambda l, j: (l, 0, j)),
                  pl.BlockSpec((None, 1, tn), lambda l, j: (l, 0, j))],
        out_specs=pl.BlockSpec((None, MOD_ROWS, tn), lambda l, j: (l, 0, j)),
        out_shape=jax.ShapeDtypeStruct((depth, MOD_ROWS, n), F32),
        compiler_params=_params(2),
        name="modulation",
    )(cs, w_mod, b_mod.reshape(depth, 1, n))


def _in_kernel(xl_ref, xc_ref, m_ref, g1_ref, w_ref, qg_ref, kg_ref, cos_ref, sin_ref, dftc_ref,
               lng_ref, lnb_ref, ws_ref, bs_ref,
               qt_ref, k_ref, vt_ref, ab_ref, cb_ref, t_ref, gm_ref, *, n_lat_tiles, split_rows):
    tm = xl_ref.shape[0]
    x = _stream_rows(pl.program_id(0), n_lat_tiles, split_rows, xl_ref, xc_ref)
    h = _rms(x) * g1_ref[...]
    h = h * (1.0 + m_ref[1:2, :]) + m_ref[0:1, :]
    hb = h.astype(BF16)

    def proj(lo, width=GROUP_W):
        return _dot(hb, w_ref[:, lo:lo + width])

    cos = cos_ref[...]
    sin = sin_ref[...]
    lane = lax.broadcasted_iota(jnp.int32, (tm, HEAD_DIM), 1)
    first_half = (lane % (ROPE_AXIS_DIM)) < (ROPE_AXIS_DIM // 2)

    def norm_rope(ph, gain):
        y = _rms(ph) * gain
        partner = jnp.where(first_half,
                            pltpu.roll(y, HEAD_DIM - ROPE_AXIS_DIM // 2, 1),
                            pltpu.roll(y, ROPE_AXIS_DIM // 2, 1))
        return y * cos + partner * sin

    def queries():
        qg = qg_ref[...] * Q_SCALE
        for half in range(ATTN_W // GROUP_W):
            p = proj(half * GROUP_W)
            for hh in range(GROUP_W // HEAD_DIM):
                c0 = half * GROUP_W + hh * HEAD_DIM
                y = norm_rope(p[:, hh * HEAD_DIM:(hh + 1) * HEAD_DIM], qg)
                qt_ref[c0:c0 + HEAD_DIM, :] = y.T.astype(BF16)

    def keys_values():
        p = proj(OFF_K)
        kg = kg_ref[...]
        for hh in range(N_KV_HEADS):
            cols = slice(hh * HEAD_DIM, (hh + 1) * HEAD_DIM)
            k_ref[:, cols] = norm_rope(p[:, cols], kg).astype(BF16)
            r0 = hh * VT_ROWS
            vt_ref[r0:r0 + HEAD_DIM, :] = p[:, KV_W + hh * HEAD_DIM:KV_W + (hh + 1) * HEAD_DIM].T.astype(BF16)
            vt_ref[r0 + HEAD_DIM:r0 + VT_ROWS, :] = jnp.ones((BF16_SUBLANES, tm), BF16)

    def fourier():
        p = proj(OFF_F).astype(BF16)
        dftc = dftc_ref[...]
        for g in range(GROUP_W // HEAD_DIM):
            r = _dot(p[:, g * HEAD_DIM:(g + 1) * HEAD_DIM], dftc)
            ab_ref[:, g * HEAD_DIM:(g + 1) * HEAD_DIM] = r[:, :HEAD_DIM].astype(BF16)
            ab_ref[:, GROUP_W + g * HEAD_DIM:GROUP_W + (g + 1) * HEAD_DIM] = r[:, HEAD_DIM:].astype(BF16)

    def conv_gates():
        t_ref[...] = (proj(OFF_CC) * proj(OFF_CH)).astype(BF16)
        cb_ref[...] = proj(OFF_CB).astype(BF16)

    def spatial_gating():
        u = _gelu_tanh(proj(OFF_GU))
        gv = _gelu_tanh(proj(OFF_GV))
        gc = gv - jnp.mean(gv, axis=-1, keepdims=True)
        vn = gc * lax.rsqrt(jnp.mean(gc * gc, axis=-1, keepdims=True) + EPS) * lng_ref[...] + lnb_ref[...]
        vn = vn.astype(BF16)
        for g in range(GROUP_W // HEAD_DIM):
            wsg = ws_ref[g]
            bsg = bs_ref[g]
            for c in range(tm // CHUNK):
                rows = slice(c * CHUNK, (c + 1) * CHUNK)
                cols = slice(g * HEAD_DIM, (g + 1) * HEAD_DIM)
                s = _dot(wsg, vn[rows, cols]) + bsg
                gm_ref[rows, cols] = (u[rows, cols] * s).astype(BF16)

    spatial_gating()
    queries()
    keys_values()
    fourier()
    conv_gates()


def _in_proj(rows_src, modt, layer, g1, w_in, qg, kg, cos_t, sin_t, dftc, lng, lnb, ws, bsf, *, lay):
    x_lat, x_ctx, split_rows = rows_src
    r_rows, d = lay["n_lat"] + lay["n_ctx"], x_lat.shape[1]
    tm = lay["tm"]
    grp, nlt, tps = lay["grp"], lay["n_lat_tiles"], lay["tiles_per_seq"]
    row = lambda w: pl.BlockSpec((tm, w), lambda i: (i, 0))
    vec = lambda w: pl.BlockSpec((1, w), lambda i: (0, 0))
    tab = pl.BlockSpec((tm, HEAD_DIM), lambda i: (jnp.where(i < nlt, i % tps, tps), 0))
    col = lambda h: pl.BlockSpec((h, tm), lambda i: (0, i))
    rows_bf16 = lambda w: jax.ShapeDtypeStruct((r_rows, w), BF16)
    cols_bf16 = lambda h: jax.ShapeDtypeStruct((h, r_rows), BF16)
    return pl.pallas_call(
        functools.partial(_in_kernel, n_lat_tiles=nlt, split_rows=split_rows),
        grid=(r_rows // tm,),
        in_specs=[*_stream_specs(tm, d, nlt, split_rows),
                  pl.BlockSpec((None, MOD_ROWS, d), lambda i: (layer, 0, grp(i))),
                  vec(d), _resident_layer(w_in.shape), vec(HEAD_DIM), vec(HEAD_DIM), tab, tab,
                  _resident(dftc.shape), vec(GROUP_W), vec(GROUP_W),
                  _resident(ws.shape), _resident(bsf.shape)],
        out_specs=[col(ATTN_W), row(KV_W), col(N_KV_HEADS * VT_ROWS), row(2 * GROUP_W), row(GROUP_W), row(GROUP_W),
                   row(GROUP_W)],
        out_shape=[cols_bf16(ATTN_W), rows_bf16(KV_W), cols_bf16(N_KV_HEADS * VT_ROWS), rows_bf16(2 * GROUP_W),
                   rows_bf16(GROUP_W), rows_bf16(GROUP_W), rows_bf16(GROUP_W)],
        compiler_params=_params(1),
        name="in_proj",
    )(x_lat, x_ctx, modt, g1, w_in, qg, kg, cos_t, sin_t, dftc, lng, lnb, ws, bsf)


def _dft_kernel(fc_ref, fs_ref, fkc_ref, fks_ref, ab_ref, rev_ref, cc_ref, cs_ref, abc_ref, o_ref, rhs_ref, y_ref,
                *, n_lat_steps, per_seq, scale):
    step = pl.program_id(0)
    n = ab_ref.shape[0]
    half = n // 2
    rb = rev_ref.shape[0]
    n_fold = half // rb

    @pl.when(jnp.logical_and(step < n_lat_steps, step % per_seq == 0))
    def _():
        n_blocks = n // rb
        rows_iota = lax.broadcasted_iota(jnp.int32, (rb, 1), 0)
        row0 = rows_iota == 0
        for j in range(n_fold):
            src = n_blocks - 1 - j
            wrap = ((n_blocks - j) % n_blocks) * rb
            mirrored = _dot(rev_ref[...], ab_ref[src * rb:(src + 1) * rb, :])
            first = ab_ref[wrap:wrap + BF16_SUBLANES, :][0:1, :].astype(F32)
            mirrored = jnp.where(row0, first, mirrored)
            own = ab_ref[j * rb:(j + 1) * rb, :].astype(F32)
            rhs_ref[j * rb:(j + 1) * rb, :] = (own[:, :GROUP_W] + mirrored[:, :GROUP_W]).astype(BF16)
            rhs_ref[half + j * rb:half + (j + 1) * rb, :] = (own[:, GROUP_W:] - mirrored[:, GROUP_W:]).astype(BF16)

        a_0 = ab_ref[0:BF16_SUBLANES, :GROUP_W][0:1, :].astype(F32)
        a_half = ab_ref[half:half + BF16_SUBLANES, :GROUP_W][0:1, :].astype(F32)
        sign = (1 - 2 * (rows_iota & 1)).astype(F32)
        edge = scale * (sign * a_half - a_0)
        starts = (_dot(fkc_ref[...], rhs_ref[:half, :]) - _dot(fks_ref[...], rhs_ref[half:, :])
                  + scale * (a_half - a_0))
        p = _dot(fc_ref[...], rhs_ref[:half, :])
        neg_q = _dot(fs_ref[...], rhs_ref[half:, :])
        for j in range(n_fold):
            rows = slice(j * rb, (j + 1) * rb)
            y_ref[rows, :] = (p[rows] + neg_q[rows] + edge).astype(BF16)
            upper = _dot(rev_ref[...], (p[rows] - neg_q[rows] + edge).astype(BF16))
            upper = jnp.where(row0, starts[j:j + 1, :], upper)
            dst = half + (n_fold - 1 - j) * rb
            y_ref[dst:dst + rb, :] = upper.astype(BF16)

    @pl.when(step < n_lat_steps)
    def _():
        tmd = o_ref.shape[0]
        o_ref[...] = y_ref[pl.ds(pl.multiple_of((step % per_seq) * tmd, tmd), tmd), :]

    @pl.when(step >= n_lat_steps)
    def _():
        n_ctx = cc_ref.shape[0]
        for r0 in range(0, o_ref.shape[0], n_ctx):
            rows = slice(r0, r0 + n_ctx)
            y = _dot(cc_ref[...], abc_ref[rows, :GROUP_W]) + _dot(cs_ref[...], abc_ref[rows, GROUP_W:])
            o_ref[rows, :] = y.astype(BF16)


def _pos_dft(f_lat, f_starts, rev, f_ctx, ab, *, batch, seq, ctx_len, with_ctx):
    tmd = min(DFT_ROWS, seq)
    while (batch * ctx_len) % tmd:
        tmd //= 2
    rb = rev.shape[0]
    assert tmd % ctx_len == 0 and seq % tmd == 0 and (seq // 2) % rb == 0 and rb % 2 == 0
    assert seq // 2 // rb <= f_starts[0].shape[0]
    per_seq = seq // tmd
    n_lat_steps = batch * per_seq
    n_ctx_steps = batch * ctx_len // tmd if with_ctx else 0
    lat = lambda s: s < n_lat_steps
    ab_seq = lambda s: jnp.minimum((s + per_seq - 1) // per_seq, batch - 1)
    return pl.pallas_call(
        functools.partial(_dft_kernel, n_lat_steps=n_lat_steps, per_seq=per_seq, scale=seq ** -0.5),
        grid=(n_lat_steps + n_ctx_steps,),
        in_specs=[_resident(f_lat[0].shape), _resident(f_lat[1].shape),
                  _resident(f_starts[0].shape), _resident(f_starts[1].shape),
                  pl.BlockSpec((seq, 2 * GROUP_W), lambda s: (ab_seq(s), 0)),
                  _resident(rev.shape), _resident(f_ctx[0].shape), _resident(f_ctx[1].shape),
                  pl.BlockSpec((tmd, 2 * GROUP_W), lambda s: (jnp.where(lat(s), n_lat_steps, s), 0))],
        out_specs=pl.BlockSpec((tmd, GROUP_W), lambda s: (s, 0)),
        out_shape=jax.ShapeDtypeStruct(((n_lat_steps + n_ctx_steps) * tmd, GROUP_W), BF16),
        scratch_shapes=[pltpu.VMEM((seq, GROUP_W), BF16), pltpu.VMEM((seq, GROUP_W), BF16)],
        compiler_params=_params(1),
        name="pos_dft",
    )(*f_lat, *f_starts, ab, rev, *f_ctx, ab)


def _attn_kernel(shift_ref, qt_ref, qtc_ref, kl_ref, kc_ref, vtl_ref, vtc_ref, *rest, with_ctx_queries,
                 fixed_shift, cast_blocks):
    n_cast = len(cast_blocks)
    n_out = 2 if with_ctx_queries else 1
    o_ref = rest[n_cast]
    step = (pl.program_id(0) * pl.num_programs(1) + pl.program_id(1)) * pl.num_programs(2) + pl.program_id(2)
    for src_ref, dst_ref, n_blocks in zip(rest[:n_cast], rest[n_cast + n_out:], cast_blocks):
        @pl.when(step < n_blocks)
        def _(src_ref=src_ref, dst_ref=dst_ref):
            dst_ref[...] = src_ref[...].astype(BF16)

    seq = kl_ref.shape[0]

    kv_chunk = KV_CHUNK if fixed_shift else KV_CHUNK // 4

    def key_chunks(with_latent_keys):
        chunks = [(kc_ref, vtc_ref, slice(None))]
        if with_latent_keys:
            chunks += [(kl_ref, vtl_ref, slice(j * kv_chunk, (j + 1) * kv_chunk)) for j in range(seq // kv_chunk)]
        return chunks

    def attend(q_ref, out_ref, with_latent_keys):
        tq = q_ref.shape[1]
        cols = Q_PER_KV * tq
        qt = jnp.concatenate([q_ref[g * HEAD_DIM:(g + 1) * HEAD_DIM, :] for g in range(Q_PER_KV)], axis=1)
        if fixed_shift:
            shift = shift_ref[0, 0]
            acc = None
            for k_ref, vt_ref, ks in key_chunks(with_latent_keys):
                p = jnp.exp2(_dot(k_ref[ks, :], qt) - shift).astype(BF16)
                part = _dot(vt_ref[:, ks], p)
                acc = part if acc is None else acc + part
            l, acc = acc[HEAD_DIM:HEAD_DIM + 1], acc[:HEAD_DIM]
        else:
            m = jnp.full((1, cols), -jnp.inf, F32)
            l = jnp.zeros((1, cols), F32)
            acc = jnp.zeros((HEAD_DIM, cols), F32)
            for k_ref, vt_ref, ks in key_chunks(with_latent_keys):
                s = _dot(k_ref[ks, :], qt)
                m_new = jnp.maximum(m, jnp.max(s, axis=0, keepdims=True))
                alpha = jnp.exp2(m - m_new)
                p = jnp.exp2(s - m_new)
                l = alpha * l + jnp.sum(p, axis=0, keepdims=True)
                acc = alpha * acc + _dot(vt_ref[:HEAD_DIM, ks], p.astype(BF16))
                m = m_new
        o = acc * (1.0 / l)
        for g in range(Q_PER_KV):
            out_ref[:, g * HEAD_DIM:(g + 1) * HEAD_DIM] = o[:, g * tq:(g + 1) * tq].T.astype(BF16)

    attend(qt_ref, o_ref, True)
    if with_ctx_queries:
        pl.when(pl.program_id(2) == 0)(lambda: attend(qtc_ref, rest[n_cast + 1], False))


def _attention(qt, k, vt, q_gain, k_gain, cast_jobs, *, batch, seq, ctx_len, with_ctx_queries):
    cast_weights = [w for w, _ in cast_jobs]
    cast_layers = [layer for _, layer in cast_jobs]
    tq = min(ATTN_Q_TILE, seq)
    assert seq % tq == 0
    nq = seq // tq
    ctx_blk0 = batch * seq // ctx_len
    qw = Q_PER_KV * HEAD_DIM
    n_i = nq
    grid = (batch, N_KV_HEADS, n_i)

    n_steps = math.prod(grid)
    widths = [next(wd for wd in range(CAST_STRIP, w.shape[2] + 1, CAST_STRIP)
                   if w.shape[2] % wd == 0 and w.shape[2] // wd <= n_steps) for w in cast_weights]
    cast_blocks = tuple(w.shape[2] // wd for w, wd in zip(cast_weights, widths))
    strip = lambda nb: lambda b, hg, i: jnp.minimum((b * N_KV_HEADS + hg) * n_i + i, nb - 1)
    cast_in = [pl.BlockSpec((None, w.shape[1], wd), lambda b, hg, i, f=strip(nb), layer=layer: (layer, 0, f(b, hg, i)))
               for w, wd, nb, layer in zip(cast_weights, widths, cast_blocks, cast_layers)]
    cast_out = [pl.BlockSpec((w.shape[1], wd), lambda b, hg, i, f=strip(nb): (0, f(b, hg, i)))
                for w, wd, nb in zip(cast_weights, widths, cast_blocks)]
    cast_shapes = [jax.ShapeDtypeStruct(w.shape[1:], BF16) for w in cast_weights]

    ctx_out_specs = [pl.BlockSpec((ctx_len, qw), lambda b, hg, i: (b, hg))] if with_ctx_queries else []
    ctx_out_shapes = [jax.ShapeDtypeStruct((batch * ctx_len, ATTN_W), BF16)] if with_ctx_queries else []

    def call(fixed_shift, shift):
        return pl.pallas_call(
            functools.partial(_attn_kernel, with_ctx_queries=with_ctx_queries, fixed_shift=fixed_shift,
                              cast_blocks=cast_blocks),
            grid=grid,
            in_specs=[pl.BlockSpec(memory_space=pltpu.SMEM),
                      pl.BlockSpec((qw, tq), lambda b, hg, i: (hg, b * nq + i)),
                      pl.BlockSpec((qw, ctx_len), lambda b, hg, i: (hg, ctx_blk0 + b)),
                      pl.BlockSpec((seq, HEAD_DIM), lambda b, hg, i: (b, hg)),
                      pl.BlockSpec((ctx_len, HEAD_DIM), lambda b, hg, i: (ctx_blk0 + b, hg)),
                      pl.BlockSpec((VT_ROWS, seq), lambda b, hg, i: (hg, b)),
                      pl.BlockSpec((VT_ROWS, ctx_len), lambda b, hg, i: (hg, ctx_blk0 + b))] + cast_in,
            out_specs=[pl.BlockSpec((tq, qw), lambda b, hg, i: (b * nq + i, hg))] + ctx_out_specs + cast_out,
            out_shape=[jax.ShapeDtypeStruct((batch * seq, ATTN_W), BF16)] + ctx_out_shapes + cast_shapes,
            compiler_params=_params(3),
            name="attention_fixed" if fixed_shift else "attention_online",
        )(shift, qt, qt, k, k, vt, vt, *cast_weights)

    bound = (HEAD_DIM * Q_SCALE * (1.0 + 2.0 ** -6)) * jnp.max(jnp.abs(q_gain)) * jnp.max(jnp.abs(k_gain))
    shift = bound.astype(F32).reshape(1, 1)
    out = lax.cond(bound <= MAX_FIXED_SHIFT, lambda: call(True, shift), lambda: call(False, shift))
    n_attn = 1 + len(ctx_out_specs)
    return out[0], (out[1] if with_ctx_queries else None), out[n_attn:]


def _seq_edge_masks(tile, tm, n_lat_tiles, seq, ctx_len):
    r = lax.broadcasted_iota(jnp.int32, (tm, 1), 0)
    period = jnp.where(tile < n_lat_tiles, seq, ctx_len)
    pos = (tile * tm + r) & (period - 1)
    return pos == 0, pos == period - 1


def _dwconv3(center, prev_row, next_row, w_ref, is_start, is_end):
    tm = center.shape[0]
    r = lax.broadcasted_iota(jnp.int32, (tm, 1), 0)
    up = jnp.where(r == 0, prev_row, pltpu.roll(center, 1, 0))
    dn = jnp.where(r == tm - 1, next_row, pltpu.roll(center, tm - 1, 0))
    up = jnp.where(is_start, 0.0, up)
    dn = jnp.where(is_end, 0.0, dn)
    return up * w_ref[0:1, :] + center * w_ref[1:2, :] + dn * w_ref[2:3, :]


def _out_kernel(al_ref, ac_ref, four_ref, cb_ref, t_ref, tp_ref, tn_ref, gm_ref, cw_ref, w_ref, xl_ref, xc_ref,
                m_ref, o_ref, mix_ref, *, n_lat_tiles, seq, ctx_len, split_rows, split_attn):
    tm = xl_ref.shape[0]
    i = pl.program_id(0)
    is_start, is_end = _seq_edge_masks(i, tm, n_lat_tiles, seq, ctx_len)
    conv = _dwconv3(t_ref[...].astype(F32), tp_ref[SUBLANES - 1:SUBLANES, :].astype(F32),
                    tn_ref[0:1, :].astype(F32), cw_ref, is_start, is_end)
    mix_ref[:, :ATTN_W] = _stream_rows(i, n_lat_tiles, split_attn, al_ref, ac_ref)
    mix_ref[:, ATTN_W:ATTN_W + GROUP_W] = four_ref[...]
    mix_ref[:, ATTN_W + GROUP_W:ATTN_W + 2 * GROUP_W] = (cb_ref[...].astype(F32) * conv).astype(BF16)
    mix_ref[:, ATTN_W + 2 * GROUP_W:] = gm_ref[...]
    x = _stream_rows(i, n_lat_tiles, split_rows, xl_ref, xc_ref)
    o_ref[...] = x + m_ref[2:3, :] * _dot(mix_ref[...], w_ref[...])


def _halo_specs(tm, width, n_rows):
    per = tm // SUBLANES
    last = n_rows // SUBLANES - 1
    prev = pl.BlockSpec((SUBLANES, width), lambda i, *_: (jnp.maximum(i * per - 1, 0), 0))
    nxt = pl.BlockSpec((SUBLANES, width), lambda i, *_: (jnp.minimum((i + 1) * per, last), 0))
    return prev, nxt


def _out_proj(attn_src, four, cb, t, gm, conv_w, w_out, rows_src, modt, layer, *, lay, n_rows):
    attn_lat, attn_ctx, split_attn = attn_src
    x_lat, x_ctx, split_rows = rows_src
    d = x_lat.shape[1]
    tm = lay["tm"]
    grp = lay["grp"]
    row = lambda w: pl.BlockSpec((tm, w), lambda i: (i, 0))
    tprev, tnext = _halo_specs(tm, GROUP_W, t.shape[0])
    kern = functools.partial(_out_kernel, n_lat_tiles=lay["n_lat_tiles"], seq=lay["seq"], ctx_len=lay["ctx_len"],
                             split_rows=split_rows, split_attn=split_attn)
    return pl.pallas_call(
        kern,
        grid=(n_rows // tm,),
        in_specs=[*_stream_specs(tm, ATTN_W, lay["n_lat_tiles"], split_attn), row(GROUP_W), row(GROUP_W), row(GROUP_W), tprev, tnext, row(GROUP_W),
                  _resident(conv_w.shape), _resident_layer(w_out.shape),
                  *_stream_specs(tm, d, lay["n_lat_tiles"], split_rows),
                  pl.BlockSpec((None, MOD_ROWS, d), lambda i: (layer, 0, grp(i)))],
        out_specs=row(d),
        out_shape=jax.ShapeDtypeStruct((n_rows, d), F32),
        scratch_shapes=[pltpu.VMEM((tm, MIX_W), BF16)],
        compiler_params=_params(1),
        name="out_proj",
    )(attn_lat, attn_ctx, four, cb, t, t, t, gm, conv_w, w_out, x_lat, x_ctx, modt)


def _ffn_kernel(x_ref, xp_ref, xn_ref, m_ref, g2_ref, wg_ref, wu_ref, cw_ref, cb_ref, wd_ref, fg_ref,
                o_ref, h_ref, *, n_lat_tiles, seq, ctx_len, final_norm):
    tm = x_ref.shape[0]
    i = pl.program_id(0)
    j = pl.program_id(1)

    def chunk(n_blocks, first):
        rs = tm // n_blocks
        ends = [(b + 1) * rs for b in range(n_blocks - 1)] + [tm + 2 * SUBLANES]
        g = jnp.concatenate([_dot(h_ref[b * rs:e, :], wg_ref[...]) for b, e in enumerate(ends)], axis=0)
        u = jnp.concatenate([_dot(h_ref[b * rs:(b + 1) * rs, :], wu_ref[...]) for b in range(n_blocks)], axis=0)
        is_start, is_end = _seq_edge_masks(i, tm, n_lat_tiles, seq, ctx_len)
        conv = _dwconv3(g[0:tm], g[tm + 2 * SUBLANES - 1:tm + 2 * SUBLANES], g[tm:tm + 1], cw_ref,
                        is_start, is_end)
        act = (_silu(conv + cb_ref[...]) * u).astype(BF16)
        part = _dot(act, wd_ref[...])
        o_ref[...] = part if first else o_ref[...] + part

    @pl.when(j == 0)
    def _():
        gain = g2_ref[...] * (1.0 + m_ref[4:5, :])
        shift = m_ref[3:4, :]

        def norm_mod(x):
            return (_rms(x) * gain + shift).astype(BF16)

        h_ref[tm:tm + SUBLANES, :] = norm_mod(xn_ref[...])
        h_ref[tm + SUBLANES:, :] = norm_mod(xp_ref[...])
        for r0 in range(0, tm, NORM_ROWS):
            h_ref[r0:r0 + NORM_ROWS, :] = norm_mod(x_ref[r0:r0 + NORM_ROWS, :])
        chunk(FFN_FIRST_BLOCKS, first=True)

    @pl.when(j > 0)
    def _():
        chunk(1, first=False)

    @pl.when(j == pl.num_programs(1) - 1)
    def _():
        gate = m_ref[5:6, :]
        for r0 in range(0, tm, NORM_ROWS):
            rows = slice(r0, r0 + NORM_ROWS)
            y = x_ref[rows, :] + gate * o_ref[rows, :]
            if final_norm:
                y = _rms(y) * fg_ref[...]
            o_ref[rows, :] = y


def _ffn(xall, modt, layer, g2, w_up, conv_w, conv_b, w_down, final_g, *, lay, n_rows, final_norm):
    d = xall.shape[1]
    d_ff = w_down.shape[1]
    tm = lay["tm"]
    grp = lay["grp"]
    nj = d_ff // FF_CHUNK
    xprev, xnext = _halo_specs(tm, d, xall.shape[0])
    kern = functools.partial(_ffn_kernel, n_lat_tiles=lay["n_lat_tiles"], seq=lay["seq"],
                             ctx_len=lay["ctx_len"], final_norm=final_norm)
    return pl.pallas_call(
        kern,
        grid=(n_rows // tm, nj),
        in_specs=[pl.BlockSpec((tm, d), lambda i, j: (i, 0)), xprev, xnext,
                  pl.BlockSpec((None, MOD_ROWS, d), lambda i, j: (layer, 0, grp(i))),
                  pl.BlockSpec((1, d), lambda i, j: (0, 0)),
                  pl.BlockSpec((None, d, FF_CHUNK), lambda i, j: (0, 0, j)),
                  pl.BlockSpec((None, d, FF_CHUNK), lambda i, j: (0, 0, nj + j)),
                  pl.BlockSpec((SUBLANES, FF_CHUNK), lambda i, j: (0, j)),
                  pl.BlockSpec((1, FF_CHUNK), lambda i, j: (0, j)),
                  pl.BlockSpec((None, FF_CHUNK, d), lambda i, j: (0, j, 0)),
                  pl.BlockSpec((1, d), lambda i, j: (0, 0))],
        out_specs=pl.BlockSpec((tm, d), lambda i, j: (i, 0)),
        out_shape=jax.ShapeDtypeStruct((n_rows, d), F32),
        scratch_shapes=[pltpu.VMEM((tm + 2 * SUBLANES, d), BF16)],
        compiler_params=_params(2, V7X_VMEM_FFN_LIMIT_BYTES),
        name="ffn",
    )(xall, xall, xall, modt, g2, w_up, w_up, conv_w, conv_b, w_down, final_g)


def _rope_tables(seq, tm):
    pos = jnp.arange(seq, dtype=jnp.int32)
    row = (pos // GRID_W).astype(F32)
    col = (pos % GRID_W).astype(F32)
    freqs = ROPE_THETA ** (-jnp.arange(0, ROPE_AXIS_DIM, 2, dtype=F32) / ROPE_AXIS_DIM)
    ang_r = row[:, None] * freqs[None, :]
    ang_c = col[:, None] * freqs[None, :]
    cos = jnp.concatenate([jnp.cos(ang_r)] * 2 + [jnp.cos(ang_c)] * 2, axis=-1)
    sin = jnp.concatenate([-jnp.sin(ang_r), jnp.sin(ang_r), -jnp.sin(ang_c), jnp.sin(ang_c)], axis=-1)
    cos = jnp.concatenate([cos, jnp.ones((tm, HEAD_DIM), F32)], axis=0)
    sin = jnp.concatenate([sin, jnp.zeros((tm, HEAD_DIM), F32)], axis=0)
    return cos, sin


def _dft_matrix(n, scale, n_kt=None):
    n_t = n_k = n if n_kt is None else n_kt
    n1 = 1
    while n1 * n1 < n:
        n1 *= 2
    n2 = n // n1
    assert n_k % n2 == 0
    t = jnp.arange(n_t, dtype=jnp.int32)[None, :]
    a = jnp.arange(n_k // n2, dtype=jnp.int32)[:, None]
    b = jnp.arange(n2, dtype=jnp.int32)[:, None]
    ang_a = ((a * t) % n1).astype(F32) * (2.0 * math.pi / n1)
    ang_b = ((b * t) % n).astype(F32) * (2.0 * math.pi / n)
    ca, sa = jnp.cos(ang_a)[:, None, :], jnp.sin(ang_a)[:, None, :]
    cb, sb = jnp.cos(ang_b)[None, :, :] * scale, jnp.sin(ang_b)[None, :, :] * scale
    c = (ca * cb - sa * sb).reshape(n_k, n_t)
    neg_s = ((-sa) * cb - ca * sb).reshape(n_k, n_t)
    return c.astype(BF16), neg_s.astype(BF16)


def _dft_rows(n, scale, ks, n_t):
    k = jnp.asarray(list(ks) + [0] * (SUBLANES - len(ks)), jnp.int32)[:, None]
    valid = (jnp.arange(SUBLANES) < len(ks))[:, None]
    ang = ((k * jnp.arange(n_t, dtype=jnp.int32)[None, :]) % n).astype(F32) * (2.0 * math.pi / n)
    return (jnp.where(valid, jnp.cos(ang) * scale, 0.0).astype(BF16),
            jnp.where(valid, -jnp.sin(ang) * scale, 0.0).astype(BF16))


def _row_reversal(rb):
    r = jnp.arange(rb, dtype=jnp.int32)
    return (r[None, :] == (rb - r)[:, None]).astype(BF16)


def _channel_dft(scale):
    k = jnp.arange(HEAD_DIM, dtype=jnp.int32)
    ang = ((k[:, None] * k[None, :]) % HEAD_DIM).astype(F32) * (2.0 * math.pi / HEAD_DIM)
    return (jnp.concatenate([jnp.cos(ang), jnp.sin(ang)], axis=1) * scale).astype(BF16)


def _layout(batch, seq, ctx_len, max_tm):
    n_lat, n_ctx = batch * seq, batch * ctx_len
    tm = max_tm
    while n_ctx % tm or seq % tm:
        tm //= 2
    assert tm >= CHUNK and seq & (seq - 1) == 0 and ctx_len & (ctx_len - 1) == 0
    assert seq % GRID_W == 0 and ctx_len % CHUNK == 0 and n_lat % ctx_len == 0 and n_lat % seq == 0
    n_lat_tiles, tps = n_lat // tm, seq // tm
    grp = lambda i: jnp.where(i < n_lat_tiles, 1 + i // tps, 0)
    return dict(tm=tm, n_lat_tiles=n_lat_tiles, tiles_per_seq=tps, grp=grp, seq=seq, ctx_len=ctx_len,
                n_lat=n_lat, n_ctx=n_ctx)


def kernel(x, c, ctx, c_ctx, w_mod, b_mod, norm1_g, norm2_g, w_in, q_norm_g, k_norm_g, conv_w, gm_ln_g,
           gm_ln_b, gm_ws, gm_b, w_out, w_up, ffn_conv_w, ffn_conv_b, w_down, final_norm_g):
    batch, seq, d = x.shape
    ctx_len = ctx.shape[1]
    depth = w_mod.shape[0]
    d_ff = w_down.shape[1]
    assert batch + 1 <= MOD_ROWS and w_in.shape[2] == IN_W and w_down.shape[1] % FF_CHUNK == 0
    lay = _layout(batch, seq, ctx_len, ROW_TILE)
    lay_ffn = _layout(batch, seq, ctx_len, FFN_ROW_TILE)
    n_lat, tm = lay["n_lat"], lay["tm"]

    rows_src = (x.reshape(n_lat, d), ctx.reshape(batch * ctx_len, d), True)

    cs = jnp.concatenate([c_ctx[None, :], c, jnp.zeros((MOD_ROWS - 1 - batch, d), F32)], axis=0)
    mod = _modulation(cs, w_mod, b_mod)
    modt = mod.reshape(depth, MOD_ROWS, N_MOD, d).transpose(0, 2, 1, 3).reshape(depth, N_MOD, MOD_ROWS * d)
    modt = jnp.pad(modt, ((0, 0), (0, MOD_ROWS - N_MOD), (0, 0)))

    cos_t, sin_t = _rope_tables(seq, tm)
    f_lat = _dft_matrix(seq, seq ** -0.5, seq // 2)
    fold_starts = range(DFT_FOLD_BLOCK, seq // 2 + 1, DFT_FOLD_BLOCK)
    f_starts = _dft_rows(seq, seq ** -0.5, fold_starts, seq // 2)
    f_ctx = _dft_matrix(ctx_len, ctx_len ** -0.5)
    rev = _row_reversal(DFT_FOLD_BLOCK)
    dftc = _channel_dft(HEAD_DIM ** -0.5)

    pad_rows = lambda w: jnp.pad(w, ((0, 0), (0, SUBLANES - w.shape[1]), (0, 0)))
    conv_w8 = pad_rows(conv_w)
    ffn_conv_w8 = pad_rows(ffn_conv_w)
    bsf = jnp.broadcast_to(gm_b[..., None], gm_b.shape + (HEAD_DIM,))
    gm_ws_b = gm_ws.astype(BF16)
    fg = final_norm_g.reshape(1, d)

    big_weights = (w_in, w_out, w_up, w_down)
    w_in_b = w_in[:1].astype(BF16)

    for l in range(depth):
        last = l == depth - 1
        qt, k, vt, ab, cb, t, gm = _in_proj(
            rows_src, modt, l, norm1_g[l].reshape(1, d), w_in_b, q_norm_g[l].reshape(1, HEAD_DIM),
            k_norm_g[l].reshape(1, HEAD_DIM), cos_t, sin_t, dftc, gm_ln_g[l].reshape(1, GROUP_W),
            gm_ln_b[l].reshape(1, GROUP_W), gm_ws_b[l], bsf[l], lay=lay)
        four = _pos_dft(f_lat, f_starts, rev, f_ctx, ab, batch=batch, seq=seq, ctx_len=ctx_len, with_ctx=not last)
        cast_jobs = [(w, 0) for w in big_weights[1:]] if l == 0 else []
        cast_jobs += [] if last else [(w, l + 1) for w in big_weights]
        attn, attn_ctx, casts = _attention(qt, k, vt, q_norm_g[l], k_norm_g[l], cast_jobs, batch=batch, seq=seq,
                                           ctx_len=ctx_len, with_ctx_queries=not last)
        attn_src = (attn, attn, False) if last else (attn, attn_ctx, True)
        if l == 0:
            w_out_b, w_up_b, w_down_b = (w[None] for w in casts[:3])
        next_weights = casts[-len(big_weights):]
        n_rows = n_lat if last else n_lat + lay["n_ctx"]
        xall = _out_proj(attn_src, four, cb, t, gm, conv_w8[l], w_out_b, rows_src, modt, l, lay=lay, n_rows=n_rows)
        xall = _ffn(xall, modt, l, norm2_g[l].reshape(1, d), w_up_b, ffn_conv_w8[l],
                    ffn_conv_b[l].reshape(1, d_ff), w_down_b, fg, lay=lay_ffn, n_rows=n_rows, final_norm=last)
        rows_src = (xall, xall, False)
        if not last:
            w_in_b, w_out_b, w_up_b, w_down_b = (w[None] for w in next_weights)
    return xall.reshape(batch, seq, d)
```

```python
import functools
import math

import jax
import jax.numpy as jnp
from jax import lax
from jax.experimental import pallas as pl
from jax.experimental.pallas import tpu as pltpu

F32 = jnp.float32
BF16 = jnp.bfloat16

GRID_W = 64
HEAD_DIM = 128
N_HEADS = 8
N_KV_HEADS = 2
Q_PER_KV = N_HEADS // N_KV_HEADS
ATTN_W = N_HEADS * HEAD_DIM
KV_W = N_KV_HEADS * HEAD_DIM
ROPE_THETA = 10000.0
ROPE_AXIS_DIM = HEAD_DIM // 2
ATTN_SCALE = HEAD_DIM ** -0.5
Q_SCALE = ATTN_SCALE * math.log2(math.e)
GROUP_W = 4 * HEAD_DIM
CHUNK = 128
N_MOD = 6
EPS = 1e-6

OFF_K = ATTN_W
OFF_V = OFF_K + KV_W
OFF_F = OFF_V + KV_W
OFF_CB = OFF_F + GROUP_W
OFF_CC = OFF_CB + GROUP_W
OFF_CH = OFF_CC + GROUP_W
OFF_GU = OFF_CH + GROUP_W
OFF_GV = OFF_GU + GROUP_W
IN_W = OFF_GV + GROUP_W
MIX_W = ATTN_W + 3 * GROUP_W

V7X_VMEM_LIMIT_BYTES = 56 * 1024 * 1024
V7X_VMEM_FFN_LIMIT_BYTES = 61 * 1024 * 1024
SUBLANES = 8
MOD_ROWS = 8
ROW_TILE = 512
FFN_ROW_TILE = 1024
FF_CHUNK = 512
FFN_FIRST_BLOCKS = 4
ATTN_Q_TILE = 512
KV_CHUNK = 1024
CAST_STRIP = 128
BF16_SUBLANES = 16
NORM_ROWS = BF16_SUBLANES
MAX_FIXED_SHIFT = 60.0
DFT_ROWS = 1024
DFT_FOLD_BLOCK = 256


def _params(n_axes, vmem_limit_bytes=V7X_VMEM_LIMIT_BYTES):
    return pltpu.CompilerParams(dimension_semantics=("arbitrary",) * n_axes, vmem_limit_bytes=vmem_limit_bytes)


def _resident(shape):
    return pl.BlockSpec(shape, lambda *_: (0,) * len(shape), pipeline_mode=pl.Buffered(1))


def _resident_layer(stacked_shape):
    rest = tuple(stacked_shape[1:])
    assert stacked_shape[0] == 1
    return pl.BlockSpec((None,) + rest, lambda *_: (0,) * (1 + len(rest)), pipeline_mode=pl.Buffered(1))


def _stream_specs(tm, d, n_lat_tiles, split):
    if not split:
        return pl.BlockSpec((tm, d), lambda i, *_: (i, 0)), pl.BlockSpec((SUBLANES, d), lambda i, *_: (0, 0))
    lat = pl.BlockSpec((tm, d), lambda i, *_: (jnp.minimum(i, n_lat_tiles - 1), 0))
    ctx = pl.BlockSpec((tm, d), lambda i, *_: (jnp.maximum(i - n_lat_tiles, 0), 0))
    return lat, ctx


def _stream_rows(tile, n_lat_tiles, split, rows_ref, ctx_ref):
    return jnp.where(tile < n_lat_tiles, rows_ref[...], ctx_ref[...]) if split else rows_ref[...]


def _dot(a, b):
    return jnp.dot(a, b, preferred_element_type=F32)


def _gelu_tanh(x):
    return 0.5 * x * (1.0 + jnp.tanh(math.sqrt(2.0 / math.pi) * (x + 0.044715 * (x * x * x))))


def _silu(x):
    return x * (1.0 / (1.0 + jnp.exp(-x)))


def _rms(x, eps=EPS):
    return x * lax.rsqrt(jnp.mean(x * x, axis=-1, keepdims=True) + eps)


def _split_bf16(x):
    hi = x.astype(BF16)
    return hi, (x - hi.astype(F32)).astype(BF16)


def _mod_kernel(c_ref, w_ref, b_ref, o_ref):
    s_hi, s_lo = _split_bf16(_silu(c_ref[...]))
    w_hi, w_lo = _split_bf16(w_ref[...])
    r = _dot(jnp.concatenate([s_hi, s_lo], axis=0), w_hi)
    o_ref[...] = r[:MOD_ROWS] + r[MOD_ROWS:] + _dot(s_hi, w_lo) + b_ref[...]


def _modulation(cs, w_mod, b_mod):
    depth, d, n = w_mod.shape
    tn = next(t for t in (2048, 1024, 512, 256, 128) if n % t == 0)
    return pl.pallas_call(
        _mod_kernel,
        grid=(depth, n // tn),
        in_specs=[pl.BlockSpec((MOD_ROWS, d), lambda l, j: (0, 0)),
                  pl.BlockSpec((None, d, tn), lambda l, j: (l, 0, j)),
                  pl.BlockSpec((None, 1, tn), lambda l, j: (l, 0, j))],
        out_specs=pl.BlockSpec((None, MOD_ROWS, tn), lambda l, j: (l, 0, j)),
        out_shape=jax.ShapeDtypeStruct((depth, MOD_ROWS, n), F32),
        compiler_params=_params(2),
        name="modulation",
    )(cs, w_mod, b_mod.reshape(depth, 1, n))


def _in_kernel(xl_ref, xc_ref, m_ref, g1_ref, w_ref, qg_ref, kg_ref, cos_ref, sin_ref, dftc_ref,
               lng_ref, lnb_ref, ws_ref, bs_ref,
               qt_ref, k_ref, vt_ref, ab_ref, cb_ref, t_ref, gm_ref, *, n_lat_tiles, split_rows):
    tm = xl_ref.shape[0]
    x = _stream_rows(pl.program_id(0), n_lat_tiles, split_rows, xl_ref, xc_ref)
    h = _rms(x) * g1_ref[...]
    h = h * (1.0 + m_ref[1:2, :]) + m_ref[0:1, :]
    hb = h.astype(BF16)

    def proj(lo, width=GROUP_W):
        return _dot(hb, w_ref[:, lo:lo + width])

    cos = cos_ref[...]
    sin = sin_ref[...]
    lane = lax.broadcasted_iota(jnp.int32, (tm, HEAD_DIM), 1)
    first_half = (lane % (ROPE_AXIS_DIM)) < (ROPE_AXIS_DIM // 2)

    def norm_rope(ph, gain):
        y = _rms(ph) * gain
        partner = jnp.where(first_half,
                            pltpu.roll(y, HEAD_DIM - ROPE_AXIS_DIM // 2, 1),
                            pltpu.roll(y, ROPE_AXIS_DIM // 2, 1))
        return y * cos + partner * sin

    def queries():
        qg = qg_ref[...] * Q_SCALE
        for half in range(ATTN_W // GROUP_W):
            p = proj(half * GROUP_W)
            for hh in range(GROUP_W // HEAD_DIM):
                c0 = half * GROUP_W + hh * HEAD_DIM
                y = norm_rope(p[:, hh * HEAD_DIM:(hh + 1) * HEAD_DIM], qg)
                qt_ref[c0:c0 + HEAD_DIM, :] = y.T.astype(BF16)

    def keys_values():
        p = proj(OFF_K)
        kg = kg_ref[...]
        for hh in range(N_KV_HEADS):
            cols = slice(hh * HEAD_DIM, (hh + 1) * HEAD_DIM)
            k_ref[:, cols] = norm_rope(p[:, cols], kg).astype(BF16)
            vt_ref[cols, :] = p[:, KV_W + hh * HEAD_DIM:KV_W + (hh + 1) * HEAD_DIM].T.astype(BF16)

    def fourier():
        p = proj(OFF_F).astype(BF16)
        dftc = dftc_ref[...]
        for g in range(GROUP_W // HEAD_DIM):
            r = _dot(p[:, g * HEAD_DIM:(g + 1) * HEAD_DIM], dftc)
            ab_ref[:, g * HEAD_DIM:(g + 1) * HEAD_DIM] = r[:, :HEAD_DIM].astype(BF16)
            ab_ref[:, GROUP_W + g * HEAD_DIM:GROUP_W + (g + 1) * HEAD_DIM] = r[:, HEAD_DIM:].astype(BF16)

    def conv_gates():
        t_ref[...] = (proj(OFF_CC) * proj(OFF_CH)).astype(BF16)
        cb_ref[...] = proj(OFF_CB).astype(BF16)

    def spatial_gating():
        u = _gelu_tanh(proj(OFF_GU))
        gv = _gelu_tanh(proj(OFF_GV))
        gc = gv - jnp.mean(gv, axis=-1, keepdims=True)
        vn = gc * lax.rsqrt(jnp.mean(gc * gc, axis=-1, keepdims=True) + EPS) * lng_ref[...] + lnb_ref[...]
        vn = vn.astype(BF16)
        for g in range(GROUP_W // HEAD_DIM):
            wsg = ws_ref[g]
            bsg = bs_ref[g]
            for c in range(tm // CHUNK):
                rows = slice(c * CHUNK, (c + 1) * CHUNK)
                cols = slice(g * HEAD_DIM, (g + 1) * HEAD_DIM)
                s = _dot(wsg, vn[rows, cols]) + bsg
                gm_ref[rows, cols] = (u[rows, cols] * s).astype(BF16)

    spatial_gating()
    queries()
    keys_values()
    fourier()
    conv_gates()


def _in_proj(rows_src, modt, layer, g1, w_in, qg, kg, cos_t, sin_t, dftc, lng, lnb, ws, bsf, *, lay):
    x_lat, x_ctx, split_rows = rows_src
    r_rows, d = lay["n_lat"] + lay["n_ctx"], x_lat.shape[1]
    tm = lay["tm"]
    grp, nlt, tps = lay["grp"], lay["n_lat_tiles"], lay["tiles_per_seq"]
    row = lambda w: pl.BlockSpec((tm, w), lambda i: (i, 0))
    vec = lambda w: pl.BlockSpec((1, w), lambda i: (0, 0))
    tab = pl.BlockSpec((tm, HEAD_DIM), lambda i: (jnp.where(i < nlt, i % tps, tps), 0))
    col = lambda h: pl.BlockSpec((h, tm), lambda i: (0, i))
    rows_bf16 = lambda w: jax.ShapeDtypeStruct((r_rows, w), BF16)
    cols_bf16 = lambda h: jax.ShapeDtypeStruct((h, r_rows), BF16)
    return pl.pallas_call(
        functools.partial(_in_kernel, n_lat_tiles=nlt, split_rows=split_rows),
        grid=(r_rows // tm,),
        in_specs=[*_stream_specs(tm, d, nlt, split_rows),
                  pl.BlockSpec((None, MOD_ROWS, d), lambda i: (layer, 0, grp(i))),
                  vec(d), _resident_layer(w_in.shape), vec(HEAD_DIM), vec(HEAD_DIM), tab, tab,
                  _resident(dftc.shape), vec(GROUP_W), vec(GROUP_W),
                  _resident(ws.shape), _resident(bsf.shape)],
        out_specs=[col(ATTN_W), row(KV_W), col(KV_W), row(2 * GROUP_W), row(GROUP_W), row(GROUP_W),
                   row(GROUP_W)],
        out_shape=[cols_bf16(ATTN_W), rows_bf16(KV_W), cols_bf16(KV_W), rows_bf16(2 * GROUP_W),
                   rows_bf16(GROUP_W), rows_bf16(GROUP_W), rows_bf16(GROUP_W)],
        compiler_params=_params(1),
        name="in_proj",
    )(x_lat, x_ctx, modt, g1, w_in, qg, kg, cos_t, sin_t, dftc, lng, lnb, ws, bsf)


def _dft_kernel(fc_ref, fs_ref, fkc_ref, fks_ref, ab_ref, rev_ref, cc_ref, cs_ref, abc_ref, o_ref, rhs_ref, y_ref,
                *, n_lat_steps, per_seq, scale):
    step = pl.program_id(0)
    n = ab_ref.shape[0]
    half = n // 2
    rb = rev_ref.shape[0]
    n_fold = half // rb

    @pl.when(jnp.logical_and(step < n_lat_steps, step % per_seq == 0))
    def _():
        n_blocks = n // rb
        rows_iota = lax.broadcasted_iota(jnp.int32, (rb, 1), 0)
        row0 = rows_iota == 0
        for j in range(n_fold):
            src = n_blocks - 1 - j
            wrap = ((n_blocks - j) % n_blocks) * rb
            mirrored = _dot(rev_ref[...], ab_ref[src * rb:(src + 1) * rb, :])
            first = ab_ref[wrap:wrap + BF16_SUBLANES, :][0:1, :].astype(F32)
            mirrored = jnp.where(row0, first, mirrored)
            own = ab_ref[j * rb:(j + 1) * rb, :].astype(F32)
            rhs_ref[j * rb:(j + 1) * rb, :] = (own[:, :GROUP_W] + mirrored[:, :GROUP_W]).astype(BF16)
            rhs_ref[half + j * rb:half + (j + 1) * rb, :] = (own[:, GROUP_W:] - mirrored[:, GROUP_W:]).astype(BF16)

        a_0 = ab_ref[0:BF16_SUBLANES, :GROUP_W][0:1, :].astype(F32)
        a_half = ab_ref[half:half + BF16_SUBLANES, :GROUP_W][0:1, :].astype(F32)
        sign = (1 - 2 * (rows_iota & 1)).astype(F32)
        edge = scale * (sign * a_half - a_0)
        starts = (_dot(fkc_ref[...], rhs_ref[:half, :]) - _dot(fks_ref[...], rhs_ref[half:, :])
                  + scale * (a_half - a_0))
        p = _dot(fc_ref[...], rhs_ref[:half, :])
        neg_q = _dot(fs_ref[...], rhs_ref[half:, :])
        for j in range(n_fold):
            rows = slice(j * rb, (j + 1) * rb)
            y_ref[rows, :] = (p[rows] + neg_q[rows] + edge).astype(BF16)
            upper = _dot(rev_ref[...], (p[rows] - neg_q[rows] + edge).astype(BF16))
            upper = jnp.where(row0, starts[j:j + 1, :], upper)
            dst = half + (n_fold - 1 - j) * rb
            y_ref[dst:dst + rb, :] = upper.astype(BF16)

    @pl.when(step < n_lat_steps)
    def _():
        tmd = o_ref.shape[0]
        o_ref[...] = y_ref[pl.ds(pl.multiple_of((step % per_seq) * tmd, tmd), tmd), :]

    @pl.when(step >= n_lat_steps)
    def _():
        n_ctx = cc_ref.shape[0]
        for r0 in range(0, o_ref.shape[0], n_ctx):
            rows = slice(r0, r0 + n_ctx)
            y = _dot(cc_ref[...], abc_ref[rows, :GROUP_W]) + _dot(cs_ref[...], abc_ref[rows, GROUP_W:])
            o_ref[rows, :] = y.astype(BF16)


def _pos_dft(f_lat, f_starts, rev, f_ctx, ab, *, batch, seq, ctx_len, with_ctx):
    tmd = min(DFT_ROWS, seq)
    while (batch * ctx_len) % tmd:
        tmd //= 2
    rb = rev.shape[0]
    assert tmd % ctx_len == 0 and seq % tmd == 0 and (seq // 2) % rb == 0 and rb % 2 == 0
    assert seq // 2 // rb <= f_starts[0].shape[0]
    per_seq = seq // tmd
    n_lat_steps = batch * per_seq
    n_ctx_steps = batch * ctx_len // tmd if with_ctx else 0
    lat = lambda s: s < n_lat_steps
    ab_seq = lambda s: jnp.minimum((s + per_seq - 1) // per_seq, batch - 1)
    return pl.pallas_call(
        functools.partial(_dft_kernel, n_lat_steps=n_lat_steps, per_seq=per_seq, scale=seq ** -0.5),
        grid=(n_lat_steps + n_ctx_steps,),
        in_specs=[_resident(f_lat[0].shape), _resident(f_lat[1].shape),
                  _resident(f_starts[0].shape), _resident(f_starts[1].shape),
                  pl.BlockSpec((seq, 2 * GROUP_W), lambda s: (ab_seq(s), 0)),
                  _resident(rev.shape), _resident(f_ctx[0].shape), _resident(f_ctx[1].shape),
                  pl.BlockSpec((tmd, 2 * GROUP_W), lambda s: (jnp.where(lat(s), n_lat_steps, s), 0))],
        out_specs=pl.BlockSpec((tmd, GROUP_W), lambda s: (s, 0)),
        out_shape=jax.ShapeDtypeStruct(((n_lat_steps + n_ctx_steps) * tmd, GROUP_W), BF16),
        scratch_shapes=[pltpu.VMEM((seq, GROUP_W), BF16), pltpu.VMEM((seq, GROUP_W), BF16)],
        compiler_params=_params(1),
        name="pos_dft",
    )(*f_lat, *f_starts, ab, rev, *f_ctx, ab)


def _attn_kernel(shift_ref, qt_ref, qtc_ref, kl_ref, kc_ref, vtl_ref, vtc_ref, *rest, with_ctx_queries,
                 fixed_shift, cast_blocks):
    n_cast = len(cast_blocks)
    n_out = 2 if with_ctx_queries else 1
    o_ref = rest[n_cast]
    step = (pl.program_id(0) * pl.num_programs(1) + pl.program_id(1)) * pl.num_programs(2) + pl.program_id(2)
    for src_ref, dst_ref, n_blocks in zip(rest[:n_cast], rest[n_cast + n_out:], cast_blocks):
        @pl.when(step < n_blocks)
        def _(src_ref=src_ref, dst_ref=dst_ref):
            dst_ref[...] = src_ref[...].astype(BF16)

    seq = kl_ref.shape[0]

    kv_chunk = KV_CHUNK if fixed_shift else KV_CHUNK // 4

    def key_chunks(with_latent_keys):
        chunks = [(kc_ref, vtc_ref, slice(None))]
        if with_latent_keys:
            chunks += [(kl_ref, vtl_ref, slice(j * kv_chunk, (j + 1) * kv_chunk)) for j in range(seq // kv_chunk)]
        return chunks

    def attend(q_ref, out_ref, with_latent_keys):
        tq = q_ref.shape[1]
        cols = Q_PER_KV * tq
        qt = jnp.concatenate([q_ref[g * HEAD_DIM:(g + 1) * HEAD_DIM, :] for g in range(Q_PER_KV)], axis=1)
        if fixed_shift:
            shift = shift_ref[0, 0]
            acc = None
            for k_ref, vt_ref, ks in key_chunks(with_latent_keys):
                p = jnp.exp2(_dot(k_ref[ks, :], qt) - shift)
                part = _dot(vt_ref[:, ks], p.astype(BF16))
                lpart = jnp.sum(p, axis=0, keepdims=True)
                acc, l = (part, lpart) if acc is None else (acc + part, l + lpart)
        else:
            m = jnp.full((1, cols), -jnp.inf, F32)
            l = jnp.zeros((1, cols), F32)
            acc = jnp.zeros((HEAD_DIM, cols), F32)
            for k_ref, vt_ref, ks in key_chunks(with_latent_keys):
                s = _dot(k_ref[ks, :], qt)
                m_new = jnp.maximum(m, jnp.max(s, axis=0, keepdims=True))
                alpha = jnp.exp2(m - m_new)
                p = jnp.exp2(s - m_new)
                l = alpha * l + jnp.sum(p, axis=0, keepdims=True)
                acc = alpha * acc + _dot(vt_ref[:, ks], p.astype(BF16))
                m = m_new
        o = acc * (1.0 / l)
        for g in range(Q_PER_KV):
            out_ref[:, g * HEAD_DIM:(g + 1) * HEAD_DIM] = o[:, g * tq:(g + 1) * tq].T.astype(BF16)

    attend(qt_ref, o_ref, True)
    if with_ctx_queries:
        pl.when(pl.program_id(2) == 0)(lambda: attend(qtc_ref, rest[n_cast + 1], False))


def _attention(qt, k, vt, q_gain, k_gain, cast_jobs, *, batch, seq, ctx_len, with_ctx_queries):
    cast_weights = [w for w, _ in cast_jobs]
    cast_layers = [layer for _, layer in cast_jobs]
    tq = min(ATTN_Q_TILE, seq)
    assert seq % tq == 0
    nq = seq // tq
    ctx_blk0 = batch * seq // ctx_len
    qw = Q_PER_KV * HEAD_DIM
    n_i = nq
    grid = (batch, N_KV_HEADS, n_i)

    n_steps = math.prod(grid)
    widths = [next(wd for wd in range(CAST_STRIP, w.shape[2] + 1, CAST_STRIP)
                   if w.shape[2] % wd == 0 and w.shape[2] // wd <= n_steps) for w in cast_weights]
    cast_blocks = tuple(w.shape[2] // wd for w, wd in zip(cast_weights, widths))
    strip = lambda nb: lambda b, hg, i: jnp.minimum((b * N_KV_HEADS + hg) * n_i + i, nb - 1)
    cast_in = [pl.BlockSpec((None, w.shape[1], wd), lambda b, hg, i, f=strip(nb), layer=layer: (layer, 0, f(b, hg, i)))
               for w, wd, nb, layer in zip(cast_weights, widths, cast_blocks, cast_layers)]
    cast_out = [pl.BlockSpec((w.shape[1], wd), lambda b, hg, i, f=strip(nb): (0, f(b, hg, i)))
                for w, wd, nb in zip(cast_weights, widths, cast_blocks)]
    cast_shapes = [jax.ShapeDtypeStruct(w.shape[1:], BF16) for w in cast_weights]

    ctx_out_specs = [pl.BlockSpec((ctx_len, qw), lambda b, hg, i: (b, hg))] if with_ctx_queries else []
    ctx_out_shapes = [jax.ShapeDtypeStruct((batch * ctx_len, ATTN_W), BF16)] if with_ctx_queries else []

    def call(fixed_shift, shift):
        return pl.pallas_call(
            functools.partial(_attn_kernel, with_ctx_queries=with_ctx_queries, fixed_shift=fixed_shift,
                              cast_blocks=cast_blocks),
            grid=grid,
            in_specs=[pl.BlockSpec(memory_space=pltpu.SMEM),
                      pl.BlockSpec((qw, tq), lambda b, hg, i: (hg, b * nq + i)),
                      pl.BlockSpec((qw, ctx_len), lambda b, hg, i: (hg, ctx_blk0 + b)),
                      pl.BlockSpec((seq, HEAD_DIM), lambda b, hg, i: (b, hg)),
                      pl.BlockSpec((ctx_len, HEAD_DIM), lambda b, hg, i: (ctx_blk0 + b, hg)),
                      pl.BlockSpec((HEAD_DIM, seq), lambda b, hg, i: (hg, b)),
                      pl.BlockSpec((HEAD_DIM, ctx_len), lambda b, hg, i: (hg, ctx_blk0 + b))] + cast_in,
            out_specs=[pl.BlockSpec((tq, qw), lambda b, hg, i: (b * nq + i, hg))] + ctx_out_specs + cast_out,
            out_shape=[jax.ShapeDtypeStruct((batch * seq, ATTN_W), BF16)] + ctx_out_shapes + cast_shapes,
            compiler_params=_params(3),
            name="attention_fixed" if fixed_shift else "attention_online",
        )(shift, qt, qt, k, k, vt, vt, *cast_weights)

    bound = (HEAD_DIM * Q_SCALE * (1.0 + 2.0 ** -6)) * jnp.max(jnp.abs(q_gain)) * jnp.max(jnp.abs(k_gain))
    shift = bound.astype(F32).reshape(1, 1)
    out = lax.cond(bound <= MAX_FIXED_SHIFT, lambda: call(True, shift), lambda: call(False, shift))
    n_attn = 1 + len(ctx_out_specs)
    return out[0], (out[1] if with_ctx_queries else None), out[n_attn:]


def _seq_edge_masks(tile, tm, n_lat_tiles, seq, ctx_len):
    r = lax.broadcasted_iota(jnp.int32, (tm, 1), 0)
    period = jnp.where(tile < n_lat_tiles, seq, ctx_len)
    pos = (tile * tm + r) & (period - 1)
    return pos == 0, pos == period - 1


def _dwconv3(center, prev_row, next_row, w_ref, is_start, is_end):
    tm = center.shape[0]
    r = lax.broadcasted_iota(jnp.int32, (tm, 1), 0)
    up = jnp.where(r == 0, prev_row, pltpu.roll(center, 1, 0))
    dn = jnp.where(r == tm - 1, next_row, pltpu.roll(center, tm - 1, 0))
    up = jnp.where(is_start, 0.0, up)
    dn = jnp.where(is_end, 0.0, dn)
    return up * w_ref[0:1, :] + center * w_ref[1:2, :] + dn * w_ref[2:3, :]


def _out_kernel(al_ref, ac_ref, four_ref, cb_ref, t_ref, tp_ref, tn_ref, gm_ref, cw_ref, w_ref, xl_ref, xc_ref,
                m_ref, o_ref, mix_ref, *, n_lat_tiles, seq, ctx_len, split_rows, split_attn):
    tm = xl_ref.shape[0]
    i = pl.program_id(0)
    is_start, is_end = _seq_edge_masks(i, tm, n_lat_tiles, seq, ctx_len)
    conv = _dwconv3(t_ref[...].astype(F32), tp_ref[SUBLANES - 1:SUBLANES, :].astype(F32),
                    tn_ref[0:1, :].astype(F32), cw_ref, is_start, is_end)
    mix_ref[:, :ATTN_W] = _stream_rows(i, n_lat_tiles, split_attn, al_ref, ac_ref)
    mix_ref[:, ATTN_W:ATTN_W + GROUP_W] = four_ref[...]
    mix_ref[:, ATTN_W + GROUP_W:ATTN_W + 2 * GROUP_W] = (cb_ref[...].astype(F32) * conv).astype(BF16)
    mix_ref[:, ATTN_W + 2 * GROUP_W:] = gm_ref[...]
    x = _stream_rows(i, n_lat_tiles, split_rows, xl_ref, xc_ref)
    o_ref[...] = x + m_ref[2:3, :] * _dot(mix_ref[...], w_ref[...])


def _halo_specs(tm, width, n_rows):
    per = tm // SUBLANES
    last = n_rows // SUBLANES - 1
    prev = pl.BlockSpec((SUBLANES, width), lambda i, *_: (jnp.maximum(i * per - 1, 0), 0))
    nxt = pl.BlockSpec((SUBLANES, width), lambda i, *_: (jnp.minimum((i + 1) * per, last), 0))
    return prev, nxt


def _out_proj(attn_src, four, cb, t, gm, conv_w, w_out, rows_src, modt, layer, *, lay, n_rows):
    attn_lat, attn_ctx, split_attn = attn_src
    x_lat, x_ctx, split_rows = rows_src
    d = x_lat.shape[1]
    tm = lay["tm"]
    grp = lay["grp"]
    row = lambda w: pl.BlockSpec((tm, w), lambda i: (i, 0))
    tprev, tnext = _halo_specs(tm, GROUP_W, t.shape[0])
    kern = functools.partial(_out_kernel, n_lat_tiles=lay["n_lat_tiles"], seq=lay["seq"], ctx_len=lay["ctx_len"],
                             split_rows=split_rows, split_attn=split_attn)
    return pl.pallas_call(
        kern,
        grid=(n_rows // tm,),
        in_specs=[*_stream_specs(tm, ATTN_W, lay["n_lat_tiles"], split_attn), row(GROUP_W), row(GROUP_W), row(GROUP_W), tprev, tnext, row(GROUP_W),
                  _resident(conv_w.shape), _resident_layer(w_out.shape),
                  *_stream_specs(tm, d, lay["n_lat_tiles"], split_rows),
                  pl.BlockSpec((None, MOD_ROWS, d), lambda i: (layer, 0, grp(i)))],
        out_specs=row(d),
        out_shape=jax.ShapeDtypeStruct((n_rows, d), F32),
        scratch_shapes=[pltpu.VMEM((tm, MIX_W), BF16)],
        compiler_params=_params(1),
        name="out_proj",
    )(attn_lat, attn_ctx, four, cb, t, t, t, gm, conv_w, w_out, x_lat, x_ctx, modt)


def _ffn_kernel(x_ref, xp_ref, xn_ref, m_ref, g2_ref, wg_ref, wu_ref, cw_ref, cb_ref, wd_ref, fg_ref,
                o_ref, h_ref, *, n_lat_tiles, seq, ctx_len, final_norm):
    tm = x_ref.shape[0]
    i = pl.program_id(0)
    j = pl.program_id(1)

    def chunk(n_blocks, first):
        rs = tm // n_blocks
        ends = [(b + 1) * rs for b in range(n_blocks - 1)] + [tm + 2 * SUBLANES]
        g = jnp.concatenate([_dot(h_ref[b * rs:e, :], wg_ref[...]) for b, e in enumerate(ends)], axis=0)
        u = jnp.concatenate([_dot(h_ref[b * rs:(b + 1) * rs, :], wu_ref[...]) for b in range(n_blocks)], axis=0)
        is_start, is_end = _seq_edge_masks(i, tm, n_lat_tiles, seq, ctx_len)
        conv = _dwconv3(g[0:tm], g[tm + 2 * SUBLANES - 1:tm + 2 * SUBLANES], g[tm:tm + 1], cw_ref,
                        is_start, is_end)
        act = (_silu(conv + cb_ref[...]) * u).astype(BF16)
        part = _dot(act, wd_ref[...])
        o_ref[...] = part if first else o_ref[...] + part

    @pl.when(j == 0)
    def _():
        gain = g2_ref[...] * (1.0 + m_ref[4:5, :])
        shift = m_ref[3:4, :]

        def norm_mod(x):
            return (_rms(x) * gain + shift).astype(BF16)

        h_ref[tm:tm + SUBLANES, :] = norm_mod(xn_ref[...])
        h_ref[tm + SUBLANES:, :] = norm_mod(xp_ref[...])
        for r0 in range(0, tm, NORM_ROWS):
            h_ref[r0:r0 + NORM_ROWS, :] = norm_mod(x_ref[r0:r0 + NORM_ROWS, :])
        chunk(FFN_FIRST_BLOCKS, first=True)

    @pl.when(j > 0)
    def _():
        chunk(1, first=False)

    @pl.when(j == pl.num_programs(1) - 1)
    def _():
        gate = m_ref[5:6, :]
        for r0 in range(0, tm, NORM_ROWS):
            rows = slice(r0, r0 + NORM_ROWS)
            y = x_ref[rows, :] + gate * o_ref[rows, :]
            if final_norm:
                y = _rms(y) * fg_ref[...]
            o_ref[rows, :] = y


def _ffn(xall, modt, layer, g2, w_up, conv_w, conv_b, w_down, final_g, *, lay, n_rows, final_norm):
    d = xall.shape[1]
    d_ff = w_down.shape[1]
    tm = lay["tm"]
    grp = lay["grp"]
    nj = d_ff // FF_CHUNK
    xprev, xnext = _halo_specs(tm, d, xall.shape[0])
    kern = functools.partial(_ffn_kernel, n_lat_tiles=lay["n_lat_tiles"], seq=lay["seq"],
                             ctx_len=lay["ctx_len"], final_norm=final_norm)
    return pl.pallas_call(
        kern,
        grid=(n_rows // tm, nj),
        in_specs=[pl.BlockSpec((tm, d), lambda i, j: (i, 0)), xprev, xnext,
                  pl.BlockSpec((None, MOD_ROWS, d), lambda i, j: (layer, 0, grp(i))),
                  pl.BlockSpec((1, d), lambda i, j: (0, 0)),
                  pl.BlockSpec((None, d, FF_CHUNK), lambda i, j: (0, 0, j)),
                  pl.BlockSpec((None, d, FF_CHUNK), lambda i, j: (0, 0, nj + j)),
                  pl.BlockSpec((SUBLANES, FF_CHUNK), lambda i, j: (0, j)),
                  pl.BlockSpec((1, FF_CHUNK), lambda i, j: (0, j)),
                  pl.BlockSpec((None, FF_CHUNK, d), lambda i, j: (0, j, 0)),
                  pl.BlockSpec((1, d), lambda i, j: (0, 0))],
        out_specs=pl.BlockSpec((tm, d), lambda i, j: (i, 0)),
        out_shape=jax.ShapeDtypeStruct((n_rows, d), F32),
        scratch_shapes=[pltpu.VMEM((tm + 2 * SUBLANES, d), BF16)],
        compiler_params=_params(2, V7X_VMEM_FFN_LIMIT_BYTES),
        name="ffn",
    )(xall, xall, xall, modt, g2, w_up, w_up, conv_w, conv_b, w_down, final_g)


def _rope_tables(seq, tm):
    pos = jnp.arange(seq, dtype=jnp.int32)
    row = (pos // GRID_W).astype(F32)
    col = (pos % GRID_W).astype(F32)
    freqs = ROPE_THETA ** (-jnp.arange(0, ROPE_AXIS_DIM, 2, dtype=F32) / ROPE_AXIS_DIM)
    ang_r = row[:, None] * freqs[None, :]
    ang_c = col[:, None] * freqs[None, :]
    cos = jnp.concatenate([jnp.cos(ang_r)] * 2 + [jnp.cos(ang_c)] * 2, axis=-1)
    sin = jnp.concatenate([-jnp.sin(ang_r), jnp.sin(ang_r), -jnp.sin(ang_c), jnp.sin(ang_c)], axis=-1)
    cos = jnp.concatenate([cos, jnp.ones((tm, HEAD_DIM), F32)], axis=0)
    sin = jnp.concatenate([sin, jnp.zeros((tm, HEAD_DIM), F32)], axis=0)
    return cos, sin


def _dft_matrix(n, scale, n_kt=None):
    n_t = n_k = n if n_kt is None else n_kt
    n1 = 1
    while n1 * n1 < n:
        n1 *= 2
    n2 = n // n1
    assert n_k % n2 == 0
    t = jnp.arange(n_t, dtype=jnp.int32)[None, :]
    a = jnp.arange(n_k // n2, dtype=jnp.int32)[:, None]
    b = jnp.arange(n2, dtype=jnp.int32)[:, None]
    ang_a = ((a * t) % n1).astype(F32) * (2.0 * math.pi / n1)
    ang_b = ((b * t) % n).astype(F32) * (2.0 * math.pi / n)
    ca, sa = jnp.cos(ang_a)[:, None, :], jnp.sin(ang_a)[:, None, :]
    cb, sb = jnp.cos(ang_b)[None, :, :] * scale, jnp.sin(ang_b)[None, :, :] * scale
    c = (ca * cb - sa * sb).reshape(n_k, n_t)
    neg_s = ((-sa) * cb - ca * sb).reshape(n_k, n_t)
    return c.astype(BF16), neg_s.astype(BF16)


def _dft_rows(n, scale, ks, n_t):
    k = jnp.asarray(list(ks) + [0] * (SUBLANES - len(ks)), jnp.int32)[:, None]
    valid = (jnp.arange(SUBLANES) < len(ks))[:, None]
    ang = ((k * jnp.arange(n_t, dtype=jnp.int32)[None, :]) % n).astype(F32) * (2.0 * math.pi / n)
    return (jnp.where(valid, jnp.cos(ang) * scale, 0.0).astype(BF16),
            jnp.where(valid, -jnp.sin(ang) * scale, 0.0).astype(BF16))


def _row_reversal(rb):
    r = jnp.arange(rb, dtype=jnp.int32)
    return (r[None, :] == (rb - r)[:, None]).astype(BF16)


def _channel_dft(scale):
    k = jnp.arange(HEAD_DIM, dtype=jnp.int32)
    ang = ((k[:, None] * k[None, :]) % HEAD_DIM).astype(F32) * (2.0 * math.pi / HEAD_DIM)
    return (jnp.concatenate([jnp.cos(ang), jnp.sin(ang)], axis=1) * scale).astype(BF16)


def _layout(batch, seq, ctx_len, max_tm):
    n_lat, n_ctx = batch * seq, batch * ctx_len
    tm = max_tm
    while n_ctx % tm or seq % tm:
        tm //= 2
    assert tm >= CHUNK and seq & (seq - 1) == 0 and ctx_len & (ctx_len - 1) == 0
    assert seq % GRID_W == 0 and ctx_len % CHUNK == 0 and n_lat % ctx_len == 0 and n_lat % seq == 0
    n_lat_tiles, tps = n_lat // tm, seq // tm
    grp = lambda i: jnp.where(i < n_lat_tiles, 1 + i // tps, 0)
    return dict(tm=tm, n_lat_tiles=n_lat_tiles, tiles_per_seq=tps, grp=grp, seq=seq, ctx_len=ctx_len,
                n_lat=n_lat, n_ctx=n_ctx)


def kernel(x, c, ctx, c_ctx, w_mod, b_mod, norm1_g, norm2_g, w_in, q_norm_g, k_norm_g, conv_w, gm_ln_g,
           gm_ln_b, gm_ws, gm_b, w_out, w_up, ffn_conv_w, ffn_conv_b, w_down, final_norm_g):
    batch, seq, d = x.shape
    ctx_len = ctx.shape[1]
    depth = w_mod.shape[0]
    d_ff = w_down.shape[1]
    assert batch + 1 <= MOD_ROWS and w_in.shape[2] == IN_W and w_down.shape[1] % FF_CHUNK == 0
    lay = _layout(batch, seq, ctx_len, ROW_TILE)
    lay_ffn = _layout(batch, seq, ctx_len, FFN_ROW_TILE)
    n_lat, tm = lay["n_lat"], lay["tm"]

    rows_src = (x.reshape(n_lat, d), ctx.reshape(batch * ctx_len, d), True)

    cs = jnp.concatenate([c_ctx[None, :], c, jnp.zeros((MOD_ROWS - 1 - batch, d), F32)], axis=0)
    mod = _modulation(cs, w_mod, b_mod)
    modt = mod.reshape(depth, MOD_ROWS, N_MOD, d).transpose(0, 2, 1, 3).reshape(depth, N_MOD, MOD_ROWS * d)
    modt = jnp.pad(modt, ((0, 0), (0, MOD_ROWS - N_MOD), (0, 0)))

    cos_t, sin_t = _rope_tables(seq, tm)
    f_lat = _dft_matrix(seq, seq ** -0.5, seq // 2)
    fold_starts = range(DFT_FOLD_BLOCK, seq // 2 + 1, DFT_FOLD_BLOCK)
    f_starts = _dft_rows(seq, seq ** -0.5, fold_starts, seq // 2)
    f_ctx = _dft_matrix(ctx_len, ctx_len ** -0.5)
    rev = _row_reversal(DFT_FOLD_BLOCK)
    dftc = _channel_dft(HEAD_DIM ** -0.5)

    pad_rows = lambda w: jnp.pad(w, ((0, 0), (0, SUBLANES - w.shape[1]), (0, 0)))
    conv_w8 = pad_rows(conv_w)
    ffn_conv_w8 = pad_rows(ffn_conv_w)
    bsf = jnp.broadcast_to(gm_b[..., None], gm_b.shape + (HEAD_DIM,))
    gm_ws_b = gm_ws.astype(BF16)
    fg = final_norm_g.reshape(1, d)

    big_weights = (w_in, w_out, w_up, w_down)
    w_in_b = w_in[:1].astype(BF16)

    for l in range(depth):
        last = l == depth - 1
        qt, k, vt, ab, cb, t, gm = _in_proj(
            rows_src, modt, l, norm1_g[l].reshape(1, d), w_in_b, q_norm_g[l].reshape(1, HEAD_DIM),
            k_norm_g[l].reshape(1, HEAD_DIM), cos_t, sin_t, dftc, gm_ln_g[l].reshape(1, GROUP_W),
            gm_ln_b[l].reshape(1, GROUP_W), gm_ws_b[l], bsf[l], lay=lay)
        four = _pos_dft(f_lat, f_starts, rev, f_ctx, ab, batch=batch, seq=seq, ctx_len=ctx_len, with_ctx=not last)
        cast_jobs = [(w, 0) for w in big_weights[1:]] if l == 0 else []
        cast_jobs += [] if last else [(w, l + 1) for w in big_weights]
        attn, attn_ctx, casts = _attention(qt, k, vt, q_norm_g[l], k_norm_g[l], cast_jobs, batch=batch, seq=seq,
                                           ctx_len=ctx_len, with_ctx_queries=not last)
        attn_src = (attn, attn, False) if last else (attn, attn_ctx, True)
        if l == 0:
            w_out_b, w_up_b, w_down_b = (w[None] for w in casts[:3])
        next_weights = casts[-len(big_weights):]
        n_rows = n_lat if last else n_lat + lay["n_ctx"]
        xall = _out_proj(attn_src, four, cb, t, gm, conv_w8[l], w_out_b, rows_src, modt, l, lay=lay, n_rows=n_rows)
        xall = _ffn(xall, modt, l, norm2_g[l].reshape(1, d), w_up_b, ffn_conv_w8[l],
                    ffn_conv_b[l].reshape(1, d_ff), w_down_b, fg, lay=lay_ffn, n_rows=n_rows, final_norm=last)
        rows_src = (xall, xall, False)
        if not last:
            w_in_b, w_out_b, w_up_b, w_down_b = (w[None] for w in next_weights)
    return xall.reshape(batch, seq, d)
```
